```python
import math
import jax, jax.numpy as jnp
from jax import lax
import numpy as np

D_MODEL = 1024
BATCH = 8
SEQ = 2048
DEPTH = 4
DEC_BATCH = 128
DEC_SEQ = 1
PAST_LEN = 16384
PAGE_SIZE = 128

N_META = 16
D_MIX = 2 * D_MODEL
D_GDN = D_MODEL // 1
D_SSM = D_MIX - D_GDN
GDN_DK = 128
GDN_DV = 128
GDN_HEADS = D_GDN // GDN_DV
GDN_QK = GDN_HEADS * GDN_DK
GDN_CONV_DIM = 2 * GDN_QK + D_GDN
SSM_HEAD_DIM = 64
SSM_HEADS = D_SSM // SSM_HEAD_DIM
SSM_GROUPS = 2
SSM_STATE = 128
SSM_CONV_DIM = D_SSM + 2 * SSM_GROUPS * SSM_STATE
CONV_W = 4
CHUNK = 64
META_PAD = (-N_META) % CHUNK
EPS = 1e-6
OFF_Z_G = GDN_CONV_DIM
OFF_B = OFF_Z_G + D_GDN
OFF_A = OFF_B + GDN_HEADS
OFF_XBC = OFF_A + GDN_HEADS
OFF_Z_S = OFF_XBC + SSM_CONV_DIM
OFF_DT = OFF_Z_S + D_SSM
N_IN = OFF_DT + SSM_HEADS
SPLITS = (OFF_Z_G, OFF_B, OFF_A, OFF_XBC, OFF_Z_S, OFF_DT)

kernel_name = "hymba_gdn_ssd_decoder_step"


def _rmsnorm(x, w):
    xf = x.astype(jnp.float32)
    y = xf * lax.rsqrt(jnp.mean(xf * xf, -1, keepdims=True) + EPS)
    return (y * w.astype(jnp.float32)).astype(x.dtype)


def _rms(x):
    return x * lax.rsqrt(jnp.mean(x * x, -1, keepdims=True) + EPS)


def _l2norm(x):
    return x * lax.rsqrt(jnp.sum(x * x, -1, keepdims=True) + EPS)


def _causal_conv(x, buf, w, b=None):
    T = x.shape[1]
    xp = jnp.concatenate([buf, x], axis=1)
    y = xp[:, 0:T] * w[0]
    for i in range(1, CONV_W):
        y = y + xp[:, i:i + T] * w[i]
    if b is not None:
        y = y + b
    return jax.nn.silu(y), xp[:, T:]


def _to_chunks(t):
    Bsz, T = t.shape[0], t.shape[1]
    t = t.reshape((Bsz, T // CHUNK, CHUNK) + t.shape[2:])
    return jnp.moveaxis(t, 2, 3)


def _from_chunks(t):
    t = jnp.moveaxis(t, 3, 2)
    return t.reshape((t.shape[0], -1) + t.shape[3:])


def _masks():
    incl = jnp.tril(jnp.ones((CHUNK, CHUNK), dtype=bool))
    strict = jnp.tril(jnp.ones((CHUNK, CHUNK), dtype=bool), -1)
    return incl, strict


def _unit_lower_inverse(a):
    n = -a
    inv = jnp.eye(CHUNK, dtype=a.dtype) + n
    p = n
    for _ in range(int(math.log2(CHUNK)) - 1):
        p = p @ p
        inv = inv + inv @ p
    return inv


def _gdn_chunked(q, k, v, beta, g, s0):
    q, k, v, beta, g = [_to_chunks(t) for t in (q, k, v, beta, g)]
    incl, strict = _masks()
    gc = jnp.cumsum(g, -1)
    decay = jnp.exp(jnp.where(incl, gc[..., :, None] - gc[..., None, :], -jnp.inf))
    kb = k * beta[..., None]
    a = jnp.where(strict, jnp.einsum('bmhik,bmhjk->bmhij', kb, k) * decay, 0.0)
    t_inv = _unit_lower_inverse(a)
    u = jnp.einsum('bmhij,bmhjv->bmhiv', t_inv, v * beta[..., None])
    w = jnp.einsum('bmhij,bmhjk->bmhik', t_inv, kb * jnp.exp(gc)[..., None])
    attn = jnp.einsum('bmhik,bmhjk->bmhij', q, k) * decay
    q_dec = q * jnp.exp(gc)[..., None]
    g_last = gc[..., -1]
    k_dec = k * jnp.exp(g_last[..., None] - gc)[..., None]

    def step(s, xs):
        q_c, k_c, u_c, w_c, at_c, gl_c = xs
        v_new = u_c - jnp.einsum('bhck,bhkv->bhcv', w_c, s)
        o = jnp.einsum('bhck,bhkv->bhcv', q_c, s) + jnp.einsum('bhij,bhjv->bhiv', at_c, v_new)
        s = s * jnp.exp(gl_c)[..., None, None] + jnp.einsum('bhck,bhcv->bhkv', k_c, v_new)
        return s, o

    xs = tuple(jnp.moveaxis(t, 1, 0) for t in (q_dec, k_dec, u, w, attn, g_last))
    s, o = lax.scan(step, s0, xs)
    return _from_chunks(jnp.moveaxis(o, 0, 1)), s


def _gdn_recurrent(q, k, v, beta, g, s0):
    def step(s, xs):
        q_t, k_t, v_t, b_t, g_t = xs
        s = s * jnp.exp(g_t)[..., None, None]
        kv = jnp.einsum('bhk,bhkv->bhv', k_t, s)
        s = s + jnp.einsum('bhk,bhv->bhkv', k_t, (v_t - kv) * b_t[..., None])
        return s, jnp.einsum('bhk,bhkv->bhv', q_t, s)
    xs = tuple(jnp.moveaxis(t, 1, 0) for t in (q, k, v, beta, g))
    s, o = lax.scan(step, s0, xs)
    return jnp.moveaxis(o, 0, 1), s


def _ssd_chunked(x, dt, A, bm, cm, s0):
    xdt = _to_chunks(x * dt[..., None])
    la = _to_chunks(dt * A)
    bm, cm = _to_chunks(bm), _to_chunks(cm)
    incl, _ = _masks()
    ac = jnp.cumsum(la, -1)
    decay = jnp.exp(jnp.where(incl, ac[..., :, None] - ac[..., None, :], -jnp.inf))
    scores = jnp.einsum('bmhis,bmhjs->bmhij', cm, bm) * decay
    y_diag = jnp.einsum('bmhij,bmhjp->bmhip', scores, xdt)
    a_last = ac[..., -1]
    ds = jnp.einsum('bmhcs,bmhcp->bmhps', bm * jnp.exp(a_last[..., None] - ac)[..., None], xdt)
    c_dec = cm * jnp.exp(ac)[..., None]

    def step(s, xs):
        c_c, ds_c, al_c = xs
        y = jnp.einsum('bhcs,bhps->bhcp', c_c, s)
        s = s * jnp.exp(al_c)[..., None, None] + ds_c
        return s, y

    xs = tuple(jnp.moveaxis(t, 1, 0) for t in (c_dec, ds, a_last))
    s, y_off = lax.scan(step, s0, xs)
    return _from_chunks(y_diag + jnp.moveaxis(y_off, 0, 1)), s


def _ssd_recurrent(x, dt, A, bm, cm, s0):
    def step(s, xs):
        x_t, dt_t, b_t, c_t = xs
        s = s * jnp.exp(dt_t * A)[..., None, None] + jnp.einsum('bhp,bhs->bhps', x_t * dt_t[..., None], b_t)
        return s, jnp.einsum('bhps,bhs->bhp', s, c_t)
    xs = tuple(jnp.moveaxis(t, 1, 0) for t in (x, dt, bm, cm))
    s, y = lax.scan(step, s0, xs)
    return jnp.moveaxis(y, 0, 1), s


def _mixer(h, gdn_buf, ssm_buf, s_gdn, s_ssm, w_in, gdn_conv_w, gdn_A_log, gdn_dt_bias, gdn_norm_w,
           ssm_conv_w, ssm_conv_b, ssm_A_log, ssm_dt_bias, ssm_D, ssm_norm_w, w_out, chunked):
    f32 = jnp.float32
    Bsz, T, _ = h.shape
    zp = jnp.einsum('btd,de->bte', h, w_in).astype(f32)
    qkv, z_g, b_g, a_g, xbc, z_s, dt_s = jnp.split(zp, SPLITS, axis=-1)
    qkv, gdn_buf = _causal_conv(qkv, gdn_buf.astype(f32), gdn_conv_w.astype(f32))
    xbc, ssm_buf = _causal_conv(xbc, ssm_buf.astype(f32), ssm_conv_w.astype(f32), ssm_conv_b.astype(f32))
    q, k, v = jnp.split(qkv, [GDN_QK, 2 * GDN_QK], axis=-1)
    q = _l2norm(q.reshape(Bsz, T, GDN_HEADS, GDN_DK)) * (GDN_DK ** -0.5)
    k = _l2norm(k.reshape(Bsz, T, GDN_HEADS, GDN_DK))
    v = v.reshape(Bsz, T, GDN_HEADS, GDN_DV)
    beta = jax.nn.sigmoid(b_g)
    g = -jnp.exp(gdn_A_log.astype(f32)) * jax.nn.softplus(a_g + gdn_dt_bias.astype(f32))
    xs_, bs_, cs_ = jnp.split(xbc, [D_SSM, D_SSM + SSM_GROUPS * SSM_STATE], axis=-1)
    xs_ = xs_.reshape(Bsz, T, SSM_HEADS, SSM_HEAD_DIM)
    rep = SSM_HEADS // SSM_GROUPS
    bs_ = jnp.repeat(bs_.reshape(Bsz, T, SSM_GROUPS, SSM_STATE), rep, axis=2)
    cs_ = jnp.repeat(cs_.reshape(Bsz, T, SSM_GROUPS, SSM_STATE), rep, axis=2)
    dt = jax.nn.softplus(dt_s + ssm_dt_bias.astype(f32))
    A = -jnp.exp(ssm_A_log.astype(f32))
    s_gdn = s_gdn.astype(f32)
    s_ssm = s_ssm.astype(f32)
    if chunked:
        def pad(t):
            return jnp.pad(t, [(0, 0), (META_PAD, 0)] + [(0, 0)] * (t.ndim - 2))
        o_g, s_gdn = _gdn_chunked(pad(q), pad(k), pad(v), pad(beta), pad(g), s_gdn)
        y_s, s_ssm = _ssd_chunked(pad(xs_), pad(dt), A, pad(bs_), pad(cs_), s_ssm)
        o_g, y_s = o_g[:, META_PAD:], y_s[:, META_PAD:]
    else:
        o_g, s_gdn = _gdn_recurrent(q, k, v, beta, g, s_gdn)
        y_s, s_ssm = _ssd_recurrent(xs_, dt, A, bs_, cs_, s_ssm)
    y_s = y_s + ssm_D.astype(f32)[:, None] * xs_
    o_g = _rms(o_g) * gdn_norm_w.astype(f32) * jax.nn.silu(z_g.reshape(Bsz, T, GDN_HEADS, GDN_DV))
    y_s = y_s.reshape(Bsz, T, D_SSM) * jax.nn.silu(z_s)
    y_s = _rms(y_s.reshape(Bsz, T, SSM_GROUPS, D_SSM // SSM_GROUPS)).reshape(Bsz, T, D_SSM) * ssm_norm_w.astype(f32)
    mix = jnp.concatenate([o_g.reshape(Bsz, T, D_GDN), y_s], axis=-1)
    out = jnp.einsum('bte,ed->btd', mix.astype(h.dtype), w_out)
    return out, gdn_buf, ssm_buf, s_gdn, s_ssm


def _dt_bias(k, shape):
    dt = jnp.exp(jax.random.uniform(k, shape, minval=math.log(1e-3), maxval=math.log(1e-1)))
    return dt + jnp.log(-jnp.expm1(-dt))


def setup_inputs(seed: int = 0) -> dict:
    key = jax.random.key(seed)
    ks = jax.random.split(key, 24)
    nrm = jax.random.normal
    f32 = jnp.float32
    return {
        "x_prompt": nrm(ks[0], (BATCH, SEQ, D_MODEL), f32),
        "x_sample": nrm(ks[1], (DEC_BATCH, DEC_SEQ, D_MODEL), f32),
        "state_gdn": 0.05 * nrm(ks[2], (DEPTH, DEC_BATCH, GDN_HEADS, GDN_DK, GDN_DV), f32),
        "state_gdn_conv": nrm(ks[3], (DEPTH, DEC_BATCH, CONV_W - 1, GDN_CONV_DIM), f32),
        "state_ssm": 0.1 * nrm(ks[4], (DEPTH, DEC_BATCH, SSM_HEADS, SSM_HEAD_DIM, SSM_STATE), f32),
        "state_ssm_conv": nrm(ks[5], (DEPTH, DEC_BATCH, CONV_W - 1, SSM_CONV_DIM), f32),
        "meta_tokens": nrm(ks[6], (N_META, D_MODEL), f32),
        "norm_w": 1.0 + 0.01 * nrm(ks[7], (DEPTH, D_MODEL), f32),
        "w_in": nrm(ks[8], (DEPTH, D_MODEL, N_IN), f32) * D_MODEL ** -0.5,
        "gdn_conv_w": nrm(ks[9], (DEPTH, CONV_W, GDN_CONV_DIM), f32) * CONV_W ** -0.5,
        "gdn_A_log": jnp.log(jax.random.uniform(ks[10], (DEPTH, GDN_HEADS), f32, 1.0, 16.0)),
        "gdn_dt_bias": _dt_bias(ks[11], (DEPTH, GDN_HEADS)),
        "gdn_norm_w": 1.0 + 0.01 * nrm(ks[12], (DEPTH, GDN_DV), f32),
        "ssm_conv_w": nrm(ks[13], (DEPTH, CONV_W, SSM_CONV_DIM), f32) * CONV_W ** -0.5,
        "ssm_conv_b": 0.01 * nrm(ks[14], (DEPTH, SSM_CONV_DIM), f32),
        "ssm_A_log": jnp.log(jax.random.uniform(ks[15], (DEPTH, SSM_HEADS), f32, 1.0, 16.0)),
        "ssm_dt_bias": _dt_bias(ks[16], (DEPTH, SSM_HEADS)),
        "ssm_D": 1.0 + 0.1 * nrm(ks[17], (DEPTH, SSM_HEADS), f32),
        "ssm_norm_w": 1.0 + 0.01 * nrm(ks[18], (DEPTH, D_SSM), f32),
        "w_out": nrm(ks[19], (DEPTH, D_MIX, D_MODEL), f32) * D_MIX ** -0.5,
        "final_norm_w": 1.0 + 0.01 * nrm(ks[20], (D_MODEL,), f32),
    }


def reference(x_prompt, x_sample, state_gdn, state_gdn_conv, state_ssm, state_ssm_conv, meta_tokens,
              norm_w, w_in, gdn_conv_w, gdn_A_log, gdn_dt_bias, gdn_norm_w, ssm_conv_w, ssm_conv_b,
              ssm_A_log, ssm_dt_bias, ssm_D, ssm_norm_w, w_out, final_norm_w):
    f32 = jnp.float32
    bp = x_prompt.shape[0]
    meta = jnp.broadcast_to(meta_tokens.astype(x_prompt.dtype)[None], (bp, N_META, D_MODEL))
    xp = jnp.concatenate([meta, x_prompt], axis=1)
    xs = x_sample
    zero_gbuf = jnp.zeros((bp, CONV_W - 1, GDN_CONV_DIM), f32)
    zero_sbuf = jnp.zeros((bp, CONV_W - 1, SSM_CONV_DIM), f32)
    zero_sg = jnp.zeros((bp, GDN_HEADS, GDN_DK, GDN_DV), f32)
    zero_ss = jnp.zeros((bp, SSM_HEADS, SSM_HEAD_DIM, SSM_STATE), f32)
    p_g, p_gc, p_s, p_sc = [], [], [], []
    d_g, d_gc, d_s, d_sc = [], [], [], []
    for l in range(DEPTH):
        wl = (w_in[l], gdn_conv_w[l], gdn_A_log[l], gdn_dt_bias[l], gdn_norm_w[l], ssm_conv_w[l],
              ssm_conv_b[l], ssm_A_log[l], ssm_dt_bias[l], ssm_D[l], ssm_norm_w[l], w_out[l])
        out, gb, sb, sg, ss = _mixer(_rmsnorm(xp, norm_w[l]), zero_gbuf, zero_sbuf, zero_sg, zero_ss, *wl, chunked=True)
        xp = xp + out.astype(xp.dtype)
        p_g.append(sg); p_gc.append(gb); p_s.append(ss); p_sc.append(sb)
        out, gb, sb, sg, ss = _mixer(_rmsnorm(xs, norm_w[l]), state_gdn_conv[l], state_ssm_conv[l],
                                     state_gdn[l], state_ssm[l], *wl, chunked=False)
        xs = xs + out.astype(xs.dtype)
        d_g.append(sg); d_gc.append(gb); d_s.append(ss); d_sc.append(sb)
    y_prompt = _rmsnorm(xp, final_norm_w)[:, N_META:]
    y_sample = _rmsnorm(xs, final_norm_w)
    return (y_prompt, y_sample,
            jnp.stack(p_g), jnp.stack(p_gc), jnp.stack(p_s), jnp.stack(p_sc),
            jnp.stack(d_g), jnp.stack(d_gc), jnp.stack(d_s), jnp.stack(d_sc))
```

```python
import functools

import jax
import jax.numpy as jnp
from jax import lax
from jax.experimental import pallas as pl
from jax.experimental.pallas import tpu as pltpu

F32 = jnp.float32
BF16 = jnp.bfloat16

D_MODEL = 1024
DEPTH = 4
N_META = 16
GDN_HEADS = 8
GDN_DK = 128
GDN_QK = 1024
D_GDN = 1024
GDN_CONV_DIM = 3072
D_SSM = 1024
SSM_HEADS = 16
SSM_HEAD_DIM = 64
SSM_GROUPS = 2
SSM_STATE = 128
SSM_CONV_DIM = 1536
CONV_W = 4
EPS = 1e-6
OFF_Z_G = GDN_CONV_DIM
OFF_B = OFF_Z_G + D_GDN
OFF_A = OFF_B + GDN_HEADS
OFF_XBC = OFF_A + GDN_HEADS
OFF_Z_S = OFF_XBC + SSM_CONV_DIM
OFF_DT = OFF_Z_S + D_SSM

CHUNK = 128
HALF = CHUNK // 2
FRONT_PAD = 112
SUBLANES = 8
LANES = 128
COL_BLOCK = 512
ROW_TILE = 512
NEG_BIG = -1e30
VMEM_LIMIT = 56 * 1024 * 1024

FEAT_BETA = 0
FEAT_G = 8
FEAT_DT = 16
FEAT_LA = 32
FEAT_END = 48


def _cparams(*sem):
    return pltpu.CompilerParams(dimension_semantics=sem, vmem_limit_bytes=VMEM_LIMIT)


def _dot(a, b):
    return jnp.dot(a.astype(BF16), b.astype(BF16), preferred_element_type=F32)


def _dot_nt(a, b):
    return lax.dot_general(a.astype(BF16), b.astype(BF16), (((1,), (1,)), ((), ())),
                           preferred_element_type=F32)


def _dot_tn(a, b):
    return lax.dot_general(a.astype(BF16), b.astype(BF16), (((0,), (0,)), ((), ())),
                           preferred_element_type=F32)


def _silu(y):
    return y * jax.nn.sigmoid(y)


def _softplus(x):
    return jnp.maximum(x, 0.0) + jnp.log1p(jnp.exp(-jnp.abs(x)))


def _rmsnorm_kernel(x_ref, w_ref, h_ref):
    x = x_ref[...]
    y = x * lax.rsqrt(jnp.mean(x * x, -1, keepdims=True) + EPS) * w_ref[...]
    h_ref[...] = y.astype(h_ref.dtype)


def _rmsnorm(x, w, tile):
    rows = x.shape[0]
    return pl.pallas_call(
        _rmsnorm_kernel,
        grid=(rows // tile,),
        in_specs=[pl.BlockSpec((tile, D_MODEL), lambda i: (i, 0)),
                  pl.BlockSpec((1, D_MODEL), lambda i: (0, 0))],
        out_specs=pl.BlockSpec((tile, D_MODEL), lambda i: (i, 0)),
        out_shape=jax.ShapeDtypeStruct((rows, D_MODEL), BF16),
        compiler_params=_cparams("arbitrary"),
        name="rmsnorm",
    )(x, w.reshape(1, D_MODEL))


def _proj_conv_kernel(h_ref, w_ref, cw_ref, cb_ref, out_ref, tail_ref, zs_ref, *, t_pad):
    zs_ref[0:SUBLANES, :] = jnp.zeros((SUBLANES, zs_ref.shape[1]), F32)
    zs_ref[SUBLANES:, :] = jnp.dot(h_ref[0], w_ref[...], preferred_element_type=F32)
    tail_ref[0] = zs_ref[t_pad:t_pad + SUBLANES, :]
    cw = cw_ref[...]
    cb = cb_ref[...]
    for r in range(t_pad // CHUNK):
        base = r * CHUNK + SUBLANES
        acc = cb + cw[3:4] * zs_ref[base:base + CHUNK, :]
        for tap in range(CONV_W - 1):
            off = base - (CONV_W - 1) + tap
            acc = acc + cw[tap:tap + 1] * zs_ref[off:off + CHUNK, :]
        out_ref[0, r * CHUNK:(r + 1) * CHUNK, :] = _silu(acc)


def _proj_silu_kernel(h_ref, w_ref, out_ref):
    out_ref[0] = _silu(jnp.dot(h_ref[0], w_ref[...], preferred_element_type=F32))


def _proj_conv(h3, w, cw, cb):
    bsz, t_pad, _ = h3.shape
    ncols = w.shape[1]
    return pl.pallas_call(
        functools.partial(_proj_conv_kernel, t_pad=t_pad),
        grid=(bsz, ncols // COL_BLOCK),
        in_specs=[pl.BlockSpec((1, t_pad, D_MODEL), lambda b, j: (b, 0, 0)),
                  pl.BlockSpec((D_MODEL, COL_BLOCK), lambda b, j: (0, j)),
                  pl.BlockSpec((CONV_W, COL_BLOCK), lambda b, j: (0, j)),
                  pl.BlockSpec((1, COL_BLOCK), lambda b, j: (0, j))],
        out_specs=[pl.BlockSpec((1, t_pad, COL_BLOCK), lambda b, j: (b, 0, j)),
                   pl.BlockSpec((1, SUBLANES, COL_BLOCK), lambda b, j: (b, 0, j))],
        out_shape=[jax.ShapeDtypeStruct((bsz, t_pad, ncols), F32),
                   jax.ShapeDtypeStruct((bsz, SUBLANES, ncols), F32)],
        scratch_shapes=[pltpu.VMEM((t_pad + SUBLANES, COL_BLOCK), F32)],
        compiler_params=_cparams("arbitrary", "arbitrary"),
        name="proj_conv",
    )(h3, w, cw, cb.reshape(1, ncols))


def _proj_silu(h3, w):
    bsz, t_pad, _ = h3.shape
    ncols = w.shape[1]
    return pl.pallas_call(
        _proj_silu_kernel,
        grid=(bsz, ncols // COL_BLOCK),
        in_specs=[pl.BlockSpec((1, t_pad, D_MODEL), lambda b, j: (b, 0, 0)),
                  pl.BlockSpec((D_MODEL, COL_BLOCK), lambda b, j: (0, j))],
        out_specs=pl.BlockSpec((1, t_pad, COL_BLOCK), lambda b, j: (b, 0, j)),
        out_shape=jax.ShapeDtypeStruct((bsz, t_pad, ncols), F32),
        compiler_params=_cparams("arbitrary", "arbitrary"),
        name="proj_silu",
    )(h3, w)


def _features(z, p_bias, p_scale):
    lane = lax.broadcasted_iota(jnp.int32, z.shape, 1)
    sp = p_scale * _softplus(z + p_bias)
    return jnp.where(lane < FEAT_G, jax.nn.sigmoid(z), jnp.where(lane < FEAT_END, sp, 0.0))


def _split3(x):
    x1 = x.astype(BF16)
    r1 = x - x1.astype(F32)
    x2 = r1.astype(BF16)
    x3 = (r1 - x2.astype(F32)).astype(BF16)
    return x1, x2, x3


def _feat_prompt_kernel(h_ref, w_ref, pb_ref, ps_ref, feat_ref, cs_ref, cst_ref, *, t_pad):
    z = jnp.dot(h_ref[0], w_ref[...], preferred_element_type=F32)
    feat = _features(z, pb_ref[...], ps_ref[...])
    row = lax.broadcasted_iota(jnp.int32, feat.shape, 0)
    feat = jnp.where(row < FRONT_PAD, 0.0, feat)
    feat_ref[0] = feat
    r = lax.broadcasted_iota(jnp.int32, (CHUNK, CHUNK), 0)
    c = lax.broadcasted_iota(jnp.int32, (CHUNK, CHUNK), 1)
    tri = (r >= c).astype(BF16)
    for ch in range(t_pad // CHUNK):
        x1, x2, x3 = _split3(feat[ch * CHUNK:(ch + 1) * CHUNK])
        cs = (jnp.dot(tri, x1, preferred_element_type=F32)
              + jnp.dot(tri, x2, preferred_element_type=F32)
              + jnp.dot(tri, x3, preferred_element_type=F32))
        cs_ref[0, ch * CHUNK:(ch + 1) * CHUNK, :] = cs
        cst_ref[0, ch] = cs.T


def _feat_prompt(h3, w_small, p_bias, p_scale):
    bsz, t_pad, _ = h3.shape
    nch = t_pad // CHUNK
    return pl.pallas_call(
        functools.partial(_feat_prompt_kernel, t_pad=t_pad),
        grid=(bsz,),
        in_specs=[pl.BlockSpec((1, t_pad, D_MODEL), lambda b: (b, 0, 0)),
                  pl.BlockSpec((D_MODEL, LANES), lambda b: (0, 0)),
                  pl.BlockSpec((1, LANES), lambda b: (0, 0)),
                  pl.BlockSpec((1, LANES), lambda b: (0, 0))],
        out_specs=[pl.BlockSpec((1, t_pad, LANES), lambda b: (b, 0, 0)),
                   pl.BlockSpec((1, t_pad, LANES), lambda b: (b, 0, 0)),
                   pl.BlockSpec((1, nch, CHUNK, LANES), lambda b: (b, 0, 0, 0))],
        out_shape=[jax.ShapeDtypeStruct((bsz, t_pad, LANES), F32),
                   jax.ShapeDtypeStruct((bsz, t_pad, LANES), F32),
                   jax.ShapeDtypeStruct((bsz, nch, CHUNK, LANES), F32)],
        compiler_params=_cparams("arbitrary"),
        name="feat_prompt",
    )(h3, w_small, p_bias, p_scale)


def _feat_decode_kernel(h_ref, w_ref, pb_ref, ps_ref, feat_ref):
    z = jnp.dot(h_ref[...], w_ref[...], preferred_element_type=F32)
    feat_ref[...] = _features(z, pb_ref[...], ps_ref[...])


def _feat_decode(h, w_small, p_bias, p_scale):
    rows = h.shape[0]
    return pl.pallas_call(
        _feat_decode_kernel,
        out_shape=jax.ShapeDtypeStruct((rows, LANES), F32),
        compiler_params=pltpu.CompilerParams(vmem_limit_bytes=VMEM_LIMIT),
        name="feat_decode",
    )(h, w_small, p_bias, p_scale)


def _gdn_chunk_kernel(q_ref, k_ref, v_ref, zg_ref, feat_ref, cs_ref, cst_ref, nw_ref,
                      o_ref, sfin_ref, s_ref):
    ch = pl.program_id(1)

    @pl.when(ch == 0)
    def _():
        s_ref[...] = jnp.zeros(s_ref.shape, F32)

    row = lax.broadcasted_iota(jnp.int32, (CHUNK, CHUNK), 0)
    col = lax.broadcasted_iota(jnp.int32, (CHUNK, CHUNK), 1)
    incl = row >= col
    strict = row > col
    same_half = (row >= HALF) == (col >= HALF)
    eye = (row == col).astype(F32)
    feat = feat_ref[0]
    cs = cs_ref[0]
    cst = cst_ref[0, 0]
    nw = nw_ref[...]
    for h in range(GDN_HEADS):
        sl = slice(h * GDN_DK, (h + 1) * GDN_DK)
        q = q_ref[0, :, sl]
        k = k_ref[0, :, sl]
        v = v_ref[0, :, sl]
        q = q * (lax.rsqrt(jnp.sum(q * q, -1, keepdims=True) + EPS) * (GDN_DK ** -0.5))
        k = k * lax.rsqrt(jnp.sum(k * k, -1, keepdims=True) + EPS)
        beta = feat[:, FEAT_BETA + h:FEAT_BETA + h + 1]
        gc = cs[:, FEAT_G + h:FEAT_G + h + 1]
        gr = cst[FEAT_G + h:FEAT_G + h + 1, :]
        decay = jnp.exp(jnp.where(incl, gc - gr, NEG_BIG))
        kb = k * beta
        a = jnp.where(strict, _dot_nt(kb, k) * decay, 0.0)
        a_diag = jnp.where(same_half, a, 0.0)
        a_off = a - a_diag
        p = -a_diag
        inv = eye + p
        for _ in range(5):
            p = _dot(p, p)
            inv = inv + _dot(inv, p)
        t_inv = inv - _dot(_dot(inv, a_off), inv)
        eg = jnp.exp(gc)
        uw = _dot(t_inv, jnp.concatenate([v * beta, kb * eg], axis=1))
        u = uw[:, :GDN_DK]
        w = uw[:, GDN_DK:]
        attn = _dot_nt(q, k) * decay
        g_last = gc[CHUNK - 1:CHUNK, :]
        q_dec = q * eg
        k_dec = k * jnp.exp(g_last - gc)
        s = s_ref[h]
        ws_qs = _dot(jnp.concatenate([w, q_dec], axis=0), s)
        v_new = u - ws_qs[:CHUNK]
        o = ws_qs[CHUNK:] + _dot(attn, v_new)
        s_ref[h] = s * jnp.exp(g_last) + _dot_tn(k_dec, v_new)
        o = o * lax.rsqrt(jnp.mean(o * o, -1, keepdims=True) + EPS) * nw * zg_ref[0, :, sl]
        o_ref[0, :, sl] = o

    @pl.when(ch == pl.num_programs(1) - 1)
    def _():
        sfin_ref[0] = s_ref[...]


def _gdn_chunk(qkv_act, z_act, feat, cs, cst, norm_w):
    bsz, t_pad, _ = qkv_act.shape
    nch = t_pad // CHUNK

    def qkv_spec(part):
        return pl.BlockSpec((1, CHUNK, GDN_QK), lambda b, c: (b, c, part))

    return pl.pallas_call(
        _gdn_chunk_kernel,
        grid=(bsz, nch),
        in_specs=[qkv_spec(0), qkv_spec(1), qkv_spec(2),
                  pl.BlockSpec((1, CHUNK, D_GDN), lambda b, c: (b, c, 0)),
                  pl.BlockSpec((1, CHUNK, LANES), lambda b, c: (b, c, 0)),
                  pl.BlockSpec((1, CHUNK, LANES), lambda b, c: (b, c, 0)),
                  pl.BlockSpec((1, 1, CHUNK, LANES), lambda b, c: (b, c, 0, 0)),
                  pl.BlockSpec((1, GDN_DK), lambda b, c: (0, 0))],
        out_specs=[pl.BlockSpec((1, CHUNK, D_GDN), lambda b, c: (b, c, 0)),
                   pl.BlockSpec((1, GDN_HEADS, GDN_DK, GDN_DK), lambda b, c: (b, 0, 0, 0))],
        out_shape=[jax.ShapeDtypeStruct((bsz, t_pad, D_GDN), F32),
                   jax.ShapeDtypeStruct((bsz, GDN_HEADS, GDN_DK, GDN_DK), F32)],
        scratch_shapes=[pltpu.VMEM((GDN_HEADS, GDN_DK, GDN_DK), F32)],
        compiler_params=_cparams("arbitrary", "arbitrary"),
        name="gdn_chunk",
    )(qkv_act, qkv_act, qkv_act, z_act, feat, cs, cst, norm_w.reshape(1, GDN_DK))


def _ssd_finish(y, zs, nw):
    yg = y * zs
    gw = D_SSM // SSM_GROUPS
    parts = []
    for g in range(SSM_GROUPS):
        blk = yg[:, g * gw:(g + 1) * gw]
        parts.append(blk * lax.rsqrt(jnp.mean(blk * blk, -1, keepdims=True) + EPS))
    return jnp.concatenate(parts, axis=1) * nw


def _ssd_chunk_kernel(x_ref, b_ref, c_ref, zs_ref, feat_ref, cs_ref, cst_ref, d_ref, nw_ref,
                      y_ref, sfin_ref, s_ref):
    ch = pl.program_id(1)

    @pl.when(ch == 0)
    def _():
        s_ref[...] = jnp.zeros(s_ref.shape, F32)

    row = lax.broadcasted_iota(jnp.int32, (CHUNK, CHUNK), 0)
    col = lax.broadcasted_iota(jnp.int32, (CHUNK, CHUNK), 1)
    incl = row >= col
    left = col < SSM_HEAD_DIM
    top = row < SSM_HEAD_DIM
    feat = feat_ref[0]
    cs = cs_ref[0]
    cst = cst_ref[0, 0]
    heads_per_group = SSM_HEADS // SSM_GROUPS
    ys = []
    for pair in range(SSM_HEADS // 2):
        grp = (2 * pair) // heads_per_group
        bm = b_ref[0, :, grp * SSM_STATE:(grp + 1) * SSM_STATE]
        cm = c_ref[0, :, grp * SSM_STATE:(grp + 1) * SSM_STATE]
        cb = _dot_nt(cm, bm)
        xp = x_ref[0, :, pair * LANES:(pair + 1) * LANES]
        h0 = 2 * pair
        h1 = h0 + 1

        def colv(base, h):
            return feat[:, base + h:base + h + 1]

        dt = jnp.where(left, colv(FEAT_DT, h0), colv(FEAT_DT, h1))
        xdt = xp * dt
        ac0 = cs[:, FEAT_LA + h0:FEAT_LA + h0 + 1]
        ac1 = cs[:, FEAT_LA + h1:FEAT_LA + h1 + 1]
        ar0 = cst[FEAT_LA + h0:FEAT_LA + h0 + 1, :]
        ar1 = cst[FEAT_LA + h1:FEAT_LA + h1 + 1, :]
        sc0 = cb * jnp.exp(jnp.where(incl, ac0 - ar0, NEG_BIG))
        sc1 = cb * jnp.exp(jnp.where(incl, ac1 - ar1, NEG_BIG))
        y_diag = jnp.where(left, _dot(sc0, xdt), _dot(sc1, xdt))
        s = s_ref[pair * LANES:(pair + 1) * LANES, :]
        y_off = _dot_nt(cm, s) * jnp.where(left, jnp.exp(ac0), jnp.exp(ac1))
        al0 = ac0[CHUNK - 1:CHUNK, :]
        al1 = ac1[CHUNK - 1:CHUNK, :]
        xe = xdt * jnp.where(left, jnp.exp(al0 - ac0), jnp.exp(al1 - ac1))
        ds = _dot_tn(xe, bm)
        s_ref[pair * LANES:(pair + 1) * LANES, :] = (
            s * jnp.where(top, jnp.exp(al0), jnp.exp(al1)) + ds)
        ys.append(y_diag + y_off + d_ref[:, pair * LANES:(pair + 1) * LANES] * xp)
    y_ref[0] = _ssd_finish(jnp.concatenate(ys, axis=1), zs_ref[0], nw_ref[...])

    @pl.when(ch == pl.num_programs(1) - 1)
    def _():
        sfin_ref[0] = s_ref[...]


def _ssd_chunk(xbc_act, z_act, feat, cs, cst, d_cols, norm_w):
    bsz, t_pad, _ = xbc_act.shape
    nch = t_pad // CHUNK
    bc_w = SSM_GROUPS * SSM_STATE
    return pl.pallas_call(
        _ssd_chunk_kernel,
        grid=(bsz, nch),
        in_specs=[pl.BlockSpec((1, CHUNK, D_SSM), lambda b, c: (b, c, 0)),
                  pl.BlockSpec((1, CHUNK, bc_w), lambda b, c: (b, c, D_SSM // bc_w)),
                  pl.BlockSpec((1, CHUNK, bc_w), lambda b, c: (b, c, D_SSM // bc_w + 1)),
                  pl.BlockSpec((1, CHUNK, D_SSM), lambda b, c: (b, c, 1)),
                  pl.BlockSpec((1, CHUNK, LANES), lambda b, c: (b, c, 0)),
                  pl.BlockSpec((1, CHUNK, LANES), lambda b, c: (b, c, 0)),
                  pl.BlockSpec((1, 1, CHUNK, LANES), lambda b, c: (b, c, 0, 0)),
                  pl.BlockSpec((1, D_SSM), lambda b, c: (0, 0)),
                  pl.BlockSpec((1, D_SSM), lambda b, c: (0, 0))],
        out_specs=[pl.BlockSpec((1, CHUNK, D_SSM), lambda b, c: (b, c, 0)),
                   pl.BlockSpec((1, SSM_HEADS * SSM_HEAD_DIM, SSM_STATE), lambda b, c: (b, 0, 0))],
        out_shape=[jax.ShapeDtypeStruct((bsz, t_pad, D_SSM), F32),
                   jax.ShapeDtypeStruct((bsz, SSM_HEADS * SSM_HEAD_DIM, SSM_STATE), F32)],
        scratch_shapes=[pltpu.VMEM((SSM_HEADS * SSM_HEAD_DIM, SSM_STATE), F32)],
        compiler_params=_cparams("arbitrary", "arbitrary"),
        name="ssd_chunk",
    )(xbc_act, xbc_act, xbc_act, z_act, feat, cs, cst, d_cols, norm_w.reshape(1, D_SSM))


def _out_proj_kernel(mg_ref, ms_ref, x_ref, w_ref, nw_ref, xo_ref, ho_ref, *, seq_rows):
    acc = (jnp.dot(mg_ref[...].astype(BF16), w_ref[0:D_GDN, :], preferred_element_type=F32)
           + jnp.dot(ms_ref[...].astype(BF16), w_ref[D_GDN:, :], preferred_element_type=F32))
    xn = x_ref[...] + acc
    if seq_rows:
        tile = xn.shape[0]
        start = lax.rem(pl.program_id(0) * tile, seq_rows)
        r = start + lax.broadcasted_iota(jnp.int32, xn.shape, 0)
        pad = (r < FRONT_PAD) | ((r >= seq_rows) & (r < seq_rows + FRONT_PAD))
        xn = jnp.where(pad, 0.0, xn)
    xo_ref[...] = xn
    hn = xn * lax.rsqrt(jnp.mean(xn * xn, -1, keepdims=True) + EPS) * nw_ref[...]
    ho_ref[...] = hn.astype(ho_ref.dtype)


def _out_proj(mix_g, mix_s, x, w_out, next_norm_w, h_dtype, tile, seq_rows):
    rows = x.shape[0]
    row_spec = pl.BlockSpec((tile, D_MODEL), lambda i: (i, 0))
    return pl.pallas_call(
        functools.partial(_out_proj_kernel, seq_rows=seq_rows),
        grid=(rows // tile,),
        in_specs=[row_spec, row_spec, row_spec,
                  pl.BlockSpec((D_GDN + D_SSM, D_MODEL), lambda i: (0, 0)),
                  pl.BlockSpec((1, D_MODEL), lambda i: (0, 0))],
        out_specs=[row_spec, row_spec],
        out_shape=[jax.ShapeDtypeStruct((rows, D_MODEL), F32),
                   jax.ShapeDtypeStruct((rows, D_MODEL), h_dtype)],
        compiler_params=_cparams("arbitrary"),
        name="out_proj",
    )(mix_g, mix_s, x, w_out, next_norm_w.reshape(1, D_MODEL))


def _proj_conv_decode_kernel(h_ref, w_ref, buf_ref, cw_ref, cb_ref, out_ref, nbuf_ref):
    z = jnp.dot(h_ref[...], w_ref[...], preferred_element_type=F32)
    cw = cw_ref[...]
    acc = cb_ref[...] + cw[3:4] * z
    for tap in range(CONV_W - 1):
        acc = acc + cw[tap:tap + 1] * buf_ref[tap]
    out_ref[...] = _silu(acc)
    nbuf_ref[0] = buf_ref[1]
    nbuf_ref[1] = buf_ref[2]
    nbuf_ref[2] = z


def _proj_silu_decode_kernel(h_ref, w_ref, out_ref):
    out_ref[...] = _silu(jnp.dot(h_ref[...], w_ref[...], preferred_element_type=F32))


def _proj_conv_decode(h, w, buf, cw, cb):
    rows = h.shape[0]
    ncols = w.shape[1]
    return pl.pallas_call(
        _proj_conv_decode_kernel,
        grid=(ncols // COL_BLOCK,),
        in_specs=[pl.BlockSpec((rows, D_MODEL), lambda j: (0, 0)),
                  pl.BlockSpec((D_MODEL, COL_BLOCK), lambda j: (0, j)),
                  pl.BlockSpec((CONV_W - 1, rows, COL_BLOCK), lambda j: (0, 0, j)),
                  pl.BlockSpec((CONV_W, COL_BLOCK), lambda j: (0, j)),
                  pl.BlockSpec((1, COL_BLOCK), lambda j: (0, j))],
        out_specs=[pl.BlockSpec((rows, COL_BLOCK), lambda j: (0, j)),
                   pl.BlockSpec((CONV_W - 1, rows, COL_BLOCK), lambda j: (0, 0, j))],
        out_shape=[jax.ShapeDtypeStruct((rows, ncols), F32),
                   jax.ShapeDtypeStruct((CONV_W - 1, rows, ncols), F32)],
        compiler_params=_cparams("arbitrary"),
        name="proj_conv_decode",
    )(h, w, buf, cw, cb.reshape(1, ncols))


def _proj_silu_decode(h, w):
    rows = h.shape[0]
    ncols = w.shape[1]
    return pl.pallas_call(
        _proj_silu_decode_kernel,
        grid=(ncols // COL_BLOCK,),
        in_specs=[pl.BlockSpec((rows, D_MODEL), lambda j: (0, 0)),
                  pl.BlockSpec((D_MODEL, COL_BLOCK), lambda j: (0, j))],
        out_specs=pl.BlockSpec((rows, COL_BLOCK), lambda j: (0, j)),
        out_shape=jax.ShapeDtypeStruct((rows, ncols), F32),
        compiler_params=_cparams("arbitrary"),
        name="proj_silu_decode",
    )(h, w)


TOK_BLOCK = 8


def _gdn_step_kernel(qt_ref, kt_ref, v_ref, zg_ref, feat_ref, nw_ref, s_ref, o_ref, so_ref):
    feat = feat_ref[...]
    vrows = v_ref[...]
    for h in range(GDN_HEADS):
        sl = slice(h * GDN_DK, (h + 1) * GDN_DK)
        qt = qt_ref[0, h]
        kt = kt_ref[0, h]
        qt = qt * (lax.rsqrt(jnp.sum(qt * qt, 0, keepdims=True) + EPS) * (GDN_DK ** -0.5))
        kt = kt * lax.rsqrt(jnp.sum(kt * kt, 0, keepdims=True) + EPS)
        for j in range(TOK_BLOCK):
            kc = kt[:, j:j + 1]
            qc = qt[:, j:j + 1]
            beta = feat[j:j + 1, FEAT_BETA + h:FEAT_BETA + h + 1]
            g = feat[j:j + 1, FEAT_G + h:FEAT_G + h + 1]
            s = s_ref[j, h] * jnp.exp(g)
            kv = jnp.sum(s * kc, 0, keepdims=True)
            d = (vrows[j:j + 1, sl] - kv) * beta
            s = s + kc * d
            so_ref[j, h] = s
            o_ref[j:j + 1, sl] = jnp.sum(s * qc, 0, keepdims=True)
    nw = nw_ref[...]
    for h in range(GDN_HEADS):
        sl = slice(h * GDN_DK, (h + 1) * GDN_DK)
        o = o_ref[:, sl]
        o_ref[:, sl] = o * lax.rsqrt(jnp.mean(o * o, -1, keepdims=True) + EPS) * nw * zg_ref[:, sl]


def _gdn_step(qt, kt, qkv_act, z_act, feat, norm_w, state):
    rows = qkv_act.shape[0]
    return pl.pallas_call(
        _gdn_step_kernel,
        grid=(rows // TOK_BLOCK,),
        in_specs=[pl.BlockSpec((1, GDN_HEADS, GDN_DK, TOK_BLOCK), lambda i: (i, 0, 0, 0)),
                  pl.BlockSpec((1, GDN_HEADS, GDN_DK, TOK_BLOCK), lambda i: (i, 0, 0, 0)),
                  pl.BlockSpec((TOK_BLOCK, D_GDN), lambda i: (i, 2)),
                  pl.BlockSpec((TOK_BLOCK, D_GDN), lambda i: (i, 0)),
                  pl.BlockSpec((TOK_BLOCK, LANES), lambda i: (i, 0)),
                  pl.BlockSpec((1, GDN_DK), lambda i: (0, 0)),
                  pl.BlockSpec((TOK_BLOCK, GDN_HEADS, GDN_DK, GDN_DK), lambda i: (i, 0, 0, 0))],
        out_specs=[pl.BlockSpec((TOK_BLOCK, D_GDN), lambda i: (i, 0)),
                   pl.BlockSpec((TOK_BLOCK, GDN_HEADS, GDN_DK, GDN_DK), lambda i: (i, 0, 0, 0))],
        out_shape=[jax.ShapeDtypeStruct((rows, D_GDN), F32),
                   jax.ShapeDtypeStruct(state.shape, F32)],
        compiler_params=_cparams("arbitrary"),
        name="gdn_step",
    )(qt, kt, qkv_act, z_act, feat, norm_w.reshape(1, GDN_DK), state)


def _ssd_step_kernel(xt_ref, b_ref, c_ref, feat_ref, s_ref, yt_ref, so_ref):
    feat = feat_ref[...]
    heads_per_group = SSM_HEADS // SSM_GROUPS
    for h in range(SSM_HEADS):
        grp = h // heads_per_group
        xt = xt_ref[0, h]
        for j in range(TOK_BLOCK):
            dt = feat[j:j + 1, FEAT_DT + h:FEAT_DT + h + 1]
            la = feat[j:j + 1, FEAT_LA + h:FEAT_LA + h + 1]
            brow = b_ref[j:j + 1, grp * SSM_STATE:(grp + 1) * SSM_STATE]
            crow = c_ref[j:j + 1, grp * SSM_STATE:(grp + 1) * SSM_STATE]
            s = s_ref[j, h] * jnp.exp(la) + (xt[:, j:j + 1] * dt) * brow
            so_ref[j, h] = s
            yt_ref[0, h, :, j:j + 1] = jnp.sum(s * crow, -1, keepdims=True)


def _ssd_step(xt, xbc_act, feat, state):
    rows = xbc_act.shape[0]
    ngrp = rows // TOK_BLOCK
    bc_w = SSM_GROUPS * SSM_STATE
    return pl.pallas_call(
        _ssd_step_kernel,
        grid=(ngrp,),
        in_specs=[pl.BlockSpec((1, SSM_HEADS, SSM_HEAD_DIM, TOK_BLOCK), lambda i: (i, 0, 0, 0)),
                  pl.BlockSpec((TOK_BLOCK, bc_w), lambda i: (i, D_SSM // bc_w)),
                  pl.BlockSpec((TOK_BLOCK, bc_w), lambda i: (i, D_SSM // bc_w + 1)),
                  pl.BlockSpec((TOK_BLOCK, LANES), lambda i: (i, 0)),
                  pl.BlockSpec((TOK_BLOCK, SSM_HEADS, SSM_HEAD_DIM, SSM_STATE),
                               lambda i: (i, 0, 0, 0))],
        out_specs=[pl.BlockSpec((1, SSM_HEADS, SSM_HEAD_DIM, TOK_BLOCK), lambda i: (i, 0, 0, 0)),
                   pl.BlockSpec((TOK_BLOCK, SSM_HEADS, SSM_HEAD_DIM, SSM_STATE),
                                lambda i: (i, 0, 0, 0))],
        out_shape=[jax.ShapeDtypeStruct((ngrp, SSM_HEADS, SSM_HEAD_DIM, TOK_BLOCK), F32),
                   jax.ShapeDtypeStruct(state.shape, F32)],
        compiler_params=_cparams("arbitrary"),
        name="ssd_step",
    )(xt, xbc_act, xbc_act, feat, state)


def _ssd_finish_decode_kernel(y_ref, x_ref, zs_ref, d_ref, nw_ref, out_ref):
    out_ref[...] = _ssd_finish(y_ref[...] + d_ref[...] * x_ref[...], zs_ref[...], nw_ref[...])


def _ssd_finish_decode(y, xbc_act, z_act, d_cols, norm_w):
    rows = y.shape[0]
    return pl.pallas_call(
        _ssd_finish_decode_kernel,
        grid=(1,),
        in_specs=[pl.BlockSpec((rows, D_SSM), lambda i: (0, 0)),
                  pl.BlockSpec((rows, D_SSM), lambda i: (0, 0)),
                  pl.BlockSpec((rows, D_SSM), lambda i: (0, 1)),
                  pl.BlockSpec((1, D_SSM), lambda i: (0, 0)),
                  pl.BlockSpec((1, D_SSM), lambda i: (0, 0))],
        out_specs=pl.BlockSpec((rows, D_SSM), lambda i: (0, 0)),
        out_shape=jax.ShapeDtypeStruct((rows, D_SSM), F32),
        compiler_params=_cparams("arbitrary"),
        name="ssd_finish_decode",
    )(y, xbc_act, z_act, d_cols, norm_w.reshape(1, D_SSM))


def _layer_weights(w_in, gdn_A_log, gdn_dt_bias, ssm_A_log, ssm_dt_bias, ssm_D):
    w_qkv = w_in[:, :OFF_Z_G].astype(BF16)
    w_xbc = w_in[:, OFF_XBC:OFF_Z_S].astype(BF16)
    w_z = jnp.concatenate([w_in[:, OFF_Z_G:OFF_B], w_in[:, OFF_Z_S:OFF_DT]], axis=1).astype(BF16)
    w_dt = w_in[:, OFF_DT:]
    w_small = jnp.concatenate(
        [w_in[:, OFF_B:OFF_XBC], w_dt, w_dt, jnp.zeros((D_MODEL, LANES - FEAT_END), F32)],
        axis=1).astype(BF16)
    zeros8 = jnp.zeros((GDN_HEADS,), F32)
    tail = jnp.zeros((LANES - FEAT_END,), F32)
    p_bias = jnp.concatenate([zeros8, gdn_dt_bias.astype(F32), ssm_dt_bias.astype(F32),
                              ssm_dt_bias.astype(F32), tail]).reshape(1, LANES)
    p_scale = jnp.concatenate([zeros8, -jnp.exp(gdn_A_log.astype(F32)),
                               jnp.ones((SSM_HEADS,), F32), -jnp.exp(ssm_A_log.astype(F32)),
                               tail]).reshape(1, LANES)
    d_cols = jnp.repeat(ssm_D.astype(F32), SSM_HEAD_DIM).reshape(1, D_SSM)
    return w_qkv, w_xbc, w_z, w_small, p_bias, p_scale, d_cols


def kernel(x_prompt, x_sample, state_gdn, state_gdn_conv, state_ssm, state_ssm_conv, meta_tokens,
           norm_w, w_in, gdn_conv_w, gdn_A_log, gdn_dt_bias, gdn_norm_w, ssm_conv_w, ssm_conv_b,
           ssm_A_log, ssm_dt_bias, ssm_D, ssm_norm_w, w_out, final_norm_w):
    bp, seq, _ = x_prompt.shape
    bd = x_sample.shape[0]
    t_pad = FRONT_PAD + N_META + seq
    meta = jnp.broadcast_to(meta_tokens.astype(F32)[None], (bp, N_META, D_MODEL))
    xp = jnp.concatenate([jnp.zeros((bp, FRONT_PAD, D_MODEL), F32), meta, x_prompt.astype(F32)],
                         axis=1).reshape(bp * t_pad, D_MODEL)
    xs = x_sample.astype(F32).reshape(bd, D_MODEL)
    hp = _rmsnorm(xp, norm_w[0], ROW_TILE)
    hs = _rmsnorm(xs, norm_w[0], bd)
    zero_bias = jnp.zeros((GDN_CONV_DIM,), F32)
    ngrp = bd // TOK_BLOCK
    p_g, p_gc, p_s, p_sc = [], [], [], []
    d_g, d_gc, d_s, d_sc = [], [], [], []
    for l in range(DEPTH):
        w_qkv, w_xbc, w_z, w_small, p_bias, p_scale, d_cols = _layer_weights(
            w_in[l], gdn_A_log[l], gdn_dt_bias[l], ssm_A_log[l], ssm_dt_bias[l], ssm_D[l])
        w_o = w_out[l].astype(BF16)
        last = l == DEPTH - 1
        nxt_w = final_norm_w if last else norm_w[l + 1]
        h_dtype = F32 if last else BF16
        gcw = gdn_conv_w[l].astype(F32)
        scw = ssm_conv_w[l].astype(F32)
        scb = ssm_conv_b[l].astype(F32)

        h3 = hp.reshape(bp, t_pad, D_MODEL)
        qkv_act, qkv_tail = _proj_conv(h3, w_qkv, gcw, zero_bias)
        xbc_act, xbc_tail = _proj_conv(h3, w_xbc, scw, scb)
        z_act = _proj_silu(h3, w_z)
        feat, cs, cst = _feat_prompt(h3, w_small, p_bias, p_scale)
        mix_g, sg = _gdn_chunk(qkv_act, z_act, feat, cs, cst, gdn_norm_w[l].astype(F32))
        mix_s, ss = _ssd_chunk(xbc_act, z_act, feat, cs, cst, d_cols, ssm_norm_w[l].astype(F32))
        xp, hp = _out_proj(mix_g.reshape(bp * t_pad, D_GDN), mix_s.reshape(bp * t_pad, D_SSM), xp,
                           w_o, nxt_w.astype(F32), h_dtype, ROW_TILE, t_pad)
        p_g.append(sg)
        p_gc.append(qkv_tail[:, SUBLANES - (CONV_W - 1):, :])
        p_s.append(ss.reshape(bp, SSM_HEADS, SSM_HEAD_DIM, SSM_STATE))
        p_sc.append(xbc_tail[:, SUBLANES - (CONV_W - 1):, :])

        gbuf = jnp.transpose(state_gdn_conv[l].astype(F32), (1, 0, 2))
        sbuf = jnp.transpose(state_ssm_conv[l].astype(F32), (1, 0, 2))
        qkv_d, gbuf_n = _proj_conv_decode(hs, w_qkv, gbuf, gcw, zero_bias)
        xbc_d, sbuf_n = _proj_conv_decode(hs, w_xbc, sbuf, scw, scb)
        z_d = _proj_silu_decode(hs, w_z)
        feat_d = _feat_decode(hs, w_small, p_bias, p_scale)

        def head_major(a, heads, dim):
            return jnp.transpose(a.reshape(ngrp, TOK_BLOCK, heads, dim), (0, 2, 3, 1))

        qt = head_major(qkv_d[:, :GDN_QK], GDN_HEADS, GDN_DK)
        kt = head_major(qkv_d[:, GDN_QK:2 * GDN_QK], GDN_HEADS, GDN_DK)
        mix_gd, sg_d = _gdn_step(qt, kt, qkv_d, z_d, feat_d, gdn_norm_w[l].astype(F32),
                                 state_gdn[l].astype(F32))
        xt = head_major(xbc_d[:, :D_SSM], SSM_HEADS, SSM_HEAD_DIM)
        yt, ss_d = _ssd_step(xt, xbc_d, feat_d, state_ssm[l].astype(F32))
        y_d = jnp.transpose(yt, (0, 3, 1, 2)).reshape(bd, D_SSM)
        mix_sd = _ssd_finish_decode(y_d, xbc_d, z_d, d_cols, ssm_norm_w[l].astype(F32))
        xs, hs = _out_proj(mix_gd, mix_sd, xs, w_o, nxt_w.astype(F32), h_dtype, bd, 0)
        d_g.append(sg_d)
        d_gc.append(jnp.transpose(gbuf_n, (1, 0, 2)))
        d_s.append(ss_d)
        d_sc.append(jnp.transpose(sbuf_n, (1, 0, 2)))

    y_prompt = hp.reshape(bp, t_pad, D_MODEL)[:, FRONT_PAD + N_META:]
    y_sample = hs.reshape(bd, 1, D_MODEL)
    return (y_prompt, y_sample,
            jnp.stack(p_g), jnp.stack(p_gc), jnp.stack(p_s), jnp.stack(p_sc),
            jnp.stack(d_g), jnp.stack(d_gc), jnp.stack(d_s), jnp.stack(d_sc))
```

```python
import functools

import jax
import jax.numpy as jnp
from jax import lax
from jax.experimental import pallas as pl
from jax.experimental.pallas import tpu as pltpu

F32 = jnp.float32
BF16 = jnp.bfloat16

D_MODEL = 1024
DEPTH = 4
N_META = 16
GDN_HEADS = 8
GDN_DK = 128
GDN_QK = 1024
D_GDN = 1024
GDN_CONV_DIM = 3072
D_SSM = 1024
SSM_HEADS = 16
SSM_HEAD_DIM = 64
SSM_GROUPS = 2
SSM_STATE = 128
SSM_CONV_DIM = 1536
CONV_W = 4
EPS = 1e-6
OFF_Z_G = GDN_CONV_DIM
OFF_B = OFF_Z_G + D_GDN
OFF_A = OFF_B + GDN_HEADS
OFF_XBC = OFF_A + GDN_HEADS
OFF_Z_S = OFF_XBC + SSM_CONV_DIM
OFF_DT = OFF_Z_S + D_SSM

CHUNK = 128
HALF = CHUNK // 2
FRONT_PAD = 112
SUBLANES = 8
LANES = 128
COL_BLOCK = 512
ROW_TILE = 512
NEG_BIG = -1e30
VMEM_LIMIT = 56 * 1024 * 1024

FEAT_BETA = 0
FEAT_G = 8
FEAT_DT = 16
FEAT_LA = 32
FEAT_END = 48


def _cparams(*sem):
    return pltpu.CompilerParams(dimension_semantics=sem, vmem_limit_bytes=VMEM_LIMIT)


def _dot(a, b):
    return jnp.dot(a.astype(BF16), b.astype(BF16), preferred_element_type=F32)


def _dot_nt(a, b):
    return lax.dot_general(a.astype(BF16), b.astype(BF16), (((1,), (1,)), ((), ())),
                           preferred_element_type=F32)


def _dot_tn(a, b):
    return lax.dot_general(a.astype(BF16), b.astype(BF16), (((0,), (0,)), ((), ())),
                           preferred_element_type=F32)


def _silu(y):
    return y * jax.nn.sigmoid(y)


def _softplus(x):
    return jnp.maximum(x, 0.0) + jnp.log1p(jnp.exp(-jnp.abs(x)))


def _rmsnorm_kernel(x_ref, w_ref, h_ref):
    x = x_ref[...]
    y = x * lax.rsqrt(jnp.mean(x * x, -1, keepdims=True) + EPS) * w_ref[...]
    h_ref[...] = y.astype(h_ref.dtype)


def _rmsnorm(x, w, tile):
    rows = x.shape[0]
    return pl.pallas_call(
        _rmsnorm_kernel,
        grid=(rows // tile,),
        in_specs=[pl.BlockSpec((tile, D_MODEL), lambda i: (i, 0)),
                  pl.BlockSpec((1, D_MODEL), lambda i: (0, 0))],
        out_specs=pl.BlockSpec((tile, D_MODEL), lambda i: (i, 0)),
        out_shape=jax.ShapeDtypeStruct((rows, D_MODEL), BF16),
        compiler_params=_cparams("arbitrary"),
        name="rmsnorm",
    )(x, w.reshape(1, D_MODEL))


def _proj_conv_kernel(h_ref, w_ref, cw_ref, cb_ref, out_ref, tail_ref, zs_ref, *, t_pad):
    zs_ref[0:SUBLANES, :] = jnp.zeros((SUBLANES, zs_ref.shape[1]), F32)
    zs_ref[SUBLANES:, :] = jnp.dot(h_ref[0], w_ref[...], preferred_element_type=F32)
    tail_ref[0] = zs_ref[t_pad:t_pad + SUBLANES, :]
    cw = cw_ref[...]
    cb = cb_ref[...]
    for r in range(t_pad // CHUNK):
        base = r * CHUNK + SUBLANES
        acc = cb + cw[3:4] * zs_ref[base:base + CHUNK, :]
        for tap in range(CONV_W - 1):
            off = base - (CONV_W - 1) + tap
            acc = acc + cw[tap:tap + 1] * zs_ref[off:off + CHUNK, :]
        out_ref[0, r * CHUNK:(r + 1) * CHUNK, :] = _silu(acc)


def _proj_silu_kernel(h_ref, w_ref, out_ref):
    out_ref[0] = _silu(jnp.dot(h_ref[0], w_ref[...], preferred_element_type=F32))


def _proj_conv(h3, w, cw, cb):
    bsz, t_pad, _ = h3.shape
    ncols = w.shape[1]
    return pl.pallas_call(
        functools.partial(_proj_conv_kernel, t_pad=t_pad),
        grid=(bsz, ncols // COL_BLOCK),
        in_specs=[pl.BlockSpec((1, t_pad, D_MODEL), lambda b, j: (b, 0, 0)),
                  pl.BlockSpec((D_MODEL, COL_BLOCK), lambda b, j: (0, j)),
                  pl.BlockSpec((CONV_W, COL_BLOCK), lambda b, j: (0, j)),
                  pl.BlockSpec((1, COL_BLOCK), lambda b, j: (0, j))],
        out_specs=[pl.BlockSpec((1, t_pad, COL_BLOCK), lambda b, j: (b, 0, j)),
                   pl.BlockSpec((1, SUBLANES, COL_BLOCK), lambda b, j: (b, 0, j))],
        out_shape=[jax.ShapeDtypeStruct((bsz, t_pad, ncols), F32),
                   jax.ShapeDtypeStruct((bsz, SUBLANES, ncols), F32)],
        scratch_shapes=[pltpu.VMEM((t_pad + SUBLANES, COL_BLOCK), F32)],
        compiler_params=_cparams("arbitrary", "arbitrary"),
        name="proj_conv",
    )(h3, w, cw, cb.reshape(1, ncols))


def _proj_silu(h3, w):
    bsz, t_pad, _ = h3.shape
    ncols = w.shape[1]
    return pl.pallas_call(
        _proj_silu_kernel,
        grid=(bsz, ncols // COL_BLOCK),
        in_specs=[pl.BlockSpec((1, t_pad, D_MODEL), lambda b, j: (b, 0, 0)),
                  pl.BlockSpec((D_MODEL, COL_BLOCK), lambda b, j: (0, j))],
        out_specs=pl.BlockSpec((1, t_pad, COL_BLOCK), lambda b, j: (b, 0, j)),
        out_shape=jax.ShapeDtypeStruct((bsz, t_pad, ncols), F32),
        compiler_params=_cparams("arbitrary", "arbitrary"),
        name="proj_silu",
    )(h3, w)


def _features(z, p_bias, p_scale):
    lane = lax.broadcasted_iota(jnp.int32, z.shape, 1)
    sp = p_scale * _softplus(z + p_bias)
    return jnp.where(lane < FEAT_G, jax.nn.sigmoid(z), jnp.where(lane < FEAT_END, sp, 0.0))


def _split3(x):
    x1 = x.astype(BF16)
    r1 = x - x1.astype(F32)
    x2 = r1.astype(BF16)
    x3 = (r1 - x2.astype(F32)).astype(BF16)
    return x1, x2, x3


def _feat_prompt_kernel(h_ref, w_ref, pb_ref, ps_ref, feat_ref, cs_ref, cst_ref, *, t_pad):
    z = jnp.dot(h_ref[0], w_ref[...], preferred_element_type=F32)
    feat = _features(z, pb_ref[...], ps_ref[...])
    row = lax.broadcasted_iota(jnp.int32, feat.shape, 0)
    feat = jnp.where(row < FRONT_PAD, 0.0, feat)
    feat_ref[0] = feat
    r = lax.broadcasted_iota(jnp.int32, (CHUNK, CHUNK), 0)
    c = lax.broadcasted_iota(jnp.int32, (CHUNK, CHUNK), 1)
    tri = (r >= c).astype(BF16)
    for ch in range(t_pad // CHUNK):
        x1, x2, x3 = _split3(feat[ch * CHUNK:(ch + 1) * CHUNK])
        cs = (jnp.dot(tri, x1, preferred_element_type=F32)
              + jnp.dot(tri, x2, preferred_element_type=F32)
              + jnp.dot(tri, x3, preferred_element_type=F32))
        cs_ref[0, ch * CHUNK:(ch + 1) * CHUNK, :] = cs
        cst_ref[0, ch] = cs.T


def _feat_prompt(h3, w_small, p_bias, p_scale):
    bsz, t_pad, _ = h3.shape
    nch = t_pad // CHUNK
    return pl.pallas_call(
        functools.partial(_feat_prompt_kernel, t_pad=t_pad),
        grid=(bsz,),
        in_specs=[pl.BlockSpec((1, t_pad, D_MODEL), lambda b: (b, 0, 0)),
                  pl.BlockSpec((D_MODEL, LANES), lambda b: (0, 0)),
                  pl.BlockSpec((1, LANES), lambda b: (0, 0)),
                  pl.BlockSpec((1, LANES), lambda b: (0, 0))],
        out_specs=[pl.BlockSpec((1, t_pad, LANES), lambda b: (b, 0, 0)),
                   pl.BlockSpec((1, t_pad, LANES), lambda b: (b, 0, 0)),
                   pl.BlockSpec((1, nch, CHUNK, LANES), lambda b: (b, 0, 0, 0))],
        out_shape=[jax.ShapeDtypeStruct((bsz, t_pad, LANES), F32),
                   jax.ShapeDtypeStruct((bsz, t_pad, LANES), F32),
                   jax.ShapeDtypeStruct((bsz, nch, CHUNK, LANES), F32)],
        compiler_params=_cparams("arbitrary"),
        name="feat_prompt",
    )(h3, w_small, p_bias, p_scale)


def _feat_decode_kernel(h_ref, w_ref, pb_ref, ps_ref, feat_ref):
    z = jnp.dot(h_ref[...], w_ref[...], preferred_element_type=F32)
    feat_ref[...] = _features(z, pb_ref[...], ps_ref[...])


def _feat_decode(h, w_small, p_bias, p_scale):
    rows = h.shape[0]
    return pl.pallas_call(
        _feat_decode_kernel,
        out_shape=jax.ShapeDtypeStruct((rows, LANES), F32),
        compiler_params=pltpu.CompilerParams(vmem_limit_bytes=VMEM_LIMIT),
        name="feat_decode",
    )(h, w_small, p_bias, p_scale)


def _gdn_chunk_kernel(q_ref, k_ref, v_ref, zg_ref, feat_ref, cs_ref, cst_ref, nw_ref,
                      o_ref, sfin_ref, s_ref, *, nseq):
    ch = pl.program_id(1)

    @pl.when(ch == 0)
    def _():
        s_ref[...] = jnp.zeros(s_ref.shape, F32)

    row = lax.broadcasted_iota(jnp.int32, (CHUNK, CHUNK), 0)
    col = lax.broadcasted_iota(jnp.int32, (CHUNK, CHUNK), 1)
    incl = row >= col
    strict = row > col
    same_half = (row >= HALF) == (col >= HALF)
    eye = (row == col).astype(F32)
    nw = nw_ref[...]
    chains = [(j, h) for j in range(nseq) for h in range(GDN_HEADS)]

    def lanes(h):
        return slice(h * GDN_DK, (h + 1) * GDN_DK)

    def unit(x, scale):
        return x * (lax.rsqrt(jnp.sum(x * x, -1, keepdims=True) + EPS) * scale)

    q = [unit(q_ref[j, :, lanes(h)], GDN_DK ** -0.5) for j, h in chains]
    k = [unit(k_ref[j, :, lanes(h)], 1.0) for j, h in chains]
    beta = [feat_ref[j, :, FEAT_BETA + h:FEAT_BETA + h + 1] for j, h in chains]
    gc = [cs_ref[j, :, FEAT_G + h:FEAT_G + h + 1] for j, h in chains]
    gr = [cst_ref[j, 0, FEAT_G + h:FEAT_G + h + 1, :] for j, h in chains]
    decay = [jnp.exp(jnp.where(incl, c - r, NEG_BIG)) for c, r in zip(gc, gr)]
    kb = [x * b for x, b in zip(k, beta)]
    kbf = [x.astype(BF16) for x in k]
    a = [jnp.where(strict, _dot_nt(x, y) * d, 0.0) for x, y, d in zip(kb, kbf, decay)]
    a_off = [jnp.where(same_half, 0.0, x).astype(BF16) for x in a]
    p = [jnp.where(same_half, -x, 0.0) for x in a]
    inv = [eye + x for x in p]
    for _ in range(5):
        pb = [x.astype(BF16) for x in p]
        p = [_dot(x, x) for x in pb]
        inv = [x + _dot(x, y) for x, y in zip(inv, p)]
    invb = [x.astype(BF16) for x in inv]
    fold = [_dot(x, y) for x, y in zip(invb, a_off)]
    t_inv = [x - _dot(f, y) for x, f, y in zip(inv, fold, invb)]
    eg = [jnp.exp(c) for c in gc]
    rhs = [jnp.concatenate([v_ref[j, :, lanes(h)] * b, x * e], axis=1)
           for (j, h), b, x, e in zip(chains, beta, kb, eg)]
    uw = [_dot(t, r) for t, r in zip(t_inv, rhs)]
    attn = [_dot_nt(x, y) * d for x, y, d in zip(q, kbf, decay)]
    g_last = [c[CHUNK - 1:CHUNK, :] for c in gc]
    k_dec = [x * jnp.exp(l - c) for x, l, c in zip(k, g_last, gc)]
    s = [s_ref[i] for i in range(len(chains))]
    ws_qs = [_dot(jnp.concatenate([x[:, GDN_DK:], y * e], axis=0), z)
             for x, y, e, z in zip(uw, q, eg, s)]
    v_new = [(x[:, :GDN_DK] - y[:CHUNK]).astype(BF16) for x, y in zip(uw, ws_qs)]
    o = [y[CHUNK:] + _dot(x, z) for y, x, z in zip(ws_qs, attn, v_new)]
    for i, (j, h) in enumerate(chains):
        s_ref[i] = s[i] * jnp.exp(g_last[i]) + _dot_tn(k_dec[i], v_new[i])
        on = o[i] * lax.rsqrt(jnp.mean(o[i] * o[i], -1, keepdims=True) + EPS)
        o_ref[j, :, lanes(h)] = on * nw * zg_ref[j, :, lanes(h)]

    @pl.when(ch == pl.num_programs(1) - 1)
    def _():
        for j in range(nseq):
            sfin_ref[j] = s_ref[j * GDN_HEADS:(j + 1) * GDN_HEADS]


GDN_SEQ_PER_STEP = 2


def _gdn_chunk(qkv_act, z_act, feat, cs, cst, norm_w):
    bsz, t_pad, _ = qkv_act.shape
    nch = t_pad // CHUNK
    nseq = GDN_SEQ_PER_STEP if bsz % GDN_SEQ_PER_STEP == 0 else 1

    def qkv_spec(part):
        return pl.BlockSpec((nseq, CHUNK, GDN_QK), lambda b, c: (b, c, part))

    return pl.pallas_call(
        functools.partial(_gdn_chunk_kernel, nseq=nseq),
        grid=(bsz // nseq, nch),
        in_specs=[qkv_spec(0), qkv_spec(1), qkv_spec(2),
                  pl.BlockSpec((nseq, CHUNK, D_GDN), lambda b, c: (b, c, 0)),
                  pl.BlockSpec((nseq, CHUNK, LANES), lambda b, c: (b, c, 0)),
                  pl.BlockSpec((nseq, CHUNK, LANES), lambda b, c: (b, c, 0)),
                  pl.BlockSpec((nseq, 1, CHUNK, LANES), lambda b, c: (b, c, 0, 0)),
                  pl.BlockSpec((1, GDN_DK), lambda b, c: (0, 0))],
        out_specs=[pl.BlockSpec((nseq, CHUNK, D_GDN), lambda b, c: (b, c, 0)),
                   pl.BlockSpec((nseq, GDN_HEADS, GDN_DK, GDN_DK), lambda b, c: (b, 0, 0, 0))],
        out_shape=[jax.ShapeDtypeStruct((bsz, t_pad, D_GDN), F32),
                   jax.ShapeDtypeStruct((bsz, GDN_HEADS, GDN_DK, GDN_DK), F32)],
        scratch_shapes=[pltpu.VMEM((nseq * GDN_HEADS, GDN_DK, GDN_DK), F32)],
        compiler_params=_cparams("arbitrary", "arbitrary"),
        name="gdn_chunk",
    )(qkv_act, qkv_act, qkv_act, z_act, feat, cs, cst, norm_w.reshape(1, GDN_DK))


def _ssd_finish(y, zs, nw):
    yg = y * zs
    gw = D_SSM // SSM_GROUPS
    parts = []
    for g in range(SSM_GROUPS):
        blk = yg[:, g * gw:(g + 1) * gw]
        parts.append(blk * lax.rsqrt(jnp.mean(blk * blk, -1, keepdims=True) + EPS))
    return jnp.concatenate(parts, axis=1) * nw


def _ssd_chunk_kernel(x_ref, b_ref, c_ref, zs_ref, feat_ref, cs_ref, cst_ref, d_ref, nw_ref,
                      y_ref, sfin_ref, s_ref):
    ch = pl.program_id(1)

    @pl.when(ch == 0)
    def _():
        s_ref[...] = jnp.zeros(s_ref.shape, F32)

    row = lax.broadcasted_iota(jnp.int32, (CHUNK, CHUNK), 0)
    col = lax.broadcasted_iota(jnp.int32, (CHUNK, CHUNK), 1)
    incl = row >= col
    left = col < SSM_HEAD_DIM
    top = row < SSM_HEAD_DIM
    feat = feat_ref[0]
    cs = cs_ref[0]
    cst = cst_ref[0, 0]
    heads_per_group = SSM_HEADS // SSM_GROUPS
    ys = []
    for pair in range(SSM_HEADS // 2):
        grp = (2 * pair) // heads_per_group
        bm = b_ref[0, :, grp * SSM_STATE:(grp + 1) * SSM_STATE]
        cm = c_ref[0, :, grp * SSM_STATE:(grp + 1) * SSM_STATE]
        cb = _dot_nt(cm, bm)
        xp = x_ref[0, :, pair * LANES:(pair + 1) * LANES]
        h0 = 2 * pair
        h1 = h0 + 1

        def colv(base, h):
            return feat[:, base + h:base + h + 1]

        dt = jnp.where(left, colv(FEAT_DT, h0), colv(FEAT_DT, h1))
        xdt = xp * dt
        ac0 = cs[:, FEAT_LA + h0:FEAT_LA + h0 + 1]
        ac1 = cs[:, FEAT_LA + h1:FEAT_LA + h1 + 1]
        ar0 = cst[FEAT_LA + h0:FEAT_LA + h0 + 1, :]
        ar1 = cst[FEAT_LA + h1:FEAT_LA + h1 + 1, :]
        sc0 = cb * jnp.exp(jnp.where(incl, ac0 - ar0, NEG_BIG))
        sc1 = cb * jnp.exp(jnp.where(incl, ac1 - ar1, NEG_BIG))
        y_diag = jnp.where(left, _dot(sc0, xdt), _dot(sc1, xdt))
        s = s_ref[pair * LANES:(pair + 1) * LANES, :]
        y_off = _dot_nt(cm, s) * jnp.where(left, jnp.exp(ac0), jnp.exp(ac1))
        al0 = ac0[CHUNK - 1:CHUNK, :]
        al1 = ac1[CHUNK - 1:CHUNK, :]
        xe = xdt * jnp.where(left, jnp.exp(al0 - ac0), jnp.exp(al1 - ac1))
        ds = _dot_tn(xe, bm)
        s_ref[pair * LANES:(pair + 1) * LANES, :] = (
            s * jnp.where(top, jnp.exp(al0), jnp.exp(al1)) + ds)
        ys.append(y_diag + y_off + d_ref[:, pair * LANES:(pair + 1) * LANES] * xp)
    y_ref[0] = _ssd_finish(jnp.concatenate(ys, axis=1), zs_ref[0], nw_ref[...])

    @pl.when(ch == pl.num_programs(1) - 1)
    def _():
        sfin_ref[0] = s_ref[...]


def _ssd_chunk(xbc_act, z_act, feat, cs, cst, d_cols, norm_w):
    bsz, t_pad, _ = xbc_act.shape
    nch = t_pad // CHUNK
    bc_w = SSM_GROUPS * SSM_STATE
    return pl.pallas_call(
        _ssd_chunk_kernel,
        grid=(bsz, nch),
        in_specs=[pl.BlockSpec((1, CHUNK, D_SSM), lambda b, c: (b, c, 0)),
                  pl.BlockSpec((1, CHUNK, bc_w), lambda b, c: (b, c, D_SSM // bc_w)),
                  pl.BlockSpec((1, CHUNK, bc_w), lambda b, c: (b, c, D_SSM // bc_w + 1)),
                  pl.BlockSpec((1, CHUNK, D_SSM), lambda b, c: (b, c, 1)),
                  pl.BlockSpec((1, CHUNK, LANES), lambda b, c: (b, c, 0)),
                  pl.BlockSpec((1, CHUNK, LANES), lambda b, c: (b, c, 0)),
                  pl.BlockSpec((1, 1, CHUNK, LANES), lambda b, c: (b, c, 0, 0)),
                  pl.BlockSpec((1, D_SSM), lambda b, c: (0, 0)),
                  pl.BlockSpec((1, D_SSM), lambda b, c: (0, 0))],
        out_specs=[pl.BlockSpec((1, CHUNK, D_SSM), lambda b, c: (b, c, 0)),
                   pl.BlockSpec((1, SSM_HEADS * SSM_HEAD_DIM, SSM_STATE), lambda b, c: (b, 0, 0))],
        out_shape=[jax.ShapeDtypeStruct((bsz, t_pad, D_SSM), F32),
                   jax.ShapeDtypeStruct((bsz, SSM_HEADS * SSM_HEAD_DIM, SSM_STATE), F32)],
        scratch_shapes=[pltpu.VMEM((SSM_HEADS * SSM_HEAD_DIM, SSM_STATE), F32)],
        compiler_params=_cparams("arbitrary", "arbitrary"),
        name="ssd_chunk",
    )(xbc_act, xbc_act, xbc_act, z_act, feat, cs, cst, d_cols, norm_w.reshape(1, D_SSM))


def _out_proj_kernel(mg_ref, ms_ref, x_ref, w_ref, nw_ref, xo_ref, ho_ref, *, seq_rows):
    acc = (jnp.dot(mg_ref[...].astype(BF16), w_ref[0:D_GDN, :], preferred_element_type=F32)
           + jnp.dot(ms_ref[...].astype(BF16), w_ref[D_GDN:, :], preferred_element_type=F32))
    xn = x_ref[...] + acc
    if seq_rows:
        tile = xn.shape[0]
        start = lax.rem(pl.program_id(0) * tile, seq_rows)
        r = start + lax.broadcasted_iota(jnp.int32, xn.shape, 0)
        pad = (r < FRONT_PAD) | ((r >= seq_rows) & (r < seq_rows + FRONT_PAD))
        xn = jnp.where(pad, 0.0, xn)
    xo_ref[...] = xn
    hn = xn * lax.rsqrt(jnp.mean(xn * xn, -1, keepdims=True) + EPS) * nw_ref[...]
    ho_ref[...] = hn.astype(ho_ref.dtype)


def _out_proj(mix_g, mix_s, x, w_out, next_norm_w, h_dtype, tile, seq_rows):
    rows = x.shape[0]
    row_spec = pl.BlockSpec((tile, D_MODEL), lambda i: (i, 0))
    return pl.pallas_call(
        functools.partial(_out_proj_kernel, seq_rows=seq_rows),
        grid=(rows // tile,),
        in_specs=[row_spec, row_spec, row_spec,
                  pl.BlockSpec((D_GDN + D_SSM, D_MODEL), lambda i: (0, 0)),
                  pl.BlockSpec((1, D_MODEL), lambda i: (0, 0))],
        out_specs=[row_spec, row_spec],
        out_shape=[jax.ShapeDtypeStruct((rows, D_MODEL), F32),
                   jax.ShapeDtypeStruct((rows, D_MODEL), h_dtype)],
        compiler_params=_cparams("arbitrary"),
        name="out_proj",
    )(mix_g, mix_s, x, w_out, next_norm_w.reshape(1, D_MODEL))


def _proj_conv_decode_kernel(h_ref, w_ref, buf_ref, cw_ref, cb_ref, out_ref, nbuf_ref):
    z = jnp.dot(h_ref[...], w_ref[...], preferred_element_type=F32)
    cw = cw_ref[...]
    acc = cb_ref[...] + cw[3:4] * z
    for tap in range(CONV_W - 1):
        acc = acc + cw[tap:tap + 1] * buf_ref[tap]
    out_ref[...] = _silu(acc)
    nbuf_ref[0] = buf_ref[1]
    nbuf_ref[1] = buf_ref[2]
    nbuf_ref[2] = z


def _proj_silu_decode_kernel(h_ref, w_ref, out_ref):
    out_ref[...] = _silu(jnp.dot(h_ref[...], w_ref[...], preferred_element_type=F32))


def _proj_conv_decode(h, w, buf, cw, cb):
    rows = h.shape[0]
    ncols = w.shape[1]
    return pl.pallas_call(
        _proj_conv_decode_kernel,
        grid=(ncols // COL_BLOCK,),
        in_specs=[pl.BlockSpec((rows, D_MODEL), lambda j: (0, 0)),
                  pl.BlockSpec((D_MODEL, COL_BLOCK), lambda j: (0, j)),
                  pl.BlockSpec((CONV_W - 1, rows, COL_BLOCK), lambda j: (0, 0, j)),
                  pl.BlockSpec((CONV_W, COL_BLOCK), lambda j: (0, j)),
                  pl.BlockSpec((1, COL_BLOCK), lambda j: (0, j))],
        out_specs=[pl.BlockSpec((rows, COL_BLOCK), lambda j: (0, j)),
                   pl.BlockSpec((CONV_W - 1, rows, COL_BLOCK), lambda j: (0, 0, j))],
        out_shape=[jax.ShapeDtypeStruct((rows, ncols), F32),
                   jax.ShapeDtypeStruct((CONV_W - 1, rows, ncols), F32)],
        compiler_params=_cparams("arbitrary"),
        name="proj_conv_decode",
    )(h, w, buf, cw, cb.reshape(1, ncols))


def _proj_silu_decode(h, w):
    rows = h.shape[0]
    ncols = w.shape[1]
    return pl.pallas_call(
        _proj_silu_decode_kernel,
        grid=(ncols // COL_BLOCK,),
        in_specs=[pl.BlockSpec((rows, D_MODEL), lambda j: (0, 0)),
                  pl.BlockSpec((D_MODEL, COL_BLOCK), lambda j: (0, j))],
        out_specs=pl.BlockSpec((rows, COL_BLOCK), lambda j: (0, j)),
        out_shape=jax.ShapeDtypeStruct((rows, ncols), F32),
        compiler_params=_cparams("arbitrary"),
        name="proj_silu_decode",
    )(h, w)


TOK_BLOCK = 8


def _gdn_step_kernel(qt_ref, kt_ref, v_ref, zg_ref, feat_ref, nw_ref, s_ref, *rest):
    o_ref, so_ref = rest[-2:]
    feat = feat_ref[...]
    vrows = v_ref[...]
    for h in range(GDN_HEADS):
        sl = slice(h * GDN_DK, (h + 1) * GDN_DK)
        qt = qt_ref[0, h]
        kt = kt_ref[0, h]
        qt = qt * (lax.rsqrt(jnp.sum(qt * qt, 0, keepdims=True) + EPS) * (GDN_DK ** -0.5))
        kt = kt * lax.rsqrt(jnp.sum(kt * kt, 0, keepdims=True) + EPS)
        for j in range(TOK_BLOCK):
            kc = kt[:, j:j + 1]
            qc = qt[:, j:j + 1]
            beta = feat[j:j + 1, FEAT_BETA + h:FEAT_BETA + h + 1]
            g = feat[j:j + 1, FEAT_G + h:FEAT_G + h + 1]
            s = s_ref[j, h] * jnp.exp(g)
            kv = jnp.sum(s * kc, 0, keepdims=True)
            d = (vrows[j:j + 1, sl] - kv) * beta
            s = s + kc * d
            so_ref[j, h] = s
            o_ref[j:j + 1, sl] = jnp.sum(s * qc, 0, keepdims=True)
    nw = nw_ref[...]
    for h in range(GDN_HEADS):
        sl = slice(h * GDN_DK, (h + 1) * GDN_DK)
        o = o_ref[:, sl]
        o_ref[:, sl] = o * lax.rsqrt(jnp.mean(o * o, -1, keepdims=True) + EPS) * nw * zg_ref[:, sl]


def _layer_state_call(kernel_fn, name, layer, states, acc, in_specs, out_specs, out_shapes, args):
    rows = states.shape[1]
    blk = (None, TOK_BLOCK) + states.shape[2:]
    zeros = (0,) * (states.ndim - 2)
    state_spec = pl.BlockSpec(blk, lambda i: (layer, i) + zeros)
    in_specs = in_specs + [state_spec]
    args = args + [states]
    aliases = {}
    if acc is not None:
        in_specs = in_specs + [pl.BlockSpec(memory_space=pl.ANY)]
        args = args + [acc]
        aliases = {len(args) - 1: len(out_shapes)}
    return pl.pallas_call(
        kernel_fn,
        grid=(rows // TOK_BLOCK,),
        in_specs=in_specs,
        out_specs=out_specs + [state_spec],
        out_shape=out_shapes + [jax.ShapeDtypeStruct(states.shape, F32)],
        input_output_aliases=aliases,
        compiler_params=_cparams("arbitrary"),
        name=name,
    )(*args)


def _gdn_step(layer, qt, kt, qkv_act, z_act, feat, norm_w, states, acc):
    rows = qkv_act.shape[0]
    return _layer_state_call(
        _gdn_step_kernel, "gdn_step", layer, states, acc,
        [pl.BlockSpec((1, GDN_HEADS, GDN_DK, TOK_BLOCK), lambda i: (i, 0, 0, 0)),
         pl.BlockSpec((1, GDN_HEADS, GDN_DK, TOK_BLOCK), lambda i: (i, 0, 0, 0)),
         pl.BlockSpec((TOK_BLOCK, D_GDN), lambda i: (i, 2)),
         pl.BlockSpec((TOK_BLOCK, D_GDN), lambda i: (i, 0)),
         pl.BlockSpec((TOK_BLOCK, LANES), lambda i: (i, 0)),
         pl.BlockSpec((1, GDN_DK), lambda i: (0, 0))],
        [pl.BlockSpec((TOK_BLOCK, D_GDN), lambda i: (i, 0))],
        [jax.ShapeDtypeStruct((rows, D_GDN), F32)],
        [qt, kt, qkv_act, z_act, feat, norm_w.reshape(1, GDN_DK)])


def _ssd_step_kernel(xt_ref, b_ref, c_ref, feat_ref, s_ref, *rest):
    yt_ref, so_ref = rest[-2:]
    feat = feat_ref[...]
    heads_per_group = SSM_HEADS // SSM_GROUPS
    ela = jnp.exp(feat)
    toks = range(TOK_BLOCK)
    for h in range(SSM_HEADS):
        grp = h // heads_per_group
        gl = slice(grp * SSM_STATE, (grp + 1) * SSM_STATE)
        xt = xt_ref[0, h]
        xdt = [xt[:, j:j + 1] * feat[j:j + 1, FEAT_DT + h:FEAT_DT + h + 1] for j in toks]
        xb = [jnp.broadcast_to(x, (SSM_HEAD_DIM, SSM_STATE)) for x in xdt]
        s = [s_ref[j, h] * ela[j:j + 1, FEAT_LA + h:FEAT_LA + h + 1] + xb[j] * b_ref[j:j + 1, gl]
             for j in toks]
        for j in toks:
            so_ref[j, h] = s[j]
        y = [jnp.sum(s[j] * c_ref[j:j + 1, gl], -1, keepdims=True) for j in toks]
        yt_ref[0, h] = jnp.concatenate(y, axis=1)


def _ssd_step(layer, xt, xbc_act, feat, states, acc):
    rows = xbc_act.shape[0]
    ngrp = rows // TOK_BLOCK
    bc_w = SSM_GROUPS * SSM_STATE
    return _layer_state_call(
        _ssd_step_kernel, "ssd_step", layer, states, acc,
        [pl.BlockSpec((1, SSM_HEADS, SSM_HEAD_DIM, TOK_BLOCK), lambda i: (i, 0, 0, 0)),
         pl.BlockSpec((TOK_BLOCK, bc_w), lambda i: (i, D_SSM // bc_w)),
         pl.BlockSpec((TOK_BLOCK, bc_w), lambda i: (i, D_SSM // bc_w + 1)),
         pl.BlockSpec((TOK_BLOCK, LANES), lambda i: (i, 0))],
        [pl.BlockSpec((1, SSM_HEADS, SSM_HEAD_DIM, TOK_BLOCK), lambda i: (i, 0, 0, 0))],
        [jax.ShapeDtypeStruct((ngrp, SSM_HEADS, SSM_HEAD_DIM, TOK_BLOCK), F32)],
        [xt, xbc_act, xbc_act, feat])


def _ssd_finish_decode_kernel(y_ref, x_ref, zs_ref, d_ref, nw_ref, out_ref):
    out_ref[...] = _ssd_finish(y_ref[...] + d_ref[...] * x_ref[...], zs_ref[...], nw_ref[...])


def _ssd_finish_decode(y, xbc_act, z_act, d_cols, norm_w):
    rows = y.shape[0]
    return pl.pallas_call(
        _ssd_finish_decode_kernel,
        grid=(1,),
        in_specs=[pl.BlockSpec((rows, D_SSM), lambda i: (0, 0)),
                  pl.BlockSpec((rows, D_SSM), lambda i: (0, 0)),
                  pl.BlockSpec((rows, D_SSM), lambda i: (0, 1)),
                  pl.BlockSpec((1, D_SSM), lambda i: (0, 0)),
                  pl.BlockSpec((1, D_SSM), lambda i: (0, 0))],
        out_specs=pl.BlockSpec((rows, D_SSM), lambda i: (0, 0)),
        out_shape=jax.ShapeDtypeStruct((rows, D_SSM), F32),
        compiler_params=_cparams("arbitrary"),
        name="ssd_finish_decode",
    )(y, xbc_act, z_act, d_cols, norm_w.reshape(1, D_SSM))


def _layer_weights(w_in, gdn_A_log, gdn_dt_bias, ssm_A_log, ssm_dt_bias, ssm_D):
    w_qkv = w_in[:, :OFF_Z_G].astype(BF16)
    w_xbc = w_in[:, OFF_XBC:OFF_Z_S].astype(BF16)
    w_z = jnp.concatenate([w_in[:, OFF_Z_G:OFF_B], w_in[:, OFF_Z_S:OFF_DT]], axis=1).astype(BF16)
    w_dt = w_in[:, OFF_DT:]
    w_small = jnp.concatenate(
        [w_in[:, OFF_B:OFF_XBC], w_dt, w_dt, jnp.zeros((D_MODEL, LANES - FEAT_END), F32)],
        axis=1).astype(BF16)
    zeros8 = jnp.zeros((GDN_HEADS,), F32)
    tail = jnp.zeros((LANES - FEAT_END,), F32)
    p_bias = jnp.concatenate([zeros8, gdn_dt_bias.astype(F32), ssm_dt_bias.astype(F32),
                              ssm_dt_bias.astype(F32), tail]).reshape(1, LANES)
    p_scale = jnp.concatenate([zeros8, -jnp.exp(gdn_A_log.astype(F32)),
                               jnp.ones((SSM_HEADS,), F32), -jnp.exp(ssm_A_log.astype(F32)),
                               tail]).reshape(1, LANES)
    d_cols = jnp.repeat(ssm_D.astype(F32), SSM_HEAD_DIM).reshape(1, D_SSM)
    return w_qkv, w_xbc, w_z, w_small, p_bias, p_scale, d_cols


def kernel(x_prompt, x_sample, state_gdn, state_gdn_conv, state_ssm, state_ssm_conv, meta_tokens,
           norm_w, w_in, gdn_conv_w, gdn_A_log, gdn_dt_bias, gdn_norm_w, ssm_conv_w, ssm_conv_b,
           ssm_A_log, ssm_dt_bias, ssm_D, ssm_norm_w, w_out, final_norm_w):
    bp, seq, _ = x_prompt.shape
    bd = x_sample.shape[0]
    t_pad = FRONT_PAD + N_META + seq
    meta = jnp.broadcast_to(meta_tokens.astype(F32)[None], (bp, N_META, D_MODEL))
    xp = jnp.concatenate([jnp.zeros((bp, FRONT_PAD, D_MODEL), F32), meta, x_prompt.astype(F32)],
                         axis=1).reshape(bp * t_pad, D_MODEL)
    xs = x_sample.astype(F32).reshape(bd, D_MODEL)
    hp = _rmsnorm(xp, norm_w[0], ROW_TILE)
    hs = _rmsnorm(xs, norm_w[0], bd)
    zero_bias = jnp.zeros((GDN_CONV_DIM,), F32)
    ngrp = bd // TOK_BLOCK
    p_g, p_gc, p_s, p_sc = [], [], [], []
    d_g, d_s = None, None
    d_gc, d_sc = [], []
    for l in range(DEPTH):
        w_qkv, w_xbc, w_z, w_small, p_bias, p_scale, d_cols = _layer_weights(
            w_in[l], gdn_A_log[l], gdn_dt_bias[l], ssm_A_log[l], ssm_dt_bias[l], ssm_D[l])
        w_o = w_out[l].astype(BF16)
        last = l == DEPTH - 1
        nxt_w = final_norm_w if last else norm_w[l + 1]
        h_dtype = F32 if last else BF16
        gcw = gdn_conv_w[l].astype(F32)
        scw = ssm_conv_w[l].astype(F32)
        scb = ssm_conv_b[l].astype(F32)

        h3 = hp.reshape(bp, t_pad, D_MODEL)
        qkv_act, qkv_tail = _proj_conv(h3, w_qkv, gcw, zero_bias)
        xbc_act, xbc_tail = _proj_conv(h3, w_xbc, scw, scb)
        z_act = _proj_silu(h3, w_z)
        feat, cs, cst = _feat_prompt(h3, w_small, p_bias, p_scale)
        mix_g, sg = _gdn_chunk(qkv_act, z_act, feat, cs, cst, gdn_norm_w[l].astype(F32))
        mix_s, ss = _ssd_chunk(xbc_act, z_act, feat, cs, cst, d_cols, ssm_norm_w[l].astype(F32))
        xp, hp = _out_proj(mix_g.reshape(bp * t_pad, D_GDN), mix_s.reshape(bp * t_pad, D_SSM), xp,
                           w_o, nxt_w.astype(F32), h_dtype, ROW_TILE, t_pad)
        p_g.append(sg)
        p_gc.append(qkv_tail[:, SUBLANES - (CONV_W - 1):, :])
        p_s.append(ss.reshape(bp, SSM_HEADS, SSM_HEAD_DIM, SSM_STATE))
        p_sc.append(xbc_tail[:, SUBLANES - (CONV_W - 1):, :])

        gbuf = jnp.transpose(state_gdn_conv[l].astype(F32), (1, 0, 2))
        sbuf = jnp.transpose(state_ssm_conv[l].astype(F32), (1, 0, 2))
        qkv_d, gbuf_n = _proj_conv_decode(hs, w_qkv, gbuf, gcw, zero_bias)
        xbc_d, sbuf_n = _proj_conv_decode(hs, w_xbc, sbuf, scw, scb)
        z_d = _proj_silu_decode(hs, w_z)
        feat_d = _feat_decode(hs, w_small, p_bias, p_scale)

        def head_major(a, heads, dim):
            return jnp.transpose(a.reshape(ngrp, TOK_BLOCK, heads, dim), (0, 2, 3, 1))

        qt = head_major(qkv_d[:, :GDN_QK], GDN_HEADS, GDN_DK)
        kt = head_major(qkv_d[:, GDN_QK:2 * GDN_QK], GDN_HEADS, GDN_DK)
        mix_gd, d_g = _gdn_step(l, qt, kt, qkv_d, z_d, feat_d, gdn_norm_w[l].astype(F32),
                                state_gdn.astype(F32), d_g)
        xt = head_major(xbc_d[:, :D_SSM], SSM_HEADS, SSM_HEAD_DIM)
        yt, d_s = _ssd_step(l, xt, xbc_d, feat_d, state_ssm.astype(F32), d_s)
        y_d = jnp.transpose(yt, (0, 3, 1, 2)).reshape(bd, D_SSM)
        mix_sd = _ssd_finish_decode(y_d, xbc_d, z_d, d_cols, ssm_norm_w[l].astype(F32))
        xs, hs = _out_proj(mix_gd, mix_sd, xs, w_o, nxt_w.astype(F32), h_dtype, bd, 0)
        d_gc.append(jnp.transpose(gbuf_n, (1, 0, 2)))
        d_sc.append(jnp.transpose(sbuf_n, (1, 0, 2)))

    y_prompt = hp.reshape(bp, t_pad, D_MODEL)[:, FRONT_PAD + N_META:]
    y_sample = hs.reshape(bd, 1, D_MODEL)
    return (y_prompt, y_sample,
            jnp.stack(p_g), jnp.stack(p_gc), jnp.stack(p_s), jnp.stack(p_sc),
            d_g, jnp.stack(d_gc), d_s, jnp.stack(d_sc))
```

```python
import functools

import jax
import jax.numpy as jnp
from jax import lax
from jax.experimental import pallas as pl
from jax.experimental.pallas import tpu as pltpu

F32 = jnp.float32
BF16 = jnp.bfloat16

D_MODEL = 1024
DEPTH = 4
N_META = 16
GDN_HEADS = 8
GDN_DK = 128
GDN_QK = 1024
D_GDN = 1024
GDN_CONV_DIM = 3072
D_SSM = 1024
SSM_HEADS = 16
SSM_HEAD_DIM = 64
SSM_GROUPS = 2
SSM_STATE = 128
SSM_CONV_DIM = 1536
CONV_W = 4
EPS = 1e-6
OFF_Z_G = GDN_CONV_DIM
OFF_B = OFF_Z_G + D_GDN
OFF_A = OFF_B + GDN_HEADS
OFF_XBC = OFF_A + GDN_HEADS
OFF_Z_S = OFF_XBC + SSM_CONV_DIM
OFF_DT = OFF_Z_S + D_SSM

CHUNK = 128
HALF = CHUNK // 2
FRONT_PAD = 112
SUBLANES = 8
LANES = 128
COL_BLOCK = 512
ROW_TILE = 512
NEG_BIG = -1e30
VMEM_LIMIT = 56 * 1024 * 1024

FEAT_BETA = 0
FEAT_G = 8
FEAT_DT = 16
FEAT_LA = 32
FEAT_END = 48


def _cparams(*sem, flags=None):
    return pltpu.CompilerParams(dimension_semantics=sem, vmem_limit_bytes=VMEM_LIMIT, flags=flags)


def _dot(a, b):
    return jnp.dot(a.astype(BF16), b.astype(BF16), preferred_element_type=F32)


def _dot_nt(a, b):
    return lax.dot_general(a.astype(BF16), b.astype(BF16), (((1,), (1,)), ((), ())),
                           preferred_element_type=F32)


def _dot_tn(a, b):
    return lax.dot_general(a.astype(BF16), b.astype(BF16), (((0,), (0,)), ((), ())),
                           preferred_element_type=F32)


def _silu(y):
    return y * jax.nn.sigmoid(y)


def _softplus(x):
    return jnp.maximum(x, 0.0) + jnp.log1p(jnp.exp(-jnp.abs(x)))


def _rmsnorm_kernel(x_ref, w_ref, h_ref):
    x = x_ref[...]
    y = x * lax.rsqrt(jnp.mean(x * x, -1, keepdims=True) + EPS) * w_ref[...]
    h_ref[...] = y.astype(h_ref.dtype)


def _rmsnorm(x, w, tile):
    rows = x.shape[0]
    return pl.pallas_call(
        _rmsnorm_kernel,
        grid=(rows // tile,),
        in_specs=[pl.BlockSpec((tile, D_MODEL), lambda i: (i, 0)),
                  pl.BlockSpec((1, D_MODEL), lambda i: (0, 0))],
        out_specs=pl.BlockSpec((tile, D_MODEL), lambda i: (i, 0)),
        out_shape=jax.ShapeDtypeStruct((rows, D_MODEL), BF16),
        compiler_params=_cparams("arbitrary"),
        name="rmsnorm",
    )(x, w.reshape(1, D_MODEL))


PROJ_ROW_BLOCKS = 8
PROJ_LOOKAHEAD = 1


def _proj_conv_kernel(h_ref, w_ref, cw_ref, cb_ref, out_ref, tail_ref, *z_refs, t_pad, bias):
    ncol = out_ref.shape[2]
    rb = t_pad // PROJ_ROW_BLOCKS
    cw = cw_ref[...]
    taps = [cw[j:j + 1] for j in range(CONV_W)]
    cb = cb_ref[...]
    first_row = lax.broadcasted_iota(jnp.int32, (SUBLANES, ncol), 0) == 0
    dyn_zero = jnp.minimum(pl.program_id(0), 0)

    def matmul(r):
        z_refs[r][...] = jnp.dot(h_ref[0, r * rb:(r + 1) * rb, :], w_ref[...],
                                 preferred_element_type=F32)

    def shift_in(x, carry):
        rolled = pltpu.roll(x, 1, 0)
        return jnp.where(first_row, carry, rolled), rolled

    def conv_tile(cur, carries):
        acc = taps[0] * cur
        new_carries = []
        for tap in range(1, CONV_W):
            shifted, rolled = shift_in(acc, carries[tap - 1])
            new_carries.append(rolled)
            acc = taps[tap] * cur + shifted
        return acc, new_carries

    def conv(r):
        zero = jnp.zeros((SUBLANES, ncol), F32)
        if r == 0:
            carries = [zero] * (CONV_W - 1)
        else:
            _, carries = conv_tile(z_refs[r - 1][rb - SUBLANES:rb, :], [zero] * (CONV_W - 1))
        for t in range(rb // SUBLANES):
            start = pl.multiple_of(dyn_zero + t * SUBLANES, SUBLANES)
            acc, carries = conv_tile(z_refs[r][pl.ds(start, SUBLANES), :], carries)
            if bias:
                acc = acc + cb
            row0 = r * rb + t * SUBLANES
            out_ref[0, row0:row0 + SUBLANES, :] = _silu(acc)

    for r in range(PROJ_ROW_BLOCKS + PROJ_LOOKAHEAD):
        if r < PROJ_ROW_BLOCKS:
            matmul(r)
        if r >= PROJ_LOOKAHEAD:
            conv(r - PROJ_LOOKAHEAD)
    tail_ref[0] = z_refs[PROJ_ROW_BLOCKS - 1][rb - SUBLANES:rb, :]


def _proj_silu_kernel(h_ref, w_ref, out_ref):
    out_ref[0] = _silu(jnp.dot(h_ref[0], w_ref[...], preferred_element_type=F32))


def _proj_conv(h3, w, cw, cb):
    bsz, t_pad, _ = h3.shape
    ncols = w.shape[1]
    bias = cb is not None
    if not bias:
        cb = jnp.zeros((ncols,), F32)
    return pl.pallas_call(
        functools.partial(_proj_conv_kernel, t_pad=t_pad, bias=bias),
        grid=(bsz, ncols // COL_BLOCK),
        in_specs=[pl.BlockSpec((1, t_pad, D_MODEL), lambda b, j: (b, 0, 0)),
                  pl.BlockSpec((D_MODEL, COL_BLOCK), lambda b, j: (0, j)),
                  pl.BlockSpec((CONV_W, COL_BLOCK), lambda b, j: (0, j)),
                  pl.BlockSpec((1, COL_BLOCK), lambda b, j: (0, j))],
        out_specs=[pl.BlockSpec((1, t_pad, COL_BLOCK), lambda b, j: (b, 0, j)),
                   pl.BlockSpec((1, SUBLANES, COL_BLOCK), lambda b, j: (b, 0, j))],
        out_shape=[jax.ShapeDtypeStruct((bsz, t_pad, ncols), F32),
                   jax.ShapeDtypeStruct((bsz, SUBLANES, ncols), F32)],
        scratch_shapes=[pltpu.VMEM((t_pad // PROJ_ROW_BLOCKS, COL_BLOCK), F32)] * PROJ_ROW_BLOCKS,
        compiler_params=_cparams("arbitrary", "arbitrary"),
        name="proj_conv",
    )(h3, w, cw, cb.reshape(1, ncols))


def _proj_silu(h3, w):
    bsz, t_pad, _ = h3.shape
    ncols = w.shape[1]
    return pl.pallas_call(
        _proj_silu_kernel,
        grid=(bsz, ncols // COL_BLOCK),
        in_specs=[pl.BlockSpec((1, t_pad, D_MODEL), lambda b, j: (b, 0, 0)),
                  pl.BlockSpec((D_MODEL, COL_BLOCK), lambda b, j: (0, j))],
        out_specs=pl.BlockSpec((1, t_pad, COL_BLOCK), lambda b, j: (b, 0, j)),
        out_shape=jax.ShapeDtypeStruct((bsz, t_pad, ncols), F32),
        compiler_params=_cparams("arbitrary", "arbitrary"),
        name="proj_silu",
    )(h3, w)


def _features(z, p_bias, p_scale):
    lane = lax.broadcasted_iota(jnp.int32, z.shape, 1)
    sp = p_scale * _softplus(z + p_bias)
    return jnp.where(lane < FEAT_G, jax.nn.sigmoid(z), jnp.where(lane < FEAT_END, sp, 0.0))


def _split3(x):
    x1 = x.astype(BF16)
    r1 = x - x1.astype(F32)
    x2 = r1.astype(BF16)
    x3 = (r1 - x2.astype(F32)).astype(BF16)
    return x1, x2, x3


def _feat_prompt_kernel(h_ref, w_ref, pb_ref, ps_ref, feat_ref, featt_ref, cs_ref, cst_ref, *,
                        t_pad):
    z = jnp.dot(h_ref[0], w_ref[...], preferred_element_type=F32)
    feat = _features(z, pb_ref[...], ps_ref[...])
    row = lax.broadcasted_iota(jnp.int32, feat.shape, 0)
    feat = jnp.where(row < FRONT_PAD, 0.0, feat)
    feat_ref[0] = feat
    r = lax.broadcasted_iota(jnp.int32, (CHUNK, CHUNK), 0)
    c = lax.broadcasted_iota(jnp.int32, (CHUNK, CHUNK), 1)
    tri = (r >= c).astype(BF16)
    for ch in range(t_pad // CHUNK):
        tile = feat[ch * CHUNK:(ch + 1) * CHUNK]
        featt_ref[0, ch] = tile.T
        x1, x2, x3 = _split3(tile)
        cs = (jnp.dot(tri, x1, preferred_element_type=F32)
              + jnp.dot(tri, x2, preferred_element_type=F32)
              + jnp.dot(tri, x3, preferred_element_type=F32))
        cs_ref[0, ch * CHUNK:(ch + 1) * CHUNK, :] = cs
        cst_ref[0, ch] = cs.T


def _feat_prompt(h3, w_small, p_bias, p_scale):
    bsz, t_pad, _ = h3.shape
    nch = t_pad // CHUNK
    return pl.pallas_call(
        functools.partial(_feat_prompt_kernel, t_pad=t_pad),
        grid=(bsz,),
        in_specs=[pl.BlockSpec((1, t_pad, D_MODEL), lambda b: (b, 0, 0)),
                  pl.BlockSpec((D_MODEL, LANES), lambda b: (0, 0)),
                  pl.BlockSpec((1, LANES), lambda b: (0, 0)),
                  pl.BlockSpec((1, LANES), lambda b: (0, 0))],
        out_specs=[pl.BlockSpec((1, t_pad, LANES), lambda b: (b, 0, 0)),
                   pl.BlockSpec((1, nch, CHUNK, LANES), lambda b: (b, 0, 0, 0)),
                   pl.BlockSpec((1, t_pad, LANES), lambda b: (b, 0, 0)),
                   pl.BlockSpec((1, nch, CHUNK, LANES), lambda b: (b, 0, 0, 0))],
        out_shape=[jax.ShapeDtypeStruct((bsz, t_pad, LANES), F32),
                   jax.ShapeDtypeStruct((bsz, nch, CHUNK, LANES), F32),
                   jax.ShapeDtypeStruct((bsz, t_pad, LANES), F32),
                   jax.ShapeDtypeStruct((bsz, nch, CHUNK, LANES), F32)],
        compiler_params=_cparams("arbitrary"),
        name="feat_prompt",
    )(h3, w_small, p_bias, p_scale)


def _feat_decode_kernel(h_ref, w_ref, pb_ref, ps_ref, feat_ref):
    z = jnp.dot(h_ref[...], w_ref[...], preferred_element_type=F32)
    feat_ref[...] = _features(z, pb_ref[...], ps_ref[...])


def _feat_decode(h, w_small, p_bias, p_scale):
    rows = h.shape[0]
    return pl.pallas_call(
        _feat_decode_kernel,
        out_shape=jax.ShapeDtypeStruct((rows, LANES), F32),
        compiler_params=pltpu.CompilerParams(vmem_limit_bytes=VMEM_LIMIT),
        name="feat_decode",
    )(h, w_small, p_bias, p_scale)


def _gdn_chunk_kernel(q_ref, k_ref, v_ref, zg_ref, feat_ref, cs_ref, cst_ref, nw_ref,
                      o_ref, sfin_ref, s_ref, *, nseq):
    ch = pl.program_id(1)

    @pl.when(ch == 0)
    def _():
        s_ref[...] = jnp.zeros(s_ref.shape, F32)

    row = lax.broadcasted_iota(jnp.int32, (CHUNK, CHUNK), 0)
    col = lax.broadcasted_iota(jnp.int32, (CHUNK, CHUNK), 1)
    incl = row >= col
    strict = row > col
    same_half = (row >= HALF) == (col >= HALF)
    eye = (row == col).astype(F32)
    nw = nw_ref[...]
    chains = [(j, h) for j in range(nseq) for h in range(GDN_HEADS)]

    def lanes(h):
        return slice(h * GDN_DK, (h + 1) * GDN_DK)

    def unit(x, scale):
        return x * (lax.rsqrt(jnp.sum(x * x, -1, keepdims=True) + EPS) * scale)

    q = [unit(q_ref[j, :, lanes(h)], GDN_DK ** -0.5) for j, h in chains]
    k = [unit(k_ref[j, :, lanes(h)], 1.0) for j, h in chains]
    beta = [feat_ref[j, :, FEAT_BETA + h:FEAT_BETA + h + 1] for j, h in chains]
    gc = [cs_ref[j, :, FEAT_G + h:FEAT_G + h + 1] for j, h in chains]
    gr = [cst_ref[j, 0, FEAT_G + h:FEAT_G + h + 1, :] for j, h in chains]
    decay = [jnp.exp(jnp.where(incl, c - r, NEG_BIG)) for c, r in zip(gc, gr)]
    kb = [x * b for x, b in zip(k, beta)]
    kbf = [x.astype(BF16) for x in k]
    a = [jnp.where(strict, _dot_nt(x, y) * d, 0.0) for x, y, d in zip(kb, kbf, decay)]
    a_off = [jnp.where(same_half, 0.0, x).astype(BF16) for x in a]
    p = [jnp.where(same_half, -x, 0.0) for x in a]
    inv = [eye + x for x in p]
    for _ in range(5):
        pb = [x.astype(BF16) for x in p]
        p = [_dot(x, x) for x in pb]
        inv = [x + _dot(x, y) for x, y in zip(inv, p)]
    invb = [x.astype(BF16) for x in inv]
    fold = [_dot(x, y) for x, y in zip(invb, a_off)]
    t_inv = [x - _dot(f, y) for x, f, y in zip(inv, fold, invb)]
    eg = [jnp.exp(c) for c in gc]
    rhs = [jnp.concatenate([v_ref[j, :, lanes(h)] * b, x * e], axis=1)
           for (j, h), b, x, e in zip(chains, beta, kb, eg)]
    uw = [_dot(t, r) for t, r in zip(t_inv, rhs)]
    attn = [_dot_nt(x, y) * d for x, y, d in zip(q, kbf, decay)]
    g_last = [c[CHUNK - 1:CHUNK, :] for c in gc]
    k_dec = [x * jnp.exp(l - c) for x, l, c in zip(k, g_last, gc)]
    s = [s_ref[i] for i in range(len(chains))]
    ws_qs = [_dot(jnp.concatenate([x[:, GDN_DK:], y * e], axis=0), z)
             for x, y, e, z in zip(uw, q, eg, s)]
    v_new = [(x[:, :GDN_DK] - y[:CHUNK]).astype(BF16) for x, y in zip(uw, ws_qs)]
    o = [y[CHUNK:] + _dot(x, z) for y, x, z in zip(ws_qs, attn, v_new)]
    for i, (j, h) in enumerate(chains):
        s_ref[i] = s[i] * jnp.exp(g_last[i]) + _dot_tn(k_dec[i], v_new[i])
        on = o[i] * lax.rsqrt(jnp.mean(o[i] * o[i], -1, keepdims=True) + EPS)
        o_ref[j, :, lanes(h)] = on * nw * zg_ref[j, :, lanes(h)]

    @pl.when(ch == pl.num_programs(1) - 1)
    def _():
        for j in range(nseq):
            sfin_ref[j] = s_ref[j * GDN_HEADS:(j + 1) * GDN_HEADS]


GDN_SEQ_PER_STEP = 2


def _gdn_chunk(qkv_act, z_act, feat, cs, cst, norm_w):
    bsz, t_pad, _ = qkv_act.shape
    nch = t_pad // CHUNK
    nseq = GDN_SEQ_PER_STEP if bsz % GDN_SEQ_PER_STEP == 0 else 1

    def qkv_spec(part):
        return pl.BlockSpec((nseq, CHUNK, GDN_QK), lambda b, c: (b, c, part))

    return pl.pallas_call(
        functools.partial(_gdn_chunk_kernel, nseq=nseq),
        grid=(bsz // nseq, nch),
        in_specs=[qkv_spec(0), qkv_spec(1), qkv_spec(2),
                  pl.BlockSpec((nseq, CHUNK, D_GDN), lambda b, c: (b, c, 0)),
                  pl.BlockSpec((nseq, CHUNK, LANES), lambda b, c: (b, c, 0)),
                  pl.BlockSpec((nseq, CHUNK, LANES), lambda b, c: (b, c, 0)),
                  pl.BlockSpec((nseq, 1, CHUNK, LANES), lambda b, c: (b, c, 0, 0)),
                  pl.BlockSpec((1, GDN_DK), lambda b, c: (0, 0))],
        out_specs=[pl.BlockSpec((nseq, CHUNK, D_GDN), lambda b, c: (b, c, 0)),
                   pl.BlockSpec((nseq, GDN_HEADS, GDN_DK, GDN_DK), lambda b, c: (b, 0, 0, 0))],
        out_shape=[jax.ShapeDtypeStruct((bsz, t_pad, D_GDN), F32),
                   jax.ShapeDtypeStruct((bsz, GDN_HEADS, GDN_DK, GDN_DK), F32)],
        scratch_shapes=[pltpu.VMEM((nseq * GDN_HEADS, GDN_DK, GDN_DK), F32)],
        compiler_params=_cparams("arbitrary", "arbitrary"),
        name="gdn_chunk",
    )(qkv_act, qkv_act, qkv_act, z_act, feat, cs, cst, norm_w.reshape(1, GDN_DK))


def _ssd_finish(y, zs, nw):
    yg = y * zs
    gw = D_SSM // SSM_GROUPS
    parts = []
    for g in range(SSM_GROUPS):
        blk = yg[:, g * gw:(g + 1) * gw]
        parts.append(blk * lax.rsqrt(jnp.mean(blk * blk, -1, keepdims=True) + EPS))
    return jnp.concatenate(parts, axis=1) * nw


def _ssd_chunk_kernel(x_ref, b_ref, c_ref, zs_ref, featt_ref, cs_ref, cst_ref, d_ref, nw_ref,
                      y_ref, sfin_ref, s_ref, *, nseq):
    ch = pl.program_id(1)

    @pl.when(ch == 0)
    def _():
        s_ref[...] = jnp.zeros(s_ref.shape, F32)

    row = lax.broadcasted_iota(jnp.int32, (CHUNK, CHUNK), 0)
    col = lax.broadcasted_iota(jnp.int32, (CHUNK, CHUNK), 1)
    upper = col >= row
    top = row < SSM_HEAD_DIM
    heads_per_group = SSM_HEADS // SSM_GROUPS
    npair = SSM_HEADS // 2
    ecs, edl = [], []
    for j in range(nseq):
        cst = cst_ref[j, 0]
        ecs.append(jnp.exp(cst))
        edl.append(jnp.exp(cst[:, CHUNK - 1:CHUNK] - cst))
    grp_slices = [slice(g * SSM_STATE, (g + 1) * SSM_STATE) for g in range(SSM_GROUPS)]
    cbt = [[_dot_nt(b_ref[j, :, gs], c_ref[j, :, gs]) for gs in grp_slices] for j in range(nseq)]
    chains = [(j, p) for j in range(nseq) for p in range(npair)]

    def grp_of(p):
        return grp_slices[(2 * p) // heads_per_group]

    def rows2(tile, base, p):
        return jnp.where(top, tile[base + 2 * p:base + 2 * p + 1, :],
                         tile[base + 2 * p + 1:base + 2 * p + 2, :])

    def decay_t(j, h):
        ac_row = cst_ref[j, 0, FEAT_LA + h:FEAT_LA + h + 1, :]
        ac_col = cs_ref[j, :, FEAT_LA + h:FEAT_LA + h + 1]
        return jnp.exp(jnp.where(upper, ac_row - ac_col, NEG_BIG))

    xt = [x_ref[j, :, p * LANES:(p + 1) * LANES].T for j, p in chains]
    xdt = [x * rows2(featt_ref[j, 0], FEAT_DT, p) for x, (j, p) in zip(xt, chains)]
    st = [jnp.concatenate([cbt[j][(2 * p) // heads_per_group] * decay_t(j, 2 * p),
                           cbt[j][(2 * p) // heads_per_group] * decay_t(j, 2 * p + 1)], axis=0)
          for j, p in chains]
    lhs = [jnp.concatenate([jnp.where(top, x, 0.0), jnp.where(top, 0.0, x)], axis=1) for x in xdt]
    y_diag = [_dot(l, r) for l, r in zip(lhs, st)]
    s = [s_ref[i * LANES:(i + 1) * LANES, :] for i in range(len(chains))]
    y_off = [_dot_nt(z, c_ref[j, :, grp_of(p)]) * rows2(ecs[j], FEAT_LA, p)
             for z, (j, p) in zip(s, chains)]
    ds = [_dot(x * rows2(edl[j], FEAT_LA, p), b_ref[j, :, grp_of(p)])
          for x, (j, p) in zip(xdt, chains)]
    for i, (j, p) in enumerate(chains):
        last = rows2(ecs[j], FEAT_LA, p)[:, CHUNK - 1:CHUNK]
        s_ref[i * LANES:(i + 1) * LANES, :] = s[i] * last + ds[i]
    yt = [a + b + d_ref[p * LANES:(p + 1) * LANES, :] * x
          for a, b, x, (j, p) in zip(y_diag, y_off, xt, chains)]
    for j in range(nseq):
        y = jnp.concatenate([yt[j * npair + p].T for p in range(npair)], axis=1)
        y_ref[j] = _ssd_finish(y, zs_ref[j], nw_ref[...])

    @pl.when(ch == pl.num_programs(1) - 1)
    def _():
        rows = SSM_HEADS * SSM_HEAD_DIM
        for j in range(nseq):
            sfin_ref[j] = s_ref[j * rows:(j + 1) * rows, :]


SSD_SEQ_PER_STEP = 2


def _ssd_chunk(xbc_act, z_act, featt, cs, cst, d_rows, norm_w):
    bsz, t_pad, _ = xbc_act.shape
    nch = t_pad // CHUNK
    bc_w = SSM_GROUPS * SSM_STATE
    nseq = SSD_SEQ_PER_STEP if bsz % SSD_SEQ_PER_STEP == 0 else 1
    rows = SSM_HEADS * SSM_HEAD_DIM
    return pl.pallas_call(
        functools.partial(_ssd_chunk_kernel, nseq=nseq),
        grid=(bsz // nseq, nch),
        in_specs=[pl.BlockSpec((nseq, CHUNK, D_SSM), lambda b, c: (b, c, 0)),
                  pl.BlockSpec((nseq, CHUNK, bc_w), lambda b, c: (b, c, D_SSM // bc_w)),
                  pl.BlockSpec((nseq, CHUNK, bc_w), lambda b, c: (b, c, D_SSM // bc_w + 1)),
                  pl.BlockSpec((nseq, CHUNK, D_SSM), lambda b, c: (b, c, 1)),
                  pl.BlockSpec((nseq, 1, CHUNK, LANES), lambda b, c: (b, c, 0, 0)),
                  pl.BlockSpec((nseq, CHUNK, LANES), lambda b, c: (b, c, 0)),
                  pl.BlockSpec((nseq, 1, CHUNK, LANES), lambda b, c: (b, c, 0, 0)),
                  pl.BlockSpec((rows, LANES), lambda b, c: (0, 0)),
                  pl.BlockSpec((1, D_SSM), lambda b, c: (0, 0))],
        out_specs=[pl.BlockSpec((nseq, CHUNK, D_SSM), lambda b, c: (b, c, 0)),
                   pl.BlockSpec((nseq, rows, SSM_STATE), lambda b, c: (b, 0, 0))],
        out_shape=[jax.ShapeDtypeStruct((bsz, t_pad, D_SSM), F32),
                   jax.ShapeDtypeStruct((bsz, rows, SSM_STATE), F32)],
        scratch_shapes=[pltpu.VMEM((nseq * rows, SSM_STATE), F32)],
        compiler_params=_cparams("arbitrary", "arbitrary"),
        name="ssd_chunk",
    )(xbc_act, xbc_act, xbc_act, z_act, featt, cs, cst, d_rows, norm_w.reshape(1, D_SSM))


def _out_proj_kernel(mg_ref, ms_ref, x_ref, w_ref, nw_ref, xo_ref, ho_ref, *, seq_rows):
    acc = (jnp.dot(mg_ref[...].astype(BF16), w_ref[0:D_GDN, :], preferred_element_type=F32)
           + jnp.dot(ms_ref[...].astype(BF16), w_ref[D_GDN:, :], preferred_element_type=F32))
    xn = x_ref[...] + acc
    if seq_rows:
        tile = xn.shape[0]
        start = lax.rem(pl.program_id(0) * tile, seq_rows)
        r = start + lax.broadcasted_iota(jnp.int32, xn.shape, 0)
        pad = (r < FRONT_PAD) | ((r >= seq_rows) & (r < seq_rows + FRONT_PAD))
        xn = jnp.where(pad, 0.0, xn)
    xo_ref[...] = xn
    hn = xn * lax.rsqrt(jnp.mean(xn * xn, -1, keepdims=True) + EPS) * nw_ref[...]
    ho_ref[...] = hn.astype(ho_ref.dtype)


def _out_proj(mix_g, mix_s, x, w_out, next_norm_w, h_dtype, tile, seq_rows):
    rows = x.shape[0]
    row_spec = pl.BlockSpec((tile, D_MODEL), lambda i: (i, 0))
    return pl.pallas_call(
        functools.partial(_out_proj_kernel, seq_rows=seq_rows),
        grid=(rows // tile,),
        in_specs=[row_spec, row_spec, row_spec,
                  pl.BlockSpec((D_GDN + D_SSM, D_MODEL), lambda i: (0, 0)),
                  pl.BlockSpec((1, D_MODEL), lambda i: (0, 0))],
        out_specs=[row_spec, row_spec],
        out_shape=[jax.ShapeDtypeStruct((rows, D_MODEL), F32),
                   jax.ShapeDtypeStruct((rows, D_MODEL), h_dtype)],
        compiler_params=_cparams("arbitrary"),
        name="out_proj",
    )(mix_g, mix_s, x, w_out, next_norm_w.reshape(1, D_MODEL))


def _proj_conv_decode_kernel(h_ref, w_ref, buf_ref, cw_ref, cb_ref, out_ref, nbuf_ref):
    z = jnp.dot(h_ref[...], w_ref[...], preferred_element_type=F32)
    cw = cw_ref[...]
    acc = cb_ref[...] + cw[3:4] * z
    for tap in range(CONV_W - 1):
        acc = acc + cw[tap:tap + 1] * buf_ref[tap]
    out_ref[...] = _silu(acc)
    nbuf_ref[0] = buf_ref[1]
    nbuf_ref[1] = buf_ref[2]
    nbuf_ref[2] = z


def _proj_silu_decode_kernel(h_ref, w_ref, out_ref):
    out_ref[...] = _silu(jnp.dot(h_ref[...], w_ref[...], preferred_element_type=F32))


def _proj_conv_decode(h, w, buf, cw, cb):
    rows = h.shape[0]
    ncols = w.shape[1]
    return pl.pallas_call(
        _proj_conv_decode_kernel,
        grid=(ncols // COL_BLOCK,),
        in_specs=[pl.BlockSpec((rows, D_MODEL), lambda j: (0, 0)),
                  pl.BlockSpec((D_MODEL, COL_BLOCK), lambda j: (0, j)),
                  pl.BlockSpec((CONV_W - 1, rows, COL_BLOCK), lambda j: (0, 0, j)),
                  pl.BlockSpec((CONV_W, COL_BLOCK), lambda j: (0, j)),
                  pl.BlockSpec((1, COL_BLOCK), lambda j: (0, j))],
        out_specs=[pl.BlockSpec((rows, COL_BLOCK), lambda j: (0, j)),
                   pl.BlockSpec((CONV_W - 1, rows, COL_BLOCK), lambda j: (0, 0, j))],
        out_shape=[jax.ShapeDtypeStruct((rows, ncols), F32),
                   jax.ShapeDtypeStruct((CONV_W - 1, rows, ncols), F32)],
        compiler_params=_cparams("arbitrary"),
        name="proj_conv_decode",
    )(h, w, buf, cw, cb.reshape(1, ncols))


def _proj_silu_decode(h, w):
    rows = h.shape[0]
    ncols = w.shape[1]
    return pl.pallas_call(
        _proj_silu_decode_kernel,
        grid=(ncols // COL_BLOCK,),
        in_specs=[pl.BlockSpec((rows, D_MODEL), lambda j: (0, 0)),
                  pl.BlockSpec((D_MODEL, COL_BLOCK), lambda j: (0, j))],
        out_specs=pl.BlockSpec((rows, COL_BLOCK), lambda j: (0, j)),
        out_shape=jax.ShapeDtypeStruct((rows, ncols), F32),
        compiler_params=_cparams("arbitrary"),
        name="proj_silu_decode",
    )(h, w)


TOK_BLOCK = 8


def _gdn_step_kernel(qt_ref, kt_ref, v_ref, zg_ref, feat_ref, nw_ref, s_ref, *rest):
    o_ref, so_ref = rest[-2:]
    feat = feat_ref[...]
    vrows = v_ref[...]
    for h in range(GDN_HEADS):
        sl = slice(h * GDN_DK, (h + 1) * GDN_DK)
        qt = qt_ref[0, h]
        kt = kt_ref[0, h]
        qt = qt * (lax.rsqrt(jnp.sum(qt * qt, 0, keepdims=True) + EPS) * (GDN_DK ** -0.5))
        kt = kt * lax.rsqrt(jnp.sum(kt * kt, 0, keepdims=True) + EPS)
        for j in range(TOK_BLOCK):
            kc = kt[:, j:j + 1]
            qc = qt[:, j:j + 1]
            beta = feat[j:j + 1, FEAT_BETA + h:FEAT_BETA + h + 1]
            g = feat[j:j + 1, FEAT_G + h:FEAT_G + h + 1]
            s = s_ref[j, h] * jnp.exp(g)
            kv = jnp.sum(s * kc, 0, keepdims=True)
            d = (vrows[j:j + 1, sl] - kv) * beta
            s = s + kc * d
            so_ref[j, h] = s
            o_ref[j:j + 1, sl] = jnp.sum(s * qc, 0, keepdims=True)
    nw = nw_ref[...]
    for h in range(GDN_HEADS):
        sl = slice(h * GDN_DK, (h + 1) * GDN_DK)
        o = o_ref[:, sl]
        o_ref[:, sl] = o * lax.rsqrt(jnp.mean(o * o, -1, keepdims=True) + EPS) * nw * zg_ref[:, sl]


def _layer_state_call(kernel_fn, name, layer, states, acc, in_specs, out_specs, out_shapes, args):
    rows = states.shape[1]
    blk = (None, TOK_BLOCK) + states.shape[2:]
    zeros = (0,) * (states.ndim - 2)
    state_spec = pl.BlockSpec(blk, lambda i: (layer, i) + zeros)
    in_specs = in_specs + [state_spec]
    args = args + [states]
    aliases = {}
    if acc is not None:
        in_specs = in_specs + [pl.BlockSpec(memory_space=pl.ANY)]
        args = args + [acc]
        aliases = {len(args) - 1: len(out_shapes)}
    return pl.pallas_call(
        kernel_fn,
        grid=(rows // TOK_BLOCK,),
        in_specs=in_specs,
        out_specs=out_specs + [state_spec],
        out_shape=out_shapes + [jax.ShapeDtypeStruct(states.shape, F32)],
        input_output_aliases=aliases,
        compiler_params=_cparams("arbitrary"),
        name=name,
    )(*args)


def _gdn_step(layer, qt, kt, qkv_act, z_act, feat, norm_w, states, acc):
    rows = qkv_act.shape[0]
    return _layer_state_call(
        _gdn_step_kernel, "gdn_step", layer, states, acc,
        [pl.BlockSpec((1, GDN_HEADS, GDN_DK, TOK_BLOCK), lambda i: (i, 0, 0, 0)),
         pl.BlockSpec((1, GDN_HEADS, GDN_DK, TOK_BLOCK), lambda i: (i, 0, 0, 0)),
         pl.BlockSpec((TOK_BLOCK, D_GDN), lambda i: (i, 2)),
         pl.BlockSpec((TOK_BLOCK, D_GDN), lambda i: (i, 0)),
         pl.BlockSpec((TOK_BLOCK, LANES), lambda i: (i, 0)),
         pl.BlockSpec((1, GDN_DK), lambda i: (0, 0))],
        [pl.BlockSpec((TOK_BLOCK, D_GDN), lambda i: (i, 0))],
        [jax.ShapeDtypeStruct((rows, D_GDN), F32)],
        [qt, kt, qkv_act, z_act, feat, norm_w.reshape(1, GDN_DK)])


def _ssd_step_kernel(xt_ref, b_ref, c_ref, feat_ref, s_ref, *rest):
    yt_ref, so_ref = rest[-2:]
    feat = feat_ref[...]
    heads_per_group = SSM_HEADS // SSM_GROUPS
    ela = jnp.exp(feat)
    toks = range(TOK_BLOCK)
    for h in range(SSM_HEADS):
        grp = h // heads_per_group
        gl = slice(grp * SSM_STATE, (grp + 1) * SSM_STATE)
        xt = xt_ref[0, h]
        xdt = [xt[:, j:j + 1] * feat[j:j + 1, FEAT_DT + h:FEAT_DT + h + 1] for j in toks]
        xb = [jnp.broadcast_to(x, (SSM_HEAD_DIM, SSM_STATE)) for x in xdt]
        s = [s_ref[j, h] * ela[j:j + 1, FEAT_LA + h:FEAT_LA + h + 1] + xb[j] * b_ref[j:j + 1, gl]
             for j in toks]
        for j in toks:
            so_ref[j, h] = s[j]
        y = [jnp.sum(s[j] * c_ref[j:j + 1, gl], -1, keepdims=True) for j in toks]
        yt_ref[0, h] = jnp.concatenate(y, axis=1)


def _ssd_step(layer, xt, xbc_act, feat, states, acc):
    rows = xbc_act.shape[0]
    ngrp = rows // TOK_BLOCK
    bc_w = SSM_GROUPS * SSM_STATE
    return _layer_state_call(
        _ssd_step_kernel, "ssd_step", layer, states, acc,
        [pl.BlockSpec((1, SSM_HEADS, SSM_HEAD_DIM, TOK_BLOCK), lambda i: (i, 0, 0, 0)),
         pl.BlockSpec((TOK_BLOCK, bc_w), lambda i: (i, D_SSM // bc_w)),
         pl.BlockSpec((TOK_BLOCK, bc_w), lambda i: (i, D_SSM // bc_w + 1)),
         pl.BlockSpec((TOK_BLOCK, LANES), lambda i: (i, 0))],
        [pl.BlockSpec((1, SSM_HEADS, SSM_HEAD_DIM, TOK_BLOCK), lambda i: (i, 0, 0, 0))],
        [jax.ShapeDtypeStruct((ngrp, SSM_HEADS, SSM_HEAD_DIM, TOK_BLOCK), F32)],
        [xt, xbc_act, xbc_act, feat])


def _ssd_finish_decode_kernel(y_ref, x_ref, zs_ref, d_ref, nw_ref, out_ref):
    out_ref[...] = _ssd_finish(y_ref[...] + d_ref[...] * x_ref[...], zs_ref[...], nw_ref[...])


def _ssd_finish_decode(y, xbc_act, z_act, d_cols, norm_w):
    rows = y.shape[0]
    return pl.pallas_call(
        _ssd_finish_decode_kernel,
        grid=(1,),
        in_specs=[pl.BlockSpec((rows, D_SSM), lambda i: (0, 0)),
                  pl.BlockSpec((rows, D_SSM), lambda i: (0, 0)),
                  pl.BlockSpec((rows, D_SSM), lambda i: (0, 1)),
                  pl.BlockSpec((1, D_SSM), lambda i: (0, 0)),
                  pl.BlockSpec((1, D_SSM), lambda i: (0, 0))],
        out_specs=pl.BlockSpec((rows, D_SSM), lambda i: (0, 0)),
        out_shape=jax.ShapeDtypeStruct((rows, D_SSM), F32),
        compiler_params=_cparams("arbitrary"),
        name="ssd_finish_decode",
    )(y, xbc_act, z_act, d_cols, norm_w.reshape(1, D_SSM))


def _layer_weights(w_in, gdn_A_log, gdn_dt_bias, ssm_A_log, ssm_dt_bias, ssm_D):
    w_qkv = w_in[:, :OFF_Z_G].astype(BF16)
    w_xbc = w_in[:, OFF_XBC:OFF_Z_S].astype(BF16)
    w_z = jnp.concatenate([w_in[:, OFF_Z_G:OFF_B], w_in[:, OFF_Z_S:OFF_DT]], axis=1).astype(BF16)
    w_dt = w_in[:, OFF_DT:]
    w_small = jnp.concatenate(
        [w_in[:, OFF_B:OFF_XBC], w_dt, w_dt, jnp.zeros((D_MODEL, LANES - FEAT_END), F32)],
        axis=1).astype(BF16)
    zeros8 = jnp.zeros((GDN_HEADS,), F32)
    tail = jnp.zeros((LANES - FEAT_END,), F32)
    p_bias = jnp.concatenate([zeros8, gdn_dt_bias.astype(F32), ssm_dt_bias.astype(F32),
                              ssm_dt_bias.astype(F32), tail]).reshape(1, LANES)
    p_scale = jnp.concatenate([zeros8, -jnp.exp(gdn_A_log.astype(F32)),
                               jnp.ones((SSM_HEADS,), F32), -jnp.exp(ssm_A_log.astype(F32)),
                               tail]).reshape(1, LANES)
    d_cols = jnp.repeat(ssm_D.astype(F32), SSM_HEAD_DIM).reshape(1, D_SSM)
    return w_qkv, w_xbc, w_z, w_small, p_bias, p_scale, d_cols


def kernel(x_prompt, x_sample, state_gdn, state_gdn_conv, state_ssm, state_ssm_conv, meta_tokens,
           norm_w, w_in, gdn_conv_w, gdn_A_log, gdn_dt_bias, gdn_norm_w, ssm_conv_w, ssm_conv_b,
           ssm_A_log, ssm_dt_bias, ssm_D, ssm_norm_w, w_out, final_norm_w):
    bp, seq, _ = x_prompt.shape
    bd = x_sample.shape[0]
    t_pad = FRONT_PAD + N_META + seq
    meta = jnp.broadcast_to(meta_tokens.astype(F32)[None], (bp, N_META, D_MODEL))
    xp = jnp.concatenate([jnp.zeros((bp, FRONT_PAD, D_MODEL), F32), meta, x_prompt.astype(F32)],
                         axis=1).reshape(bp * t_pad, D_MODEL)
    xs = x_sample.astype(F32).reshape(bd, D_MODEL)
    hp = _rmsnorm(xp, norm_w[0], ROW_TILE)
    hs = _rmsnorm(xs, norm_w[0], bd)
    zero_bias = jnp.zeros((GDN_CONV_DIM,), F32)
    ngrp = bd // TOK_BLOCK
    p_g, p_gc, p_s, p_sc = [], [], [], []
    d_g, d_s = None, None
    d_gc, d_sc = [], []
    for l in range(DEPTH):
        w_qkv, w_xbc, w_z, w_small, p_bias, p_scale, d_cols = _layer_weights(
            w_in[l], gdn_A_log[l], gdn_dt_bias[l], ssm_A_log[l], ssm_dt_bias[l], ssm_D[l])
        w_o = w_out[l].astype(BF16)
        last = l == DEPTH - 1
        nxt_w = final_norm_w if last else norm_w[l + 1]
        h_dtype = F32 if last else BF16
        gcw = gdn_conv_w[l].astype(F32)
        scw = ssm_conv_w[l].astype(F32)
        scb = ssm_conv_b[l].astype(F32)

        h3 = hp.reshape(bp, t_pad, D_MODEL)
        qkv_act, qkv_tail = _proj_conv(h3, w_qkv, gcw, None)
        xbc_act, xbc_tail = _proj_conv(h3, w_xbc, scw, scb)
        z_act = _proj_silu(h3, w_z)
        feat, featt, cs, cst = _feat_prompt(h3, w_small, p_bias, p_scale)
        d_rows = jnp.broadcast_to(d_cols.reshape(D_SSM, 1), (D_SSM, LANES))
        mix_g, sg = _gdn_chunk(qkv_act, z_act, feat, cs, cst, gdn_norm_w[l].astype(F32))
        mix_s, ss = _ssd_chunk(xbc_act, z_act, featt, cs, cst, d_rows, ssm_norm_w[l].astype(F32))
        xp, hp = _out_proj(mix_g.reshape(bp * t_pad, D_GDN), mix_s.reshape(bp * t_pad, D_SSM), xp,
                           w_o, nxt_w.astype(F32), h_dtype, ROW_TILE, t_pad)
        p_g.append(sg)
        p_gc.append(qkv_tail[:, SUBLANES - (CONV_W - 1):, :])
        p_s.append(ss.reshape(bp, SSM_HEADS, SSM_HEAD_DIM, SSM_STATE))
        p_sc.append(xbc_tail[:, SUBLANES - (CONV_W - 1):, :])

        gbuf = jnp.transpose(state_gdn_conv[l].astype(F32), (1, 0, 2))
        sbuf = jnp.transpose(state_ssm_conv[l].astype(F32), (1, 0, 2))
        qkv_d, gbuf_n = _proj_conv_decode(hs, w_qkv, gbuf, gcw, zero_bias)
        xbc_d, sbuf_n = _proj_conv_decode(hs, w_xbc, sbuf, scw, scb)
        z_d = _proj_silu_decode(hs, w_z)
        feat_d = _feat_decode(hs, w_small, p_bias, p_scale)

        def head_major(a, heads, dim):
            return jnp.transpose(a.reshape(ngrp, TOK_BLOCK, heads, dim), (0, 2, 3, 1))

        qt = head_major(qkv_d[:, :GDN_QK], GDN_HEADS, GDN_DK)
        kt = head_major(qkv_d[:, GDN_QK:2 * GDN_QK], GDN_HEADS, GDN_DK)
        mix_gd, d_g = _gdn_step(l, qt, kt, qkv_d, z_d, feat_d, gdn_norm_w[l].astype(F32),
                                state_gdn.astype(F32), d_g)
        xt = head_major(xbc_d[:, :D_SSM], SSM_HEADS, SSM_HEAD_DIM)
        yt, d_s = _ssd_step(l, xt, xbc_d, feat_d, state_ssm.astype(F32), d_s)
        y_d = jnp.transpose(yt, (0, 3, 1, 2)).reshape(bd, D_SSM)
        mix_sd = _ssd_finish_decode(y_d, xbc_d, z_d, d_cols, ssm_norm_w[l].astype(F32))
        xs, hs = _out_proj(mix_gd, mix_sd, xs, w_o, nxt_w.astype(F32), h_dtype, bd, 0)
        d_gc.append(jnp.transpose(gbuf_n, (1, 0, 2)))
        d_sc.append(jnp.transpose(sbuf_n, (1, 0, 2)))

    y_prompt = hp.reshape(bp, t_pad, D_MODEL)[:, FRONT_PAD + N_META:]
    y_sample = hs.reshape(bd, 1, D_MODEL)
    return (y_prompt, y_sample,
            jnp.stack(p_g), jnp.stack(p_gc), jnp.stack(p_s), jnp.stack(p_sc),
            d_g, jnp.stack(d_gc), d_s, jnp.stack(d_sc))
```

```python
import functools

import jax
import jax.numpy as jnp
from jax import lax
from jax.experimental import pallas as pl
from jax.experimental.pallas import tpu as pltpu

F32 = jnp.float32
BF16 = jnp.bfloat16

D_MODEL = 1024
DEPTH = 4
N_META = 16
GDN_HEADS = 8
GDN_DK = 128
GDN_QK = 1024
D_GDN = 1024
GDN_CONV_DIM = 3072
D_SSM = 1024
SSM_HEADS = 16
SSM_HEAD_DIM = 64
SSM_GROUPS = 2
SSM_STATE = 128
SSM_CONV_DIM = 1536
CONV_W = 4
EPS = 1e-6
OFF_Z_G = GDN_CONV_DIM
OFF_B = OFF_Z_G + D_GDN
OFF_A = OFF_B + GDN_HEADS
OFF_XBC = OFF_A + GDN_HEADS
OFF_Z_S = OFF_XBC + SSM_CONV_DIM
OFF_DT = OFF_Z_S + D_SSM

CHUNK = 128
HALF = CHUNK // 2
FRONT_PAD = 112
SUBLANES = 8
LANES = 128
COL_BLOCK = 512
ROW_TILE = 512
NEG_BIG = -1e30
VMEM_LIMIT = 56 * 1024 * 1024

FEAT_BETA = 0
FEAT_G = 8
FEAT_DT = 16
FEAT_LA = 32
FEAT_END = 48


def _cparams(*sem, flags=None):
    return pltpu.CompilerParams(dimension_semantics=sem, vmem_limit_bytes=VMEM_LIMIT, flags=flags)


def _dot(a, b):
    return jnp.dot(a.astype(BF16), b.astype(BF16), preferred_element_type=F32)


def _dot_nt(a, b):
    return lax.dot_general(a.astype(BF16), b.astype(BF16), (((1,), (1,)), ((), ())),
                           preferred_element_type=F32)


def _dot_tn(a, b):
    return lax.dot_general(a.astype(BF16), b.astype(BF16), (((0,), (0,)), ((), ())),
                           preferred_element_type=F32)


def _silu_of_twice(h):
    return h * jnp.tanh(h) + h


def _softplus(x):
    return jnp.maximum(x, 0.0) + jnp.log1p(jnp.exp(-jnp.abs(x)))


def _rmsnorm_kernel(x_ref, w_ref, h_ref):
    x = x_ref[...]
    y = x * lax.rsqrt(jnp.mean(x * x, -1, keepdims=True) + EPS) * w_ref[...]
    h_ref[...] = y.astype(h_ref.dtype)


def _rmsnorm(x, w, tile):
    rows = x.shape[0]
    return pl.pallas_call(
        _rmsnorm_kernel,
        grid=(rows // tile,),
        in_specs=[pl.BlockSpec((tile, D_MODEL), lambda i: (i, 0)),
                  pl.BlockSpec((1, D_MODEL), lambda i: (0, 0))],
        out_specs=pl.BlockSpec((tile, D_MODEL), lambda i: (i, 0)),
        out_shape=jax.ShapeDtypeStruct((rows, D_MODEL), BF16),
        compiler_params=_cparams("arbitrary"),
        name="rmsnorm",
    )(x, w.reshape(1, D_MODEL))


PROJ_ROW_BLOCKS = 8
PROJ_LOOKAHEAD = 1


def _proj_conv_kernel(h_ref, w_ref, cw_ref, cb_ref, out_ref, tail_ref, *scratch, t_pad, bias):
    z_refs, y_ref = scratch[:-1], scratch[-1]
    ncol = out_ref.shape[2]
    ntile = ncol // LANES
    rb = t_pad // PROJ_ROW_BLOCKS
    stride = rb // SUBLANES
    cw = cw_ref[...]
    cb = cb_ref[...]
    first_row = lax.broadcasted_iota(jnp.int32, (SUBLANES, LANES), 0) == 0
    zero = jnp.zeros((SUBLANES, LANES), F32)
    dyn_zero = jnp.minimum(pl.program_id(0), 0)

    def matmul(r):
        z = jnp.dot(h_ref[0, r * rb:(r + 1) * rb, :], w_ref[...], preferred_element_type=F32)
        for c in range(ntile):
            z_refs[r][c] = z[:, c * LANES:(c + 1) * LANES]

    def shift_in(x, carry):
        return jnp.where(first_row, carry, pltpu.roll(x, 1, 0))

    def conv(r):
        for c in range(ntile):
            taps = [cw[j:j + 1, c * LANES:(c + 1) * LANES] for j in range(CONV_W)]
            carries = [zero] * (CONV_W - 1)
            if r > 0:
                hist = z_refs[r - 1][c, rb - SUBLANES:rb, :]
                part = taps[0] * hist
                for tap in range(1, CONV_W):
                    carries[tap - 1] = pltpu.roll(part, 1, 0)
                    part = taps[tap] * hist + carries[tap - 1]
            v = [z_refs[r][c, pl.ds(dyn_zero + i, SUBLANES, stride=stride), :] for i in range(stride)]
            acc = [taps[0] * x for x in v]
            for tap in range(1, CONV_W):
                wrapped = shift_in(acc[stride - 1], carries[tap - 1])
                acc = [taps[tap] * x + (wrapped if i == 0 else acc[i - 1]) for i, x in enumerate(v)]
            for i, y in enumerate(acc):
                if bias:
                    y = y + cb[:, c * LANES:(c + 1) * LANES]
                y_ref[c, pl.ds(i, SUBLANES, stride=stride), :] = _silu_of_twice(y)
        for c in range(ntile):
            out_ref[0, r * rb:(r + 1) * rb, c * LANES:(c + 1) * LANES] = y_ref[c]

    for r in range(PROJ_ROW_BLOCKS + PROJ_LOOKAHEAD):
        if r < PROJ_ROW_BLOCKS:
            matmul(r)
        if r >= PROJ_LOOKAHEAD:
            conv(r - PROJ_LOOKAHEAD)
    for c in range(ntile):
        tail_ref[0, :, c * LANES:(c + 1) * LANES] = z_refs[PROJ_ROW_BLOCKS - 1][c, rb - SUBLANES:rb, :]


def _proj_silu_kernel(h_ref, w_ref, out_ref):
    out_ref[0] = _silu_of_twice(jnp.dot(h_ref[0], w_ref[...], preferred_element_type=F32))


def _proj_conv(h3, w, cw, cb):
    bsz, t_pad, _ = h3.shape
    ncols = w.shape[1]
    bias = cb is not None
    if not bias:
        cb = jnp.zeros((ncols,), F32)
    return pl.pallas_call(
        functools.partial(_proj_conv_kernel, t_pad=t_pad, bias=bias),
        grid=(bsz, ncols // COL_BLOCK),
        in_specs=[pl.BlockSpec((1, t_pad, D_MODEL), lambda b, j: (b, 0, 0)),
                  pl.BlockSpec((D_MODEL, COL_BLOCK), lambda b, j: (0, j)),
                  pl.BlockSpec((CONV_W, COL_BLOCK), lambda b, j: (0, j)),
                  pl.BlockSpec((1, COL_BLOCK), lambda b, j: (0, j))],
        out_specs=[pl.BlockSpec((1, t_pad, COL_BLOCK), lambda b, j: (b, 0, j)),
                   pl.BlockSpec((1, SUBLANES, COL_BLOCK), lambda b, j: (b, 0, j))],
        out_shape=[jax.ShapeDtypeStruct((bsz, t_pad, ncols), F32),
                   jax.ShapeDtypeStruct((bsz, SUBLANES, ncols), F32)],
        scratch_shapes=[pltpu.VMEM((COL_BLOCK // LANES, t_pad // PROJ_ROW_BLOCKS, LANES), F32)]
        * (PROJ_ROW_BLOCKS + 1),
        compiler_params=_cparams("arbitrary", "arbitrary"),
        name="proj_conv",
    )(h3, w, cw, cb.reshape(1, ncols))


def _proj_silu(h3, w):
    bsz, t_pad, _ = h3.shape
    ncols = w.shape[1]
    return pl.pallas_call(
        _proj_silu_kernel,
        grid=(bsz, ncols // COL_BLOCK),
        in_specs=[pl.BlockSpec((1, t_pad, D_MODEL), lambda b, j: (b, 0, 0)),
                  pl.BlockSpec((D_MODEL, COL_BLOCK), lambda b, j: (0, j))],
        out_specs=pl.BlockSpec((1, t_pad, COL_BLOCK), lambda b, j: (b, 0, j)),
        out_shape=jax.ShapeDtypeStruct((bsz, t_pad, ncols), F32),
        compiler_params=_cparams("arbitrary", "arbitrary"),
        name="proj_silu",
    )(h3, w)


def _features(z, p_bias, p_scale):
    lane = lax.broadcasted_iota(jnp.int32, z.shape, 1)
    sp = p_scale * _softplus(z + p_bias)
    return jnp.where(lane < FEAT_G, jax.nn.sigmoid(z), jnp.where(lane < FEAT_END, sp, 0.0))


def _split3(x):
    x1 = x.astype(BF16)
    r1 = x - x1.astype(F32)
    x2 = r1.astype(BF16)
    x3 = (r1 - x2.astype(F32)).astype(BF16)
    return x1, x2, x3


def _feat_prompt_kernel(h_ref, w_ref, pb_ref, ps_ref, feat_ref, featt_ref, cs_ref, cst_ref, *,
                        t_pad):
    z = jnp.dot(h_ref[0], w_ref[...], preferred_element_type=F32)
    feat = _features(z, pb_ref[...], ps_ref[...])
    row = lax.broadcasted_iota(jnp.int32, feat.shape, 0)
    feat = jnp.where(row < FRONT_PAD, 0.0, feat)
    feat_ref[0] = feat
    r = lax.broadcasted_iota(jnp.int32, (CHUNK, CHUNK), 0)
    c = lax.broadcasted_iota(jnp.int32, (CHUNK, CHUNK), 1)
    tri = (r >= c).astype(BF16)
    for ch in range(t_pad // CHUNK):
        tile = feat[ch * CHUNK:(ch + 1) * CHUNK]
        featt_ref[0, ch] = tile.T
        x1, x2, x3 = _split3(tile)
        cs = (jnp.dot(tri, x1, preferred_element_type=F32)
              + jnp.dot(tri, x2, preferred_element_type=F32)
              + jnp.dot(tri, x3, preferred_element_type=F32))
        cs_ref[0, ch * CHUNK:(ch + 1) * CHUNK, :] = cs
        cst_ref[0, ch] = cs.T


def _feat_prompt(h3, w_small, p_bias, p_scale):
    bsz, t_pad, _ = h3.shape
    nch = t_pad // CHUNK
    return pl.pallas_call(
        functools.partial(_feat_prompt_kernel, t_pad=t_pad),
        grid=(bsz,),
        in_specs=[pl.BlockSpec((1, t_pad, D_MODEL), lambda b: (b, 0, 0)),
                  pl.BlockSpec((D_MODEL, LANES), lambda b: (0, 0)),
                  pl.BlockSpec((1, LANES), lambda b: (0, 0)),
                  pl.BlockSpec((1, LANES), lambda b: (0, 0))],
        out_specs=[pl.BlockSpec((1, t_pad, LANES), lambda b: (b, 0, 0)),
                   pl.BlockSpec((1, nch, CHUNK, LANES), lambda b: (b, 0, 0, 0)),
                   pl.BlockSpec((1, t_pad, LANES), lambda b: (b, 0, 0)),
                   pl.BlockSpec((1, nch, CHUNK, LANES), lambda b: (b, 0, 0, 0))],
        out_shape=[jax.ShapeDtypeStruct((bsz, t_pad, LANES), F32),
                   jax.ShapeDtypeStruct((bsz, nch, CHUNK, LANES), F32),
                   jax.ShapeDtypeStruct((bsz, t_pad, LANES), F32),
                   jax.ShapeDtypeStruct((bsz, nch, CHUNK, LANES), F32)],
        compiler_params=_cparams("arbitrary"),
        name="feat_prompt",
    )(h3, w_small, p_bias, p_scale)


def _feat_decode_kernel(h_ref, w_ref, pb_ref, ps_ref, feat_ref):
    z = jnp.dot(h_ref[...], w_ref[...], preferred_element_type=F32)
    feat_ref[...] = _features(z, pb_ref[...], ps_ref[...])


def _feat_decode(h, w_small, p_bias, p_scale):
    rows = h.shape[0]
    return pl.pallas_call(
        _feat_decode_kernel,
        out_shape=jax.ShapeDtypeStruct((rows, LANES), F32),
        compiler_params=pltpu.CompilerParams(vmem_limit_bytes=VMEM_LIMIT),
        name="feat_decode",
    )(h, w_small, p_bias, p_scale)


def _gdn_chunk_kernel(q_ref, k_ref, v_ref, zg_ref, feat_ref, cs_ref, cst_ref, nw_ref,
                      o_ref, sfin_ref, s_ref, *, nseq):
    ch = pl.program_id(1)

    @pl.when(ch == 0)
    def _():
        s_ref[...] = jnp.zeros(s_ref.shape, F32)

    row = lax.broadcasted_iota(jnp.int32, (CHUNK, CHUNK), 0)
    col = lax.broadcasted_iota(jnp.int32, (CHUNK, CHUNK), 1)
    incl = row >= col
    strict = row > col
    row_c = lax.broadcasted_iota(jnp.int32, (HALF, CHUNK), 0)
    col_c = lax.broadcasted_iota(jnp.int32, (HALF, CHUNK), 1)
    lo_c = col_c < HALF
    eye_c = (jnp.bitwise_and(col_c, HALF - 1) == row_c).astype(F32)
    zero_c = jnp.zeros((HALF, CHUNK), BF16)

    def bdiag(x):
        return jnp.concatenate([jnp.where(lo_c, x, zero_c), jnp.where(lo_c, zero_c, x)], axis=0)

    nw = nw_ref[...]
    chains = [(j, h) for j in range(nseq) for h in range(GDN_HEADS)]

    def lanes(h):
        return slice(h * GDN_DK, (h + 1) * GDN_DK)

    def unit(x, scale):
        return x * (lax.rsqrt(jnp.sum(x * x, -1, keepdims=True) + EPS) * scale)

    q = [unit(q_ref[j, :, lanes(h)], GDN_DK ** -0.5) for j, h in chains]
    k = [unit(k_ref[j, :, lanes(h)], 1.0) for j, h in chains]
    beta = [feat_ref[j, :, FEAT_BETA + h:FEAT_BETA + h + 1] for j, h in chains]
    gc = [cs_ref[j, :, FEAT_G + h:FEAT_G + h + 1] for j, h in chains]
    gr = [cst_ref[j, 0, FEAT_G + h:FEAT_G + h + 1, :] for j, h in chains]
    decay = [jnp.exp(jnp.where(incl, c - r, NEG_BIG)) for c, r in zip(gc, gr)]
    kb = [x * b for x, b in zip(k, beta)]
    kbf = [x.astype(BF16) for x in k]
    a = [jnp.where(strict, _dot_nt(x, y) * d, 0.0) for x, y, d in zip(kb, kbf, decay)]
    p = [jnp.where(lo_c, -x[:HALF], -x[HALF:]) for x in a]
    inv = [eye_c + x for x in p]
    r_p = [bdiag(x.astype(BF16)) for x in p]
    for _ in range(5):
        p = [_dot(x, r) for x, r in zip(p, r_p)]
        r_p = [bdiag(x.astype(BF16)) for x in p]
        inv = [x + _dot(x, r) for x, r in zip(inv, r_p)]
    invb = [x.astype(BF16) for x in inv]
    r_a21 = [jnp.concatenate([zero_c, jnp.where(lo_c, x[HALF:].astype(BF16), zero_c)], axis=0)
             for x in a]
    fold = [_dot(x, r) for x, r in zip(invb, r_a21)]
    r_i11 = [jnp.concatenate([jnp.where(lo_c, x, zero_c), zero_c], axis=0) for x in invb]
    fold = [_dot(x, r) for x, r in zip(fold, r_i11)]
    t_inv = [jnp.concatenate([jnp.where(lo_c, x, 0.0), jnp.where(lo_c, -y, x)], axis=0)
             for x, y in zip(inv, fold)]
    eg = [jnp.exp(c) for c in gc]
    rhs = [jnp.concatenate([v_ref[j, :, lanes(h)] * b, x * e], axis=1)
           for (j, h), b, x, e in zip(chains, beta, kb, eg)]
    uw = [_dot(t, r) for t, r in zip(t_inv, rhs)]
    attn = [_dot_nt(x, y) * d for x, y, d in zip(q, kbf, decay)]
    g_last = [c[CHUNK - 1:CHUNK, :] for c in gc]
    k_dec = [x * jnp.exp(l - c) for x, l, c in zip(k, g_last, gc)]
    s = [s_ref[i] for i in range(len(chains))]
    ws_qs = [_dot(jnp.concatenate([x[:, GDN_DK:], y * e], axis=0), z)
             for x, y, e, z in zip(uw, q, eg, s)]
    v_new = [(x[:, :GDN_DK] - y[:CHUNK]).astype(BF16) for x, y in zip(uw, ws_qs)]
    o = [y[CHUNK:] + _dot(x, z) for y, x, z in zip(ws_qs, attn, v_new)]
    for i, (j, h) in enumerate(chains):
        s_ref[i] = s[i] * jnp.exp(g_last[i]) + _dot_tn(k_dec[i], v_new[i])
        on = o[i] * lax.rsqrt(jnp.mean(o[i] * o[i], -1, keepdims=True) + EPS)
        o_ref[j, :, lanes(h)] = on * nw * zg_ref[j, :, lanes(h)]

    @pl.when(ch == pl.num_programs(1) - 1)
    def _():
        for j in range(nseq):
            sfin_ref[j] = s_ref[j * GDN_HEADS:(j + 1) * GDN_HEADS]


GDN_SEQ_PER_STEP = 2


def _gdn_chunk(qkv_act, z_act, feat, cs, cst, norm_w):
    bsz, t_pad, _ = qkv_act.shape
    nch = t_pad // CHUNK
    nseq = GDN_SEQ_PER_STEP if bsz % GDN_SEQ_PER_STEP == 0 else 1

    def qkv_spec(part):
        return pl.BlockSpec((nseq, CHUNK, GDN_QK), lambda b, c: (b, c, part))

    return pl.pallas_call(
        functools.partial(_gdn_chunk_kernel, nseq=nseq),
        grid=(bsz // nseq, nch),
        in_specs=[qkv_spec(0), qkv_spec(1), qkv_spec(2),
                  pl.BlockSpec((nseq, CHUNK, D_GDN), lambda b, c: (b, c, 0)),
                  pl.BlockSpec((nseq, CHUNK, LANES), lambda b, c: (b, c, 0)),
                  pl.BlockSpec((nseq, CHUNK, LANES), lambda b, c: (b, c, 0)),
                  pl.BlockSpec((nseq, 1, CHUNK, LANES), lambda b, c: (b, c, 0, 0)),
                  pl.BlockSpec((1, GDN_DK), lambda b, c: (0, 0))],
        out_specs=[pl.BlockSpec((nseq, CHUNK, D_GDN), lambda b, c: (b, c, 0)),
                   pl.BlockSpec((nseq, GDN_HEADS, GDN_DK, GDN_DK), lambda b, c: (b, 0, 0, 0))],
        out_shape=[jax.ShapeDtypeStruct((bsz, t_pad, D_GDN), F32),
                   jax.ShapeDtypeStruct((bsz, GDN_HEADS, GDN_DK, GDN_DK), F32)],
        scratch_shapes=[pltpu.VMEM((nseq * GDN_HEADS, GDN_DK, GDN_DK), F32)],
        compiler_params=_cparams("arbitrary", "arbitrary"),
        name="gdn_chunk",
    )(qkv_act, qkv_act, qkv_act, z_act, feat, cs, cst, norm_w.reshape(1, GDN_DK))


def _ssd_finish(y, zs, nw):
    yg = y * zs
    gw = D_SSM // SSM_GROUPS
    parts = []
    for g in range(SSM_GROUPS):
        blk = yg[:, g * gw:(g + 1) * gw]
        parts.append(blk * lax.rsqrt(jnp.mean(blk * blk, -1, keepdims=True) + EPS))
    return jnp.concatenate(parts, axis=1) * nw


def _ssd_chunk_kernel(x_ref, b_ref, c_ref, zs_ref, featt_ref, cs_ref, cst_ref, d_ref, nw_ref,
                      y_ref, sfin_ref, s_ref, *, nseq):
    ch = pl.program_id(1)

    @pl.when(ch == 0)
    def _():
        s_ref[...] = jnp.zeros(s_ref.shape, F32)

    row = lax.broadcasted_iota(jnp.int32, (CHUNK, CHUNK), 0)
    col = lax.broadcasted_iota(jnp.int32, (CHUNK, CHUNK), 1)
    upper = col >= row
    top = row < SSM_HEAD_DIM
    heads_per_group = SSM_HEADS // SSM_GROUPS
    npair = SSM_HEADS // 2
    ecs, edl = [], []
    for j in range(nseq):
        cst = cst_ref[j, 0]
        ecs.append(jnp.exp(cst))
        edl.append(jnp.exp(cst[:, CHUNK - 1:CHUNK] - cst))
    grp_slices = [slice(g * SSM_STATE, (g + 1) * SSM_STATE) for g in range(SSM_GROUPS)]
    cbt = [[_dot_nt(b_ref[j, :, gs], c_ref[j, :, gs]) for gs in grp_slices] for j in range(nseq)]
    chains = [(j, p) for j in range(nseq) for p in range(npair)]

    def grp_of(p):
        return grp_slices[(2 * p) // heads_per_group]

    def rows2(tile, base, p):
        return jnp.where(top, tile[base + 2 * p:base + 2 * p + 1, :],
                         tile[base + 2 * p + 1:base + 2 * p + 2, :])

    def decay_t(j, h):
        ac_row = cst_ref[j, 0, FEAT_LA + h:FEAT_LA + h + 1, :]
        ac_col = cs_ref[j, :, FEAT_LA + h:FEAT_LA + h + 1]
        return jnp.exp(jnp.where(upper, ac_row - ac_col, NEG_BIG))

    xt = [x_ref[j, :, p * LANES:(p + 1) * LANES].T for j, p in chains]
    xdt = [x * rows2(featt_ref[j, 0], FEAT_DT, p) for x, (j, p) in zip(xt, chains)]
    st = [jnp.concatenate([cbt[j][(2 * p) // heads_per_group] * decay_t(j, 2 * p),
                           cbt[j][(2 * p) // heads_per_group] * decay_t(j, 2 * p + 1)], axis=0)
          for j, p in chains]
    lhs = [jnp.concatenate([jnp.where(top, x, 0.0), jnp.where(top, 0.0, x)], axis=1) for x in xdt]
    y_diag = [_dot(l, r) for l, r in zip(lhs, st)]
    s = [s_ref[i * LANES:(i + 1) * LANES, :] for i in range(len(chains))]
    y_off = [_dot_nt(z, c_ref[j, :, grp_of(p)]) * rows2(ecs[j], FEAT_LA, p)
             for z, (j, p) in zip(s, chains)]
    ds = [_dot(x * rows2(edl[j], FEAT_LA, p), b_ref[j, :, grp_of(p)])
          for x, (j, p) in zip(xdt, chains)]
    for i, (j, p) in enumerate(chains):
        last = rows2(ecs[j], FEAT_LA, p)[:, CHUNK - 1:CHUNK]
        s_ref[i * LANES:(i + 1) * LANES, :] = s[i] * last + ds[i]
    yt = [a + b + d_ref[p * LANES:(p + 1) * LANES, :] * x
          for a, b, x, (j, p) in zip(y_diag, y_off, xt, chains)]
    for j in range(nseq):
        y = jnp.concatenate([yt[j * npair + p].T for p in range(npair)], axis=1)
        y_ref[j] = _ssd_finish(y, zs_ref[j], nw_ref[...])

    @pl.when(ch == pl.num_programs(1) - 1)
    def _():
        rows = SSM_HEADS * SSM_HEAD_DIM
        for j in range(nseq):
            sfin_ref[j] = s_ref[j * rows:(j + 1) * rows, :]


SSD_SEQ_PER_STEP = 2


def _ssd_chunk(xbc_act, z_act, featt, cs, cst, d_rows, norm_w):
    bsz, t_pad, _ = xbc_act.shape
    nch = t_pad // CHUNK
    bc_w = SSM_GROUPS * SSM_STATE
    nseq = SSD_SEQ_PER_STEP if bsz % SSD_SEQ_PER_STEP == 0 else 1
    rows = SSM_HEADS * SSM_HEAD_DIM
    return pl.pallas_call(
        functools.partial(_ssd_chunk_kernel, nseq=nseq),
        grid=(bsz // nseq, nch),
        in_specs=[pl.BlockSpec((nseq, CHUNK, D_SSM), lambda b, c: (b, c, 0)),
                  pl.BlockSpec((nseq, CHUNK, bc_w), lambda b, c: (b, c, D_SSM // bc_w)),
                  pl.BlockSpec((nseq, CHUNK, bc_w), lambda b, c: (b, c, D_SSM // bc_w + 1)),
                  pl.BlockSpec((nseq, CHUNK, D_SSM), lambda b, c: (b, c, 1)),
                  pl.BlockSpec((nseq, 1, CHUNK, LANES), lambda b, c: (b, c, 0, 0)),
                  pl.BlockSpec((nseq, CHUNK, LANES), lambda b, c: (b, c, 0)),
                  pl.BlockSpec((nseq, 1, CHUNK, LANES), lambda b, c: (b, c, 0, 0)),
                  pl.BlockSpec((rows, LANES), lambda b, c: (0, 0)),
                  pl.BlockSpec((1, D_SSM), lambda b, c: (0, 0))],
        out_specs=[pl.BlockSpec((nseq, CHUNK, D_SSM), lambda b, c: (b, c, 0)),
                   pl.BlockSpec((nseq, rows, SSM_STATE), lambda b, c: (b, 0, 0))],
        out_shape=[jax.ShapeDtypeStruct((bsz, t_pad, D_SSM), F32),
                   jax.ShapeDtypeStruct((bsz, rows, SSM_STATE), F32)],
        scratch_shapes=[pltpu.VMEM((nseq * rows, SSM_STATE), F32)],
        compiler_params=_cparams("arbitrary", "arbitrary"),
        name="ssd_chunk",
    )(xbc_act, xbc_act, xbc_act, z_act, featt, cs, cst, d_rows, norm_w.reshape(1, D_SSM))


def _out_proj_kernel(mg_ref, ms_ref, x_ref, w_ref, nw_ref, xo_ref, ho_ref, *, seq_rows):
    acc = (jnp.dot(mg_ref[...].astype(BF16), w_ref[0:D_GDN, :], preferred_element_type=F32)
           + jnp.dot(ms_ref[...].astype(BF16), w_ref[D_GDN:, :], preferred_element_type=F32))
    xn = x_ref[...] + acc
    if seq_rows:
        tile = xn.shape[0]
        start = lax.rem(pl.program_id(0) * tile, seq_rows)
        r = start + lax.broadcasted_iota(jnp.int32, xn.shape, 0)
        pad = (r < FRONT_PAD) | ((r >= seq_rows) & (r < seq_rows + FRONT_PAD))
        xn = jnp.where(pad, 0.0, xn)
    xo_ref[...] = xn
    hn = xn * lax.rsqrt(jnp.mean(xn * xn, -1, keepdims=True) + EPS) * nw_ref[...]
    ho_ref[...] = hn.astype(ho_ref.dtype)


def _out_proj(mix_g, mix_s, x, w_out, next_norm_w, h_dtype, tile, seq_rows):
    rows = x.shape[0]
    row_spec = pl.BlockSpec((tile, D_MODEL), lambda i: (i, 0))
    return pl.pallas_call(
        functools.partial(_out_proj_kernel, seq_rows=seq_rows),
        grid=(rows // tile,),
        in_specs=[row_spec, row_spec, row_spec,
                  pl.BlockSpec((D_GDN + D_SSM, D_MODEL), lambda i: (0, 0)),
                  pl.BlockSpec((1, D_MODEL), lambda i: (0, 0))],
        out_specs=[row_spec, row_spec],
        out_shape=[jax.ShapeDtypeStruct((rows, D_MODEL), F32),
                   jax.ShapeDtypeStruct((rows, D_MODEL), h_dtype)],
        compiler_params=_cparams("arbitrary"),
        name="out_proj",
    )(mix_g, mix_s, x, w_out, next_norm_w.reshape(1, D_MODEL))


def _proj_conv_decode_kernel(h_ref, w_ref, buf_ref, cw_ref, cb_ref, out_ref, nbuf_ref):
    z = jnp.dot(h_ref[...], w_ref[...], preferred_element_type=F32)
    cw = cw_ref[...]
    acc = cb_ref[...] + cw[3:4] * z
    for tap in range(CONV_W - 1):
        acc = acc + cw[tap:tap + 1] * buf_ref[tap]
    out_ref[...] = _silu_of_twice(acc)
    nbuf_ref[0] = buf_ref[1]
    nbuf_ref[1] = buf_ref[2]
    nbuf_ref[2] = z


def _proj_silu_decode_kernel(h_ref, w_ref, out_ref):
    out_ref[...] = _silu_of_twice(jnp.dot(h_ref[...], w_ref[...], preferred_element_type=F32))


def _proj_conv_decode(h, w, buf, cw, cb):
    rows = h.shape[0]
    ncols = w.shape[1]
    return pl.pallas_call(
        _proj_conv_decode_kernel,
        grid=(ncols // COL_BLOCK,),
        in_specs=[pl.BlockSpec((rows, D_MODEL), lambda j: (0, 0)),
                  pl.BlockSpec((D_MODEL, COL_BLOCK), lambda j: (0, j)),
                  pl.BlockSpec((CONV_W - 1, rows, COL_BLOCK), lambda j: (0, 0, j)),
                  pl.BlockSpec((CONV_W, COL_BLOCK), lambda j: (0, j)),
                  pl.BlockSpec((1, COL_BLOCK), lambda j: (0, j))],
        out_specs=[pl.BlockSpec((rows, COL_BLOCK), lambda j: (0, j)),
                   pl.BlockSpec((CONV_W - 1, rows, COL_BLOCK), lambda j: (0, 0, j))],
        out_shape=[jax.ShapeDtypeStruct((rows, ncols), F32),
                   jax.ShapeDtypeStruct((CONV_W - 1, rows, ncols), F32)],
        compiler_params=_cparams("arbitrary"),
        name="proj_conv_decode",
    )(h, w, buf, cw, cb.reshape(1, ncols))


def _proj_silu_decode(h, w):
    rows = h.shape[0]
    ncols = w.shape[1]
    return pl.pallas_call(
        _proj_silu_decode_kernel,
        grid=(ncols // COL_BLOCK,),
        in_specs=[pl.BlockSpec((rows, D_MODEL), lambda j: (0, 0)),
                  pl.BlockSpec((D_MODEL, COL_BLOCK), lambda j: (0, j))],
        out_specs=pl.BlockSpec((rows, COL_BLOCK), lambda j: (0, j)),
        out_shape=jax.ShapeDtypeStruct((rows, ncols), F32),
        compiler_params=_cparams("arbitrary"),
        name="proj_silu_decode",
    )(h, w)


TOK_BLOCK = 8


def _gdn_step_kernel(qt_ref, kt_ref, v_ref, zg_ref, feat_ref, nw_ref, s_ref, *rest):
    o_ref, so_ref = rest[-2:]
    feat = feat_ref[...]
    vrows = v_ref[...]
    for h in range(GDN_HEADS):
        sl = slice(h * GDN_DK, (h + 1) * GDN_DK)
        qt = qt_ref[0, h]
        kt = kt_ref[0, h]
        qt = qt * (lax.rsqrt(jnp.sum(qt * qt, 0, keepdims=True) + EPS) * (GDN_DK ** -0.5))
        kt = kt * lax.rsqrt(jnp.sum(kt * kt, 0, keepdims=True) + EPS)
        for j in range(TOK_BLOCK):
            kc = kt[:, j:j + 1]
            qc = qt[:, j:j + 1]
            beta = feat[j:j + 1, FEAT_BETA + h:FEAT_BETA + h + 1]
            g = feat[j:j + 1, FEAT_G + h:FEAT_G + h + 1]
            s = s_ref[j, h] * jnp.exp(g)
            kv = jnp.sum(s * kc, 0, keepdims=True)
            d = (vrows[j:j + 1, sl] - kv) * beta
            s = s + kc * d
            so_ref[j, h] = s
            o_ref[j:j + 1, sl] = jnp.sum(s * qc, 0, keepdims=True)
    nw = nw_ref[...]
    for h in range(GDN_HEADS):
        sl = slice(h * GDN_DK, (h + 1) * GDN_DK)
        o = o_ref[:, sl]
        o_ref[:, sl] = o * lax.rsqrt(jnp.mean(o * o, -1, keepdims=True) + EPS) * nw * zg_ref[:, sl]


def _layer_state_call(kernel_fn, name, layer, states, acc, in_specs, out_specs, out_shapes, args):
    rows = states.shape[1]
    blk = (None, TOK_BLOCK) + states.shape[2:]
    zeros = (0,) * (states.ndim - 2)
    state_spec = pl.BlockSpec(blk, lambda i: (layer, i) + zeros)
    in_specs = in_specs + [state_spec]
    args = args + [states]
    aliases = {}
    if acc is not None:
        in_specs = in_specs + [pl.BlockSpec(memory_space=pl.ANY)]
        args = args + [acc]
        aliases = {len(args) - 1: len(out_shapes)}
    return pl.pallas_call(
        kernel_fn,
        grid=(rows // TOK_BLOCK,),
        in_specs=in_specs,
        out_specs=out_specs + [state_spec],
        out_shape=out_shapes + [jax.ShapeDtypeStruct(states.shape, F32)],
        input_output_aliases=aliases,
        compiler_params=_cparams("arbitrary"),
        name=name,
    )(*args)


def _gdn_step(layer, qt, kt, qkv_act, z_act, feat, norm_w, states, acc):
    rows = qkv_act.shape[0]
    return _layer_state_call(
        _gdn_step_kernel, "gdn_step", layer, states, acc,
        [pl.BlockSpec((1, GDN_HEADS, GDN_DK, TOK_BLOCK), lambda i: (i, 0, 0, 0)),
         pl.BlockSpec((1, GDN_HEADS, GDN_DK, TOK_BLOCK), lambda i: (i, 0, 0, 0)),
         pl.BlockSpec((TOK_BLOCK, D_GDN), lambda i: (i, 2)),
         pl.BlockSpec((TOK_BLOCK, D_GDN), lambda i: (i, 0)),
         pl.BlockSpec((TOK_BLOCK, LANES), lambda i: (i, 0)),
         pl.BlockSpec((1, GDN_DK), lambda i: (0, 0))],
        [pl.BlockSpec((TOK_BLOCK, D_GDN), lambda i: (i, 0))],
        [jax.ShapeDtypeStruct((rows, D_GDN), F32)],
        [qt, kt, qkv_act, z_act, feat, norm_w.reshape(1, GDN_DK)])


SSD_STEP_HEADS = 4


def _ssd_step_kernel(xt_ref, b_ref, c_ref, feat_ref, s_ref, *rest):
    yt_ref, so_ref = rest[-2:]
    feat = feat_ref[...]
    heads_per_group = SSM_HEADS // SSM_GROUPS
    ela = jnp.exp(feat)
    for h0 in range(0, SSM_HEADS, SSD_STEP_HEADS):
        hj = [(h, j) for h in range(h0, h0 + SSD_STEP_HEADS) for j in range(TOK_BLOCK)]

        def grp_lanes(h):
            grp = h // heads_per_group
            return slice(grp * SSM_STATE, (grp + 1) * SSM_STATE)

        xdt = [xt_ref[0, h, :, j:j + 1] * feat[j:j + 1, FEAT_DT + h:FEAT_DT + h + 1] for h, j in hj]
        xb = [jnp.broadcast_to(x, (SSM_HEAD_DIM, SSM_STATE)) for x in xdt]
        s = [s_ref[j, h] * ela[j:j + 1, FEAT_LA + h:FEAT_LA + h + 1] + x * b_ref[j:j + 1, grp_lanes(h)]
             for (h, j), x in zip(hj, xb)]
        for (h, j), x in zip(hj, s):
            so_ref[j, h] = x
        y = [jnp.sum(x * c_ref[j:j + 1, grp_lanes(h)], -1, keepdims=True) for (h, j), x in zip(hj, s)]
        for i in range(SSD_STEP_HEADS):
            yt_ref[0, h0 + i] = jnp.concatenate(y[i * TOK_BLOCK:(i + 1) * TOK_BLOCK], axis=1)


def _ssd_step(layer, xt, xbc_act, feat, states, acc):
    rows = xbc_act.shape[0]
    ngrp = rows // TOK_BLOCK
    bc_w = SSM_GROUPS * SSM_STATE
    return _layer_state_call(
        _ssd_step_kernel, "ssd_step", layer, states, acc,
        [pl.BlockSpec((1, SSM_HEADS, SSM_HEAD_DIM, TOK_BLOCK), lambda i: (i, 0, 0, 0)),
         pl.BlockSpec((TOK_BLOCK, bc_w), lambda i: (i, D_SSM // bc_w)),
         pl.BlockSpec((TOK_BLOCK, bc_w), lambda i: (i, D_SSM // bc_w + 1)),
         pl.BlockSpec((TOK_BLOCK, LANES), lambda i: (i, 0))],
        [pl.BlockSpec((1, SSM_HEADS, SSM_HEAD_DIM, TOK_BLOCK), lambda i: (i, 0, 0, 0))],
        [jax.ShapeDtypeStruct((ngrp, SSM_HEADS, SSM_HEAD_DIM, TOK_BLOCK), F32)],
        [xt, xbc_act, xbc_act, feat])


def _ssd_finish_decode_kernel(y_ref, x_ref, zs_ref, d_ref, nw_ref, out_ref):
    out_ref[...] = _ssd_finish(y_ref[...] + d_ref[...] * x_ref[...], zs_ref[...], nw_ref[...])


def _ssd_finish_decode(y, xbc_act, z_act, d_cols, norm_w):
    rows = y.shape[0]
    return pl.pallas_call(
        _ssd_finish_decode_kernel,
        grid=(1,),
        in_specs=[pl.BlockSpec((rows, D_SSM), lambda i: (0, 0)),
                  pl.BlockSpec((rows, D_SSM), lambda i: (0, 0)),
                  pl.BlockSpec((rows, D_SSM), lambda i: (0, 1)),
                  pl.BlockSpec((1, D_SSM), lambda i: (0, 0)),
                  pl.BlockSpec((1, D_SSM), lambda i: (0, 0))],
        out_specs=pl.BlockSpec((rows, D_SSM), lambda i: (0, 0)),
        out_shape=jax.ShapeDtypeStruct((rows, D_SSM), F32),
        compiler_params=_cparams("arbitrary"),
        name="ssd_finish_decode",
    )(y, xbc_act, z_act, d_cols, norm_w.reshape(1, D_SSM))


def _layer_weights(w_in, gdn_A_log, gdn_dt_bias, ssm_A_log, ssm_dt_bias, ssm_D):
    w_qkv = w_in[:, :OFF_Z_G].astype(BF16)
    w_xbc = w_in[:, OFF_XBC:OFF_Z_S].astype(BF16)
    w_z = (0.5 * jnp.concatenate([w_in[:, OFF_Z_G:OFF_B], w_in[:, OFF_Z_S:OFF_DT]], axis=1)
           ).astype(BF16)
    w_dt = w_in[:, OFF_DT:]
    w_small = jnp.concatenate(
        [w_in[:, OFF_B:OFF_XBC], w_dt, w_dt, jnp.zeros((D_MODEL, LANES - FEAT_END), F32)],
        axis=1).astype(BF16)
    zeros8 = jnp.zeros((GDN_HEADS,), F32)
    tail = jnp.zeros((LANES - FEAT_END,), F32)
    p_bias = jnp.concatenate([zeros8, gdn_dt_bias.astype(F32), ssm_dt_bias.astype(F32),
                              ssm_dt_bias.astype(F32), tail]).reshape(1, LANES)
    p_scale = jnp.concatenate([zeros8, -jnp.exp(gdn_A_log.astype(F32)),
                               jnp.ones((SSM_HEADS,), F32), -jnp.exp(ssm_A_log.astype(F32)),
                               tail]).reshape(1, LANES)
    d_cols = jnp.repeat(ssm_D.astype(F32), SSM_HEAD_DIM).reshape(1, D_SSM)
    return w_qkv, w_xbc, w_z, w_small, p_bias, p_scale, d_cols


def kernel(x_prompt, x_sample, state_gdn, state_gdn_conv, state_ssm, state_ssm_conv, meta_tokens,
           norm_w, w_in, gdn_conv_w, gdn_A_log, gdn_dt_bias, gdn_norm_w, ssm_conv_w, ssm_conv_b,
           ssm_A_log, ssm_dt_bias, ssm_D, ssm_norm_w, w_out, final_norm_w):
    bp, seq, _ = x_prompt.shape
    bd = x_sample.shape[0]
    t_pad = FRONT_PAD + N_META + seq
    meta = jnp.broadcast_to(meta_tokens.astype(F32)[None], (bp, N_META, D_MODEL))
    xp = jnp.concatenate([jnp.zeros((bp, FRONT_PAD, D_MODEL), F32), meta, x_prompt.astype(F32)],
                         axis=1).reshape(bp * t_pad, D_MODEL)
    xs = x_sample.astype(F32).reshape(bd, D_MODEL)
    hp = _rmsnorm(xp, norm_w[0], ROW_TILE)
    hs = _rmsnorm(xs, norm_w[0], bd)
    zero_bias = jnp.zeros((GDN_CONV_DIM,), F32)
    ngrp = bd // TOK_BLOCK
    p_g, p_gc, p_s, p_sc = [], [], [], []
    d_g, d_s = None, None
    d_gc, d_sc = [], []
    for l in range(DEPTH):
        w_qkv, w_xbc, w_z, w_small, p_bias, p_scale, d_cols = _layer_weights(
            w_in[l], gdn_A_log[l], gdn_dt_bias[l], ssm_A_log[l], ssm_dt_bias[l], ssm_D[l])
        w_o = w_out[l].astype(BF16)
        last = l == DEPTH - 1
        nxt_w = final_norm_w if last else norm_w[l + 1]
        h_dtype = F32 if last else BF16
        gcw = 0.5 * gdn_conv_w[l].astype(F32)
        scw = 0.5 * ssm_conv_w[l].astype(F32)
        scb = 0.5 * ssm_conv_b[l].astype(F32)

        h3 = hp.reshape(bp, t_pad, D_MODEL)
        qkv_act, qkv_tail = _proj_conv(h3, w_qkv, gcw, None)
        xbc_act, xbc_tail = _proj_conv(h3, w_xbc, scw, scb)
        z_act = _proj_silu(h3, w_z)
        feat, featt, cs, cst = _feat_prompt(h3, w_small, p_bias, p_scale)
        d_rows = jnp.broadcast_to(d_cols.reshape(D_SSM, 1), (D_SSM, LANES))
        mix_g, sg = _gdn_chunk(qkv_act, z_act, feat, cs, cst, gdn_norm_w[l].astype(F32))
        mix_s, ss = _ssd_chunk(xbc_act, z_act, featt, cs, cst, d_rows, ssm_norm_w[l].astype(F32))
        xp, hp = _out_proj(mix_g.reshape(bp * t_pad, D_GDN), mix_s.reshape(bp * t_pad, D_SSM), xp,
                           w_o, nxt_w.astype(F32), h_dtype, ROW_TILE, t_pad)
        p_g.append(sg)
        p_gc.append(qkv_tail[:, SUBLANES - (CONV_W - 1):, :])
        p_s.append(ss.reshape(bp, SSM_HEADS, SSM_HEAD_DIM, SSM_STATE))
        p_sc.append(xbc_tail[:, SUBLANES - (CONV_W - 1):, :])

        gbuf = jnp.transpose(state_gdn_conv[l].astype(F32), (1, 0, 2))
        sbuf = jnp.transpose(state_ssm_conv[l].astype(F32), (1, 0, 2))
        qkv_d, gbuf_n = _proj_conv_decode(hs, w_qkv, gbuf, gcw, zero_bias)
        xbc_d, sbuf_n = _proj_conv_decode(hs, w_xbc, sbuf, scw, scb)
        z_d = _proj_silu_decode(hs, w_z)
        feat_d = _feat_decode(hs, w_small, p_bias, p_scale)

        def head_major(a, heads, dim):
            return jnp.transpose(a.reshape(ngrp, TOK_BLOCK, heads, dim), (0, 2, 3, 1))

        qt = head_major(qkv_d[:, :GDN_QK], GDN_HEADS, GDN_DK)
        kt = head_major(qkv_d[:, GDN_QK:2 * GDN_QK], GDN_HEADS, GDN_DK)
        mix_gd, d_g = _gdn_step(l, qt, kt, qkv_d, z_d, feat_d, gdn_norm_w[l].astype(F32),
                                state_gdn.astype(F32), d_g)
        xt = head_major(xbc_d[:, :D_SSM], SSM_HEADS, SSM_HEAD_DIM)
        yt, d_s = _ssd_step(l, xt, xbc_d, feat_d, state_ssm.astype(F32), d_s)
        y_d = jnp.transpose(yt, (0, 3, 1, 2)).reshape(bd, D_SSM)
        mix_sd = _ssd_finish_decode(y_d, xbc_d, z_d, d_cols, ssm_norm_w[l].astype(F32))
        xs, hs = _out_proj(mix_gd, mix_sd, xs, w_o, nxt_w.astype(F32), h_dtype, bd, 0)
        d_gc.append(jnp.transpose(gbuf_n, (1, 0, 2)))
        d_sc.append(jnp.transpose(sbuf_n, (1, 0, 2)))

    y_prompt = hp.reshape(bp, t_pad, D_MODEL)[:, FRONT_PAD + N_META:]
    y_sample = hs.reshape(bd, 1, D_MODEL)
    return (y_prompt, y_sample,
            jnp.stack(p_g), jnp.stack(p_gc), jnp.stack(p_s), jnp.stack(p_sc),
            d_g, jnp.stack(d_gc), d_s, jnp.stack(d_sc))
```

```python
import functools

import jax
import jax.numpy as jnp
from jax import lax
from jax.experimental import pallas as pl
from jax.experimental.pallas import tpu as pltpu

F32 = jnp.float32
BF16 = jnp.bfloat16

D_MODEL = 1024
DEPTH = 4
N_META = 16
GDN_HEADS = 8
GDN_DK = 128
GDN_QK = 1024
D_GDN = 1024
GDN_CONV_DIM = 3072
D_SSM = 1024
SSM_HEADS = 16
SSM_HEAD_DIM = 64
SSM_GROUPS = 2
SSM_STATE = 128
SSM_CONV_DIM = 1536
CONV_W = 4
EPS = 1e-6
OFF_Z_G = GDN_CONV_DIM
OFF_B = OFF_Z_G + D_GDN
OFF_A = OFF_B + GDN_HEADS
OFF_XBC = OFF_A + GDN_HEADS
OFF_Z_S = OFF_XBC + SSM_CONV_DIM
OFF_DT = OFF_Z_S + D_SSM

CHUNK = 128
HALF = CHUNK // 2
FRONT_PAD = 112
SUBLANES = 8
LANES = 128
COL_BLOCK = 512
ROW_TILE = 512
NEG_BIG = -1e30
ACT_DTYPE = BF16
VMEM_LIMIT = 56 * 1024 * 1024

FEAT_BETA = 0
FEAT_G = 8
FEAT_DT = 16
FEAT_LA = 32
FEAT_END = 48


def _cparams(*sem, flags=None):
    return pltpu.CompilerParams(dimension_semantics=sem, vmem_limit_bytes=VMEM_LIMIT, flags=flags)


def _dot(a, b):
    return jnp.dot(a.astype(BF16), b.astype(BF16), preferred_element_type=F32)


def _dot_nt(a, b):
    return lax.dot_general(a.astype(BF16), b.astype(BF16), (((1,), (1,)), ((), ())),
                           preferred_element_type=F32)


def _dot_tn(a, b):
    return lax.dot_general(a.astype(BF16), b.astype(BF16), (((0,), (0,)), ((), ())),
                           preferred_element_type=F32)


def _silu_of_twice(h):
    return h * jnp.tanh(h) + h


def _softplus(x):
    return jnp.maximum(x, 0.0) + jnp.log1p(jnp.exp(-jnp.abs(x)))


def _rmsnorm_kernel(x_ref, w_ref, h_ref):
    x = x_ref[...]
    y = x * lax.rsqrt(jnp.mean(x * x, -1, keepdims=True) + EPS) * w_ref[...]
    h_ref[...] = y.astype(h_ref.dtype)


def _rmsnorm(x, w, tile):
    rows = x.shape[0]
    return pl.pallas_call(
        _rmsnorm_kernel,
        grid=(rows // tile,),
        in_specs=[pl.BlockSpec((tile, D_MODEL), lambda i: (i, 0)),
                  pl.BlockSpec((1, D_MODEL), lambda i: (0, 0))],
        out_specs=pl.BlockSpec((tile, D_MODEL), lambda i: (i, 0)),
        out_shape=jax.ShapeDtypeStruct((rows, D_MODEL), BF16),
        compiler_params=_cparams("arbitrary"),
        name="rmsnorm",
    )(x, w.reshape(1, D_MODEL))


def _embed_prompt_kernel(x_ref, meta_ref, w_ref, xo_ref, ho_ref):
    @pl.when(pl.program_id(1) == 0)
    def _():
        xo_ref[...] = jnp.concatenate([jnp.zeros((FRONT_PAD, D_MODEL), F32), meta_ref[...]], axis=0)

    @pl.when(pl.program_id(1) > 0)
    def _():
        xo_ref[...] = x_ref[0]

    x = xo_ref[...]
    y = x * lax.rsqrt(jnp.mean(x * x, -1, keepdims=True) + EPS) * w_ref[...]
    ho_ref[...] = y.astype(ho_ref.dtype)


def _embed_prompt(x_prompt, meta_tokens, w):
    bp, seq, _ = x_prompt.shape
    nch = (FRONT_PAD + N_META + seq) // CHUNK
    rows = bp * nch * CHUNK
    row_spec = pl.BlockSpec((CHUNK, D_MODEL), lambda b, c: (b * nch + c, 0))
    return pl.pallas_call(
        _embed_prompt_kernel,
        grid=(bp, nch),
        in_specs=[pl.BlockSpec((1, CHUNK, D_MODEL), lambda b, c: (b, jnp.maximum(c - 1, 0), 0)),
                  pl.BlockSpec((N_META, D_MODEL), lambda b, c: (0, 0)),
                  pl.BlockSpec((1, D_MODEL), lambda b, c: (0, 0))],
        out_specs=[row_spec, row_spec],
        out_shape=[jax.ShapeDtypeStruct((rows, D_MODEL), F32),
                   jax.ShapeDtypeStruct((rows, D_MODEL), BF16)],
        compiler_params=_cparams("arbitrary", "arbitrary"),
        name="embed_prompt",
    )(x_prompt, meta_tokens, w.reshape(1, D_MODEL))


PROJ_ROW_BLOCKS = 8
PROJ_LOOKAHEAD = 1


def _proj_conv_kernel(h_ref, w_ref, cw_ref, cb_ref, out_ref, tail_ref, *scratch, t_pad, bias):
    z_refs, y_ref = scratch[:-1], scratch[-1]
    ncol = out_ref.shape[2]
    ntile = ncol // LANES
    rb = t_pad // PROJ_ROW_BLOCKS
    stride = rb // SUBLANES
    cw = cw_ref[...]
    cb = cb_ref[...]
    first_row = lax.broadcasted_iota(jnp.int32, (SUBLANES, LANES), 0) == 0
    zero = jnp.zeros((SUBLANES, LANES), F32)
    dyn_zero = jnp.minimum(pl.program_id(0), 0)

    def matmul(r):
        z = jnp.dot(h_ref[0, r * rb:(r + 1) * rb, :], w_ref[...], preferred_element_type=F32)
        for c in range(ntile):
            z_refs[r][c] = z[:, c * LANES:(c + 1) * LANES]

    def shift_in(x, carry):
        return jnp.where(first_row, carry, pltpu.roll(x, 1, 0))

    def conv(r):
        for c in range(ntile):
            taps = [cw[j:j + 1, c * LANES:(c + 1) * LANES] for j in range(CONV_W)]
            carries = [zero] * (CONV_W - 1)
            if r > 0:
                hist = z_refs[r - 1][c, rb - SUBLANES:rb, :]
                part = taps[0] * hist
                for tap in range(1, CONV_W):
                    carries[tap - 1] = pltpu.roll(part, 1, 0)
                    part = taps[tap] * hist + carries[tap - 1]
            v = [z_refs[r][c, pl.ds(dyn_zero + i, SUBLANES, stride=stride), :] for i in range(stride)]
            acc = [taps[0] * x for x in v]
            for tap in range(1, CONV_W):
                wrapped = shift_in(acc[stride - 1], carries[tap - 1])
                acc = [taps[tap] * x + (wrapped if i == 0 else acc[i - 1]) for i, x in enumerate(v)]
            for i, y in enumerate(acc):
                if bias:
                    y = y + cb[:, c * LANES:(c + 1) * LANES]
                y_ref[c, pl.ds(i, SUBLANES, stride=stride), :] = _silu_of_twice(y)
        for c in range(ntile):
            out_ref[0, r * rb:(r + 1) * rb, c * LANES:(c + 1) * LANES] = y_ref[c].astype(out_ref.dtype)

    for r in range(PROJ_ROW_BLOCKS + PROJ_LOOKAHEAD):
        if r < PROJ_ROW_BLOCKS:
            matmul(r)
        if r >= PROJ_LOOKAHEAD:
            conv(r - PROJ_LOOKAHEAD)
    for c in range(ntile):
        tail_ref[0, :, c * LANES:(c + 1) * LANES] = z_refs[PROJ_ROW_BLOCKS - 1][c, rb - SUBLANES:rb, :]


def _proj_silu_kernel(h_ref, w_ref, out_ref):
    out_ref[0] = _silu_of_twice(
        jnp.dot(h_ref[0], w_ref[...], preferred_element_type=F32)).astype(out_ref.dtype)


def _proj_conv(h3, w, cw, cb):
    bsz, t_pad, _ = h3.shape
    ncols = w.shape[1]
    bias = cb is not None
    if not bias:
        cb = jnp.zeros((ncols,), F32)
    return pl.pallas_call(
        functools.partial(_proj_conv_kernel, t_pad=t_pad, bias=bias),
        grid=(bsz, ncols // COL_BLOCK),
        in_specs=[pl.BlockSpec((1, t_pad, D_MODEL), lambda b, j: (b, 0, 0)),
                  pl.BlockSpec((D_MODEL, COL_BLOCK), lambda b, j: (0, j)),
                  pl.BlockSpec((CONV_W, COL_BLOCK), lambda b, j: (0, j)),
                  pl.BlockSpec((1, COL_BLOCK), lambda b, j: (0, j))],
        out_specs=[pl.BlockSpec((1, t_pad, COL_BLOCK), lambda b, j: (b, 0, j)),
                   pl.BlockSpec((1, SUBLANES, COL_BLOCK), lambda b, j: (b, 0, j))],
        out_shape=[jax.ShapeDtypeStruct((bsz, t_pad, ncols), ACT_DTYPE),
                   jax.ShapeDtypeStruct((bsz, SUBLANES, ncols), F32)],
        scratch_shapes=[pltpu.VMEM((COL_BLOCK // LANES, t_pad // PROJ_ROW_BLOCKS, LANES), F32)]
        * (PROJ_ROW_BLOCKS + 1),
        compiler_params=_cparams("arbitrary", "arbitrary"),
        name="proj_conv",
    )(h3, w, cw, cb.reshape(1, ncols))


def _proj_silu(h3, w):
    bsz, t_pad, _ = h3.shape
    ncols = w.shape[1]
    return pl.pallas_call(
        _proj_silu_kernel,
        grid=(bsz, ncols // COL_BLOCK),
        in_specs=[pl.BlockSpec((1, t_pad, D_MODEL), lambda b, j: (b, 0, 0)),
                  pl.BlockSpec((D_MODEL, COL_BLOCK), lambda b, j: (0, j))],
        out_specs=pl.BlockSpec((1, t_pad, COL_BLOCK), lambda b, j: (b, 0, j)),
        out_shape=jax.ShapeDtypeStruct((bsz, t_pad, ncols), ACT_DTYPE),
        compiler_params=_cparams("arbitrary", "arbitrary"),
        name="proj_silu",
    )(h3, w)


def _features(z, p_bias, p_scale):
    lane = lax.broadcasted_iota(jnp.int32, z.shape, 1)
    sp = p_scale * _softplus(z + p_bias)
    return jnp.where(lane < FEAT_G, jax.nn.sigmoid(z), jnp.where(lane < FEAT_END, sp, 0.0))


def _split3(x):
    x1 = x.astype(BF16)
    r1 = x - x1.astype(F32)
    x2 = r1.astype(BF16)
    x3 = (r1 - x2.astype(F32)).astype(BF16)
    return x1, x2, x3


def _feat_prompt_kernel(h_ref, w_ref, pb_ref, ps_ref, feat_ref, featt_ref, cs_ref, cst_ref, *,
                        t_pad):
    z = jnp.dot(h_ref[0], w_ref[...], preferred_element_type=F32)
    feat = _features(z, pb_ref[...], ps_ref[...])
    row = lax.broadcasted_iota(jnp.int32, feat.shape, 0)
    feat = jnp.where(row < FRONT_PAD, 0.0, feat)
    feat_ref[0] = feat
    r = lax.broadcasted_iota(jnp.int32, (CHUNK, CHUNK), 0)
    c = lax.broadcasted_iota(jnp.int32, (CHUNK, CHUNK), 1)
    tri = (r >= c).astype(BF16)
    for ch in range(t_pad // CHUNK):
        tile = feat[ch * CHUNK:(ch + 1) * CHUNK]
        featt_ref[0, ch] = tile.T
        x1, x2, x3 = _split3(tile)
        cs = (jnp.dot(tri, x1, preferred_element_type=F32)
              + jnp.dot(tri, x2, preferred_element_type=F32)
              + jnp.dot(tri, x3, preferred_element_type=F32))
        cs_ref[0, ch * CHUNK:(ch + 1) * CHUNK, :] = cs
        cst_ref[0, ch] = cs.T


def _feat_prompt(h3, w_small, p_bias, p_scale):
    bsz, t_pad, _ = h3.shape
    nch = t_pad // CHUNK
    return pl.pallas_call(
        functools.partial(_feat_prompt_kernel, t_pad=t_pad),
        grid=(bsz,),
        in_specs=[pl.BlockSpec((1, t_pad, D_MODEL), lambda b: (b, 0, 0)),
                  pl.BlockSpec((D_MODEL, LANES), lambda b: (0, 0)),
                  pl.BlockSpec((1, LANES), lambda b: (0, 0)),
                  pl.BlockSpec((1, LANES), lambda b: (0, 0))],
        out_specs=[pl.BlockSpec((1, t_pad, LANES), lambda b: (b, 0, 0)),
                   pl.BlockSpec((1, nch, CHUNK, LANES), lambda b: (b, 0, 0, 0)),
                   pl.BlockSpec((1, t_pad, LANES), lambda b: (b, 0, 0)),
                   pl.BlockSpec((1, nch, CHUNK, LANES), lambda b: (b, 0, 0, 0))],
        out_shape=[jax.ShapeDtypeStruct((bsz, t_pad, LANES), F32),
                   jax.ShapeDtypeStruct((bsz, nch, CHUNK, LANES), F32),
                   jax.ShapeDtypeStruct((bsz, t_pad, LANES), F32),
                   jax.ShapeDtypeStruct((bsz, nch, CHUNK, LANES), F32)],
        compiler_params=_cparams("arbitrary"),
        name="feat_prompt",
    )(h3, w_small, p_bias, p_scale)


def _feat_decode_kernel(h_ref, w_ref, pb_ref, ps_ref, feat_ref):
    z = jnp.dot(h_ref[...], w_ref[...], preferred_element_type=F32)
    feat_ref[...] = _features(z, pb_ref[...], ps_ref[...])


def _feat_decode(h, w_small, p_bias, p_scale):
    rows = h.shape[0]
    return pl.pallas_call(
        _feat_decode_kernel,
        out_shape=jax.ShapeDtypeStruct((rows, LANES), F32),
        compiler_params=pltpu.CompilerParams(vmem_limit_bytes=VMEM_LIMIT),
        name="feat_decode",
    )(h, w_small, p_bias, p_scale)


def _gdn_chunk_kernel(q_ref, k_ref, v_ref, zg_ref, feat_ref, cs_ref, cst_ref, nw_ref,
                      o_ref, sfin_ref, s_ref, *, nseq):
    ch = pl.program_id(1)

    @pl.when(ch == 0)
    def _():
        s_ref[...] = jnp.zeros(s_ref.shape, F32)

    row = lax.broadcasted_iota(jnp.int32, (CHUNK, CHUNK), 0)
    col = lax.broadcasted_iota(jnp.int32, (CHUNK, CHUNK), 1)
    incl = row >= col
    strict = row > col
    row_c = lax.broadcasted_iota(jnp.int32, (HALF, CHUNK), 0)
    col_c = lax.broadcasted_iota(jnp.int32, (HALF, CHUNK), 1)
    lo_c = col_c < HALF
    eye_c = (jnp.bitwise_and(col_c, HALF - 1) == row_c).astype(F32)
    zero_c = jnp.zeros((HALF, CHUNK), BF16)

    def bdiag(x):
        return jnp.concatenate([jnp.where(lo_c, x, zero_c), jnp.where(lo_c, zero_c, x)], axis=0)

    nw = nw_ref[...]
    chains = [(j, h) for j in range(nseq) for h in range(GDN_HEADS)]

    def lanes(h):
        return slice(h * GDN_DK, (h + 1) * GDN_DK)

    def unit(x, scale):
        return x * (lax.rsqrt(jnp.sum(x * x, -1, keepdims=True) + EPS) * scale)

    q = [unit(q_ref[j, :, lanes(h)].astype(F32), GDN_DK ** -0.5) for j, h in chains]
    k = [unit(k_ref[j, :, lanes(h)].astype(F32), 1.0) for j, h in chains]
    beta = [feat_ref[j, :, FEAT_BETA + h:FEAT_BETA + h + 1] for j, h in chains]
    gc = [cs_ref[j, :, FEAT_G + h:FEAT_G + h + 1] for j, h in chains]
    gr = [cst_ref[j, 0, FEAT_G + h:FEAT_G + h + 1, :] for j, h in chains]
    decay = [jnp.exp(jnp.where(incl, c - r, NEG_BIG)) for c, r in zip(gc, gr)]
    kb = [x * b for x, b in zip(k, beta)]
    kbf = [x.astype(BF16) for x in k]
    a = [jnp.where(strict, _dot_nt(x, y) * d, 0.0) for x, y, d in zip(kb, kbf, decay)]
    p = [jnp.where(lo_c, -x[:HALF], -x[HALF:]) for x in a]
    inv = [eye_c + x for x in p]
    r_p = [bdiag(x.astype(BF16)) for x in p]
    for _ in range(5):
        p = [_dot(x, r) for x, r in zip(p, r_p)]
        r_p = [bdiag(x.astype(BF16)) for x in p]
        inv = [x + _dot(x, r) for x, r in zip(inv, r_p)]
    invb = [x.astype(BF16) for x in inv]
    r_a21 = [jnp.concatenate([zero_c, jnp.where(lo_c, x[HALF:].astype(BF16), zero_c)], axis=0)
             for x in a]
    fold = [_dot(x, r) for x, r in zip(invb, r_a21)]
    r_i11 = [jnp.concatenate([jnp.where(lo_c, x, zero_c), zero_c], axis=0) for x in invb]
    fold = [_dot(x, r) for x, r in zip(fold, r_i11)]
    t_inv = [jnp.concatenate([jnp.where(lo_c, x, 0.0), jnp.where(lo_c, -y, x)], axis=0)
             for x, y in zip(inv, fold)]
    eg = [jnp.exp(c) for c in gc]
    rhs = [jnp.concatenate([v_ref[j, :, lanes(h)].astype(F32) * b, x * e], axis=1)
           for (j, h), b, x, e in zip(chains, beta, kb, eg)]
    uw = [_dot(t, r) for t, r in zip(t_inv, rhs)]
    attn = [_dot_nt(x, y) * d for x, y, d in zip(q, kbf, decay)]
    g_last = [c[CHUNK - 1:CHUNK, :] for c in gc]
    k_dec = [x * jnp.exp(l - c) for x, l, c in zip(k, g_last, gc)]
    s = [s_ref[i] for i in range(len(chains))]
    ws_qs = [_dot(jnp.concatenate([x[:, GDN_DK:], y * e], axis=0), z)
             for x, y, e, z in zip(uw, q, eg, s)]
    v_new = [(x[:, :GDN_DK] - y[:CHUNK]).astype(BF16) for x, y in zip(uw, ws_qs)]
    o = [y[CHUNK:] + _dot(x, z) for y, x, z in zip(ws_qs, attn, v_new)]
    for i, (j, h) in enumerate(chains):
        s_ref[i] = s[i] * jnp.exp(g_last[i]) + _dot_tn(k_dec[i], v_new[i])
        on = o[i] * lax.rsqrt(jnp.mean(o[i] * o[i], -1, keepdims=True) + EPS)
        o_ref[j, :, lanes(h)] = (on * nw * zg_ref[j, :, lanes(h)].astype(F32)).astype(o_ref.dtype)

    @pl.when(ch == pl.num_programs(1) - 1)
    def _():
        for j in range(nseq):
            sfin_ref[j] = s_ref[j * GDN_HEADS:(j + 1) * GDN_HEADS]


GDN_SEQ_PER_STEP = 2


def _gdn_chunk(qkv_act, z_act, feat, cs, cst, norm_w):
    bsz, t_pad, _ = qkv_act.shape
    nch = t_pad // CHUNK
    nseq = GDN_SEQ_PER_STEP if bsz % GDN_SEQ_PER_STEP == 0 else 1

    def qkv_spec(part):
        return pl.BlockSpec((nseq, CHUNK, GDN_QK), lambda b, c: (b, c, part))

    return pl.pallas_call(
        functools.partial(_gdn_chunk_kernel, nseq=nseq),
        grid=(bsz // nseq, nch),
        in_specs=[qkv_spec(0), qkv_spec(1), qkv_spec(2),
                  pl.BlockSpec((nseq, CHUNK, D_GDN), lambda b, c: (b, c, 0)),
                  pl.BlockSpec((nseq, CHUNK, LANES), lambda b, c: (b, c, 0)),
                  pl.BlockSpec((nseq, CHUNK, LANES), lambda b, c: (b, c, 0)),
                  pl.BlockSpec((nseq, 1, CHUNK, LANES), lambda b, c: (b, c, 0, 0)),
                  pl.BlockSpec((1, GDN_DK), lambda b, c: (0, 0))],
        out_specs=[pl.BlockSpec((nseq, CHUNK, D_GDN), lambda b, c: (b, c, 0)),
                   pl.BlockSpec((nseq, GDN_HEADS, GDN_DK, GDN_DK), lambda b, c: (b, 0, 0, 0))],
        out_shape=[jax.ShapeDtypeStruct((bsz, t_pad, D_GDN), ACT_DTYPE),
                   jax.ShapeDtypeStruct((bsz, GDN_HEADS, GDN_DK, GDN_DK), F32)],
        scratch_shapes=[pltpu.VMEM((nseq * GDN_HEADS, GDN_DK, GDN_DK), F32)],
        compiler_params=_cparams("arbitrary", "arbitrary"),
        name="gdn_chunk",
    )(qkv_act, qkv_act, qkv_act, z_act, feat, cs, cst, norm_w.reshape(1, GDN_DK))


def _ssd_finish(y, zs, nw):
    yg = y * zs
    gw = D_SSM // SSM_GROUPS
    parts = []
    for g in range(SSM_GROUPS):
        blk = yg[:, g * gw:(g + 1) * gw]
        parts.append(blk * lax.rsqrt(jnp.mean(blk * blk, -1, keepdims=True) + EPS))
    return jnp.concatenate(parts, axis=1) * nw


def _ssd_chunk_kernel(x_ref, b_ref, c_ref, zs_ref, featt_ref, cs_ref, cst_ref, d_ref, nw_ref,
                      y_ref, sfin_ref, s_ref, *, nseq):
    ch = pl.program_id(1)

    @pl.when(ch == 0)
    def _():
        s_ref[...] = jnp.zeros(s_ref.shape, F32)

    row = lax.broadcasted_iota(jnp.int32, (CHUNK, CHUNK), 0)
    col = lax.broadcasted_iota(jnp.int32, (CHUNK, CHUNK), 1)
    upper = col >= row
    top = row < SSM_HEAD_DIM
    heads_per_group = SSM_HEADS // SSM_GROUPS
    npair = SSM_HEADS // 2
    ecs, edl = [], []
    for j in range(nseq):
        cst = cst_ref[j, 0]
        ecs.append(jnp.exp(cst))
        edl.append(jnp.exp(cst[:, CHUNK - 1:CHUNK] - cst))
    grp_slices = [slice(g * SSM_STATE, (g + 1) * SSM_STATE) for g in range(SSM_GROUPS)]
    cbt = [[_dot_nt(b_ref[j, :, gs], c_ref[j, :, gs]) for gs in grp_slices] for j in range(nseq)]
    chains = [(j, p) for j in range(nseq) for p in range(npair)]

    def grp_of(p):
        return grp_slices[(2 * p) // heads_per_group]

    def rows2(tile, base, p):
        return jnp.where(top, tile[base + 2 * p:base + 2 * p + 1, :],
                         tile[base + 2 * p + 1:base + 2 * p + 2, :])

    def decay_t(j, h):
        ac_row = cst_ref[j, 0, FEAT_LA + h:FEAT_LA + h + 1, :]
        ac_col = cs_ref[j, :, FEAT_LA + h:FEAT_LA + h + 1]
        return jnp.exp(jnp.where(upper, ac_row - ac_col, NEG_BIG))

    xt = [x_ref[j, :, p * LANES:(p + 1) * LANES].astype(F32).T for j, p in chains]
    xdt = [x * rows2(featt_ref[j, 0], FEAT_DT, p) for x, (j, p) in zip(xt, chains)]
    st = [jnp.concatenate([cbt[j][(2 * p) // heads_per_group] * decay_t(j, 2 * p),
                           cbt[j][(2 * p) // heads_per_group] * decay_t(j, 2 * p + 1)], axis=0)
          for j, p in chains]
    lhs = [jnp.concatenate([jnp.where(top, x, 0.0), jnp.where(top, 0.0, x)], axis=1) for x in xdt]
    y_diag = [_dot(l, r) for l, r in zip(lhs, st)]
    s = [s_ref[i * LANES:(i + 1) * LANES, :] for i in range(len(chains))]
    y_off = [_dot_nt(z, c_ref[j, :, grp_of(p)]) * rows2(ecs[j], FEAT_LA, p)
             for z, (j, p) in zip(s, chains)]
    ds = [_dot(x * rows2(edl[j], FEAT_LA, p), b_ref[j, :, grp_of(p)])
          for x, (j, p) in zip(xdt, chains)]
    for i, (j, p) in enumerate(chains):
        last = rows2(ecs[j], FEAT_LA, p)[:, CHUNK - 1:CHUNK]
        s_ref[i * LANES:(i + 1) * LANES, :] = s[i] * last + ds[i]
    yt = [a + b + d_ref[p * LANES:(p + 1) * LANES, :] * x
          for a, b, x, (j, p) in zip(y_diag, y_off, xt, chains)]
    for j in range(nseq):
        y = jnp.concatenate([yt[j * npair + p].T for p in range(npair)], axis=1)
        y_ref[j] = _ssd_finish(y, zs_ref[j].astype(F32), nw_ref[...]).astype(y_ref.dtype)

    @pl.when(ch == pl.num_programs(1) - 1)
    def _():
        rows = SSM_HEADS * SSM_HEAD_DIM
        for j in range(nseq):
            sfin_ref[j] = s_ref[j * rows:(j + 1) * rows, :]


SSD_SEQ_PER_STEP = 2


def _ssd_chunk(xbc_act, z_act, featt, cs, cst, d_rows, norm_w):
    bsz, t_pad, _ = xbc_act.shape
    nch = t_pad // CHUNK
    bc_w = SSM_GROUPS * SSM_STATE
    nseq = SSD_SEQ_PER_STEP if bsz % SSD_SEQ_PER_STEP == 0 else 1
    rows = SSM_HEADS * SSM_HEAD_DIM
    return pl.pallas_call(
        functools.partial(_ssd_chunk_kernel, nseq=nseq),
        grid=(bsz // nseq, nch),
        in_specs=[pl.BlockSpec((nseq, CHUNK, D_SSM), lambda b, c: (b, c, 0)),
                  pl.BlockSpec((nseq, CHUNK, bc_w), lambda b, c: (b, c, D_SSM // bc_w)),
                  pl.BlockSpec((nseq, CHUNK, bc_w), lambda b, c: (b, c, D_SSM // bc_w + 1)),
                  pl.BlockSpec((nseq, CHUNK, D_SSM), lambda b, c: (b, c, 1)),
                  pl.BlockSpec((nseq, 1, CHUNK, LANES), lambda b, c: (b, c, 0, 0)),
                  pl.BlockSpec((nseq, CHUNK, LANES), lambda b, c: (b, c, 0)),
                  pl.BlockSpec((nseq, 1, CHUNK, LANES), lambda b, c: (b, c, 0, 0)),
                  pl.BlockSpec((rows, LANES), lambda b, c: (0, 0)),
                  pl.BlockSpec((1, D_SSM), lambda b, c: (0, 0))],
        out_specs=[pl.BlockSpec((nseq, CHUNK, D_SSM), lambda b, c: (b, c, 0)),
                   pl.BlockSpec((nseq, rows, SSM_STATE), lambda b, c: (b, 0, 0))],
        out_shape=[jax.ShapeDtypeStruct((bsz, t_pad, D_SSM), ACT_DTYPE),
                   jax.ShapeDtypeStruct((bsz, rows, SSM_STATE), F32)],
        scratch_shapes=[pltpu.VMEM((nseq * rows, SSM_STATE), F32)],
        compiler_params=_cparams("arbitrary", "arbitrary"),
        name="ssd_chunk",
    )(xbc_act, xbc_act, xbc_act, z_act, featt, cs, cst, d_rows, norm_w.reshape(1, D_SSM))


def _out_proj_kernel(mg_ref, ms_ref, x_ref, w_ref, nw_ref, xo_ref, ho_ref, *, seq_rows):
    acc = (jnp.dot(mg_ref[...].astype(BF16), w_ref[0:D_GDN, :], preferred_element_type=F32)
           + jnp.dot(ms_ref[...].astype(BF16), w_ref[D_GDN:, :], preferred_element_type=F32))
    xn = x_ref[...] + acc
    if seq_rows:
        tile = xn.shape[0]
        start = lax.rem(pl.program_id(0) * tile, seq_rows)
        r = start + lax.broadcasted_iota(jnp.int32, xn.shape, 0)
        pad = (r < FRONT_PAD) | ((r >= seq_rows) & (r < seq_rows + FRONT_PAD))
        xn = jnp.where(pad, 0.0, xn)
    xo_ref[...] = xn
    hn = xn * lax.rsqrt(jnp.mean(xn * xn, -1, keepdims=True) + EPS) * nw_ref[...]
    ho_ref[...] = hn.astype(ho_ref.dtype)


def _out_proj(mix_g, mix_s, x, w_out, next_norm_w, h_dtype, tile, seq_rows):
    rows = x.shape[0]
    row_spec = pl.BlockSpec((tile, D_MODEL), lambda i: (i, 0))
    return pl.pallas_call(
        functools.partial(_out_proj_kernel, seq_rows=seq_rows),
        grid=(rows // tile,),
        in_specs=[row_spec, row_spec, row_spec,
                  pl.BlockSpec((D_GDN + D_SSM, D_MODEL), lambda i: (0, 0)),
                  pl.BlockSpec((1, D_MODEL), lambda i: (0, 0))],
        out_specs=[row_spec, row_spec],
        out_shape=[jax.ShapeDtypeStruct((rows, D_MODEL), F32),
                   jax.ShapeDtypeStruct((rows, D_MODEL), h_dtype)],
        compiler_params=_cparams("arbitrary"),
        name="out_proj",
    )(mix_g, mix_s, x, w_out, next_norm_w.reshape(1, D_MODEL))


def _proj_conv_decode_kernel(h_ref, w_ref, buf_ref, cw_ref, cb_ref, out_ref, nbuf_ref):
    z = jnp.dot(h_ref[...], w_ref[...], preferred_element_type=F32)
    cw = cw_ref[...]
    acc = cb_ref[...] + cw[3:4] * z
    for tap in range(CONV_W - 1):
        acc = acc + cw[tap:tap + 1] * buf_ref[tap]
    out_ref[...] = _silu_of_twice(acc)
    nbuf_ref[0] = buf_ref[1]
    nbuf_ref[1] = buf_ref[2]
    nbuf_ref[2] = z


def _proj_silu_decode_kernel(h_ref, w_ref, out_ref):
    out_ref[...] = _silu_of_twice(jnp.dot(h_ref[...], w_ref[...], preferred_element_type=F32))


def _proj_conv_decode(h, w, buf, cw, cb):
    rows = h.shape[0]
    ncols = w.shape[1]
    return pl.pallas_call(
        _proj_conv_decode_kernel,
        grid=(ncols // COL_BLOCK,),
        in_specs=[pl.BlockSpec((rows, D_MODEL), lambda j: (0, 0)),
                  pl.BlockSpec((D_MODEL, COL_BLOCK), lambda j: (0, j)),
                  pl.BlockSpec((CONV_W - 1, rows, COL_BLOCK), lambda j: (0, 0, j)),
                  pl.BlockSpec((CONV_W, COL_BLOCK), lambda j: (0, j)),
                  pl.BlockSpec((1, COL_BLOCK), lambda j: (0, j))],
        out_specs=[pl.BlockSpec((rows, COL_BLOCK), lambda j: (0, j)),
                   pl.BlockSpec((CONV_W - 1, rows, COL_BLOCK), lambda j: (0, 0, j))],
        out_shape=[jax.ShapeDtypeStruct((rows, ncols), F32),
                   jax.ShapeDtypeStruct((CONV_W - 1, rows, ncols), F32)],
        compiler_params=_cparams("arbitrary"),
        name="proj_conv_decode",
    )(h, w, buf, cw, cb.reshape(1, ncols))


def _proj_silu_decode(h, w):
    rows = h.shape[0]
    ncols = w.shape[1]
    return pl.pallas_call(
        _proj_silu_decode_kernel,
        grid=(ncols // COL_BLOCK,),
        in_specs=[pl.BlockSpec((rows, D_MODEL), lambda j: (0, 0)),
                  pl.BlockSpec((D_MODEL, COL_BLOCK), lambda j: (0, j))],
        out_specs=pl.BlockSpec((rows, COL_BLOCK), lambda j: (0, j)),
        out_shape=jax.ShapeDtypeStruct((rows, ncols), F32),
        compiler_params=_cparams("arbitrary"),
        name="proj_silu_decode",
    )(h, w)


TOK_BLOCK = 8


def _gdn_step_kernel(qt_ref, kt_ref, v_ref, zg_ref, feat_ref, nw_ref, s_ref, *rest):
    o_ref, so_ref = rest[-2:]
    feat = feat_ref[...]
    vrows = v_ref[...]
    for h in range(GDN_HEADS):
        sl = slice(h * GDN_DK, (h + 1) * GDN_DK)
        qt = qt_ref[0, h]
        kt = kt_ref[0, h]
        qt = qt * (lax.rsqrt(jnp.sum(qt * qt, 0, keepdims=True) + EPS) * (GDN_DK ** -0.5))
        kt = kt * lax.rsqrt(jnp.sum(kt * kt, 0, keepdims=True) + EPS)
        for j in range(TOK_BLOCK):
            kc = kt[:, j:j + 1]
            qc = qt[:, j:j + 1]
            beta = feat[j:j + 1, FEAT_BETA + h:FEAT_BETA + h + 1]
            g = feat[j:j + 1, FEAT_G + h:FEAT_G + h + 1]
            s = s_ref[j, h] * jnp.exp(g)
            kv = jnp.sum(s * kc, 0, keepdims=True)
            d = (vrows[j:j + 1, sl] - kv) * beta
            s = s + kc * d
            so_ref[j, h] = s
            o_ref[j:j + 1, sl] = jnp.sum(s * qc, 0, keepdims=True)
    nw = nw_ref[...]
    for h in range(GDN_HEADS):
        sl = slice(h * GDN_DK, (h + 1) * GDN_DK)
        o = o_ref[:, sl]
        o_ref[:, sl] = o * lax.rsqrt(jnp.mean(o * o, -1, keepdims=True) + EPS) * nw * zg_ref[:, sl]


def _layer_state_call(kernel_fn, name, layer, states, acc, in_specs, out_specs, out_shapes, args):
    rows = states.shape[1]
    blk = (None, TOK_BLOCK) + states.shape[2:]
    zeros = (0,) * (states.ndim - 2)
    state_spec = pl.BlockSpec(blk, lambda i: (layer, i) + zeros)
    in_specs = in_specs + [state_spec]
    args = args + [states]
    aliases = {}
    if acc is not None:
        in_specs = in_specs + [pl.BlockSpec(memory_space=pl.ANY)]
        args = args + [acc]
        aliases = {len(args) - 1: len(out_shapes)}
    return pl.pallas_call(
        kernel_fn,
        grid=(rows // TOK_BLOCK,),
        in_specs=in_specs,
        out_specs=out_specs + [state_spec],
        out_shape=out_shapes + [jax.ShapeDtypeStruct(states.shape, F32)],
        input_output_aliases=aliases,
        compiler_params=_cparams("arbitrary"),
        name=name,
    )(*args)


def _gdn_step(layer, qt, kt, qkv_act, z_act, feat, norm_w, states, acc):
    rows = qkv_act.shape[0]
    return _layer_state_call(
        _gdn_step_kernel, "gdn_step", layer, states, acc,
        [pl.BlockSpec((1, GDN_HEADS, GDN_DK, TOK_BLOCK), lambda i: (i, 0, 0, 0)),
         pl.BlockSpec((1, GDN_HEADS, GDN_DK, TOK_BLOCK), lambda i: (i, 0, 0, 0)),
         pl.BlockSpec((TOK_BLOCK, D_GDN), lambda i: (i, 2)),
         pl.BlockSpec((TOK_BLOCK, D_GDN), lambda i: (i, 0)),
         pl.BlockSpec((TOK_BLOCK, LANES), lambda i: (i, 0)),
         pl.BlockSpec((1, GDN_DK), lambda i: (0, 0))],
        [pl.BlockSpec((TOK_BLOCK, D_GDN), lambda i: (i, 0))],
        [jax.ShapeDtypeStruct((rows, D_GDN), F32)],
        [qt, kt, qkv_act, z_act, feat, norm_w.reshape(1, GDN_DK)])


SSD_STEP_HEADS = 4


def _ssd_step_kernel(xt_ref, b_ref, c_ref, feat_ref, s_ref, *rest):
    yt_ref, so_ref = rest[-2:]
    feat = feat_ref[...]
    heads_per_group = SSM_HEADS // SSM_GROUPS
    ela = jnp.exp(feat)
    for h0 in range(0, SSM_HEADS, SSD_STEP_HEADS):
        hj = [(h, j) for h in range(h0, h0 + SSD_STEP_HEADS) for j in range(TOK_BLOCK)]

        def grp_lanes(h):
            grp = h // heads_per_group
            return slice(grp * SSM_STATE, (grp + 1) * SSM_STATE)

        xdt = [xt_ref[0, h, :, j:j + 1] * feat[j:j + 1, FEAT_DT + h:FEAT_DT + h + 1] for h, j in hj]
        xb = [jnp.broadcast_to(x, (SSM_HEAD_DIM, SSM_STATE)) for x in xdt]
        s = [s_ref[j, h] * ela[j:j + 1, FEAT_LA + h:FEAT_LA + h + 1] + x * b_ref[j:j + 1, grp_lanes(h)]
             for (h, j), x in zip(hj, xb)]
        for (h, j), x in zip(hj, s):
            so_ref[j, h] = x
        y = [jnp.sum(x * c_ref[j:j + 1, grp_lanes(h)], -1, keepdims=True) for (h, j), x in zip(hj, s)]
        for i in range(SSD_STEP_HEADS):
            yt_ref[0, h0 + i] = jnp.concatenate(y[i * TOK_BLOCK:(i + 1) * TOK_BLOCK], axis=1)


def _ssd_step(layer, xt, xbc_act, feat, states, acc):
    rows = xbc_act.shape[0]
    ngrp = rows // TOK_BLOCK
    bc_w = SSM_GROUPS * SSM_STATE
    return _layer_state_call(
        _ssd_step_kernel, "ssd_step", layer, states, acc,
        [pl.BlockSpec((1, SSM_HEADS, SSM_HEAD_DIM, TOK_BLOCK), lambda i: (i, 0, 0, 0)),
         pl.BlockSpec((TOK_BLOCK, bc_w), lambda i: (i, D_SSM // bc_w)),
         pl.BlockSpec((TOK_BLOCK, bc_w), lambda i: (i, D_SSM // bc_w + 1)),
         pl.BlockSpec((TOK_BLOCK, LANES), lambda i: (i, 0))],
        [pl.BlockSpec((1, SSM_HEADS, SSM_HEAD_DIM, TOK_BLOCK), lambda i: (i, 0, 0, 0))],
        [jax.ShapeDtypeStruct((ngrp, SSM_HEADS, SSM_HEAD_DIM, TOK_BLOCK), F32)],
        [xt, xbc_act, xbc_act, feat])


def _ssd_finish_decode_kernel(y_ref, x_ref, zs_ref, d_ref, nw_ref, out_ref):
    out_ref[...] = _ssd_finish(y_ref[...] + d_ref[...] * x_ref[...], zs_ref[...], nw_ref[...])


def _ssd_finish_decode(y, xbc_act, z_act, d_cols, norm_w):
    rows = y.shape[0]
    return pl.pallas_call(
        _ssd_finish_decode_kernel,
        grid=(1,),
        in_specs=[pl.BlockSpec((rows, D_SSM), lambda i: (0, 0)),
                  pl.BlockSpec((rows, D_SSM), lambda i: (0, 0)),
                  pl.BlockSpec((rows, D_SSM), lambda i: (0, 1)),
                  pl.BlockSpec((1, D_SSM), lambda i: (0, 0)),
                  pl.BlockSpec((1, D_SSM), lambda i: (0, 0))],
        out_specs=pl.BlockSpec((rows, D_SSM), lambda i: (0, 0)),
        out_shape=jax.ShapeDtypeStruct((rows, D_SSM), F32),
        compiler_params=_cparams("arbitrary"),
        name="ssd_finish_decode",
    )(y, xbc_act, z_act, d_cols, norm_w.reshape(1, D_SSM))


def _layer_weights(w_in, gdn_A_log, gdn_dt_bias, ssm_A_log, ssm_dt_bias, ssm_D):
    w_qkv = w_in[:, :OFF_Z_G].astype(BF16)
    w_xbc = w_in[:, OFF_XBC:OFF_Z_S].astype(BF16)
    w_z = (0.5 * jnp.concatenate([w_in[:, OFF_Z_G:OFF_B], w_in[:, OFF_Z_S:OFF_DT]], axis=1)
           ).astype(BF16)
    w_dt = w_in[:, OFF_DT:]
    w_small = jnp.concatenate(
        [w_in[:, OFF_B:OFF_XBC], w_dt, w_dt, jnp.zeros((D_MODEL, LANES - FEAT_END), F32)],
        axis=1).astype(BF16)
    zeros8 = jnp.zeros((GDN_HEADS,), F32)
    tail = jnp.zeros((LANES - FEAT_END,), F32)
    p_bias = jnp.concatenate([zeros8, gdn_dt_bias.astype(F32), ssm_dt_bias.astype(F32),
                              ssm_dt_bias.astype(F32), tail]).reshape(1, LANES)
    p_scale = jnp.concatenate([zeros8, -jnp.exp(gdn_A_log.astype(F32)),
                               jnp.ones((SSM_HEADS,), F32), -jnp.exp(ssm_A_log.astype(F32)),
                               tail]).reshape(1, LANES)
    d_cols = jnp.repeat(ssm_D.astype(F32), SSM_HEAD_DIM).reshape(1, D_SSM)
    return w_qkv, w_xbc, w_z, w_small, p_bias, p_scale, d_cols


def kernel(x_prompt, x_sample, state_gdn, state_gdn_conv, state_ssm, state_ssm_conv, meta_tokens,
           norm_w, w_in, gdn_conv_w, gdn_A_log, gdn_dt_bias, gdn_norm_w, ssm_conv_w, ssm_conv_b,
           ssm_A_log, ssm_dt_bias, ssm_D, ssm_norm_w, w_out, final_norm_w):
    bp, seq, _ = x_prompt.shape
    bd = x_sample.shape[0]
    t_pad = FRONT_PAD + N_META + seq
    xp, hp = _embed_prompt(x_prompt.astype(F32), meta_tokens.astype(F32), norm_w[0].astype(F32))
    xs = x_sample.astype(F32).reshape(bd, D_MODEL)
    hs = _rmsnorm(xs, norm_w[0], bd)
    zero_bias = jnp.zeros((GDN_CONV_DIM,), F32)
    ngrp = bd // TOK_BLOCK
    p_g, p_gc, p_s, p_sc = [], [], [], []
    d_g, d_s = None, None
    d_gc, d_sc = [], []
    for l in range(DEPTH):
        w_qkv, w_xbc, w_z, w_small, p_bias, p_scale, d_cols = _layer_weights(
            w_in[l], gdn_A_log[l], gdn_dt_bias[l], ssm_A_log[l], ssm_dt_bias[l], ssm_D[l])
        w_o = w_out[l].astype(BF16)
        last = l == DEPTH - 1
        nxt_w = final_norm_w if last else norm_w[l + 1]
        h_dtype = F32 if last else BF16
        gcw = 0.5 * gdn_conv_w[l].astype(F32)
        scw = 0.5 * ssm_conv_w[l].astype(F32)
        scb = 0.5 * ssm_conv_b[l].astype(F32)

        h3 = hp.reshape(bp, t_pad, D_MODEL)
        qkv_act, qkv_tail = _proj_conv(h3, w_qkv, gcw, None)
        xbc_act, xbc_tail = _proj_conv(h3, w_xbc, scw, scb)
        z_act = _proj_silu(h3, w_z)
        feat, featt, cs, cst = _feat_prompt(h3, w_small, p_bias, p_scale)
        d_rows = jnp.broadcast_to(d_cols.reshape(D_SSM, 1), (D_SSM, LANES))
        mix_g, sg = _gdn_chunk(qkv_act, z_act, feat, cs, cst, gdn_norm_w[l].astype(F32))
        mix_s, ss = _ssd_chunk(xbc_act, z_act, featt, cs, cst, d_rows, ssm_norm_w[l].astype(F32))
        xp, hp = _out_proj(mix_g.reshape(bp * t_pad, D_GDN), mix_s.reshape(bp * t_pad, D_SSM), xp,
                           w_o, nxt_w.astype(F32), h_dtype, ROW_TILE, t_pad)
        p_g.append(sg)
        p_gc.append(qkv_tail[:, SUBLANES - (CONV_W - 1):, :])
        p_s.append(ss.reshape(bp, SSM_HEADS, SSM_HEAD_DIM, SSM_STATE))
        p_sc.append(xbc_tail[:, SUBLANES - (CONV_W - 1):, :])

        gbuf = jnp.transpose(state_gdn_conv[l].astype(F32), (1, 0, 2))
        sbuf = jnp.transpose(state_ssm_conv[l].astype(F32), (1, 0, 2))
        qkv_d, gbuf_n = _proj_conv_decode(hs, w_qkv, gbuf, gcw, zero_bias)
        xbc_d, sbuf_n = _proj_conv_decode(hs, w_xbc, sbuf, scw, scb)
        z_d = _proj_silu_decode(hs, w_z)
        feat_d = _feat_decode(hs, w_small, p_bias, p_scale)

        def head_major(a, heads, dim):
            return jnp.transpose(a.reshape(ngrp, TOK_BLOCK, heads, dim), (0, 2, 3, 1))

        qt = head_major(qkv_d[:, :GDN_QK], GDN_HEADS, GDN_DK)
        kt = head_major(qkv_d[:, GDN_QK:2 * GDN_QK], GDN_HEADS, GDN_DK)
        mix_gd, d_g = _gdn_step(l, qt, kt, qkv_d, z_d, feat_d, gdn_norm_w[l].astype(F32),
                                state_gdn.astype(F32), d_g)
        xt = head_major(xbc_d[:, :D_SSM], SSM_HEADS, SSM_HEAD_DIM)
        yt, d_s = _ssd_step(l, xt, xbc_d, feat_d, state_ssm.astype(F32), d_s)
        y_d = jnp.transpose(yt, (0, 3, 1, 2)).reshape(bd, D_SSM)
        mix_sd = _ssd_finish_decode(y_d, xbc_d, z_d, d_cols, ssm_norm_w[l].astype(F32))
        xs, hs = _out_proj(mix_gd, mix_sd, xs, w_o, nxt_w.astype(F32), h_dtype, bd, 0)
        d_gc.append(jnp.transpose(gbuf_n, (1, 0, 2)))
        d_sc.append(jnp.transpose(sbuf_n, (1, 0, 2)))

    y_prompt = hp.reshape(bp, t_pad, D_MODEL)[:, FRONT_PAD + N_META:]
    y_sample = hs.reshape(bd, 1, D_MODEL)
    return (y_prompt, y_sample,
            jnp.stack(p_g), jnp.stack(p_gc), jnp.stack(p_s), jnp.stack(p_sc),
            d_g, jnp.stack(d_gc), d_s, jnp.stack(d_sc))
```

```python
import functools

import jax
import jax.numpy as jnp
from jax import lax
from jax.experimental import pallas as pl
from jax.experimental.pallas import tpu as pltpu

F32 = jnp.float32
BF16 = jnp.bfloat16

D_MODEL = 1024
DEPTH = 4
N_META = 16
GDN_HEADS = 8
GDN_DK = 128
GDN_QK = 1024
D_GDN = 1024
GDN_CONV_DIM = 3072
D_SSM = 1024
SSM_HEADS = 16
SSM_HEAD_DIM = 64
SSM_GROUPS = 2
SSM_STATE = 128
SSM_CONV_DIM = 1536
CONV_W = 4
EPS = 1e-6
OFF_Z_G = GDN_CONV_DIM
OFF_B = OFF_Z_G + D_GDN
OFF_A = OFF_B + GDN_HEADS
OFF_XBC = OFF_A + GDN_HEADS
OFF_Z_S = OFF_XBC + SSM_CONV_DIM
OFF_DT = OFF_Z_S + D_SSM

CHUNK = 128
HALF = CHUNK // 2
FRONT_PAD = 112
SUBLANES = 8
LANES = 128
COL_BLOCK = 512
ROW_TILE = 512
NEG_BIG = -1e30
ACT_DTYPE = BF16
VMEM_LIMIT = 56 * 1024 * 1024

FEAT_BETA = 0
FEAT_G = 8
FEAT_DT = 16
FEAT_LA = 32
FEAT_END = 48


def _cparams(*sem, flags=None):
    return pltpu.CompilerParams(dimension_semantics=sem, vmem_limit_bytes=VMEM_LIMIT, flags=flags)


def _dot(a, b):
    return jnp.dot(a.astype(BF16), b.astype(BF16), preferred_element_type=F32)


def _dot_nt(a, b):
    return lax.dot_general(a.astype(BF16), b.astype(BF16), (((1,), (1,)), ((), ())),
                           preferred_element_type=F32)


def _dot_tn(a, b):
    return lax.dot_general(a.astype(BF16), b.astype(BF16), (((0,), (0,)), ((), ())),
                           preferred_element_type=F32)


def _silu_of_twice(h):
    return h * jnp.tanh(h) + h


def _softplus(x):
    return jnp.maximum(x, 0.0) + jnp.log1p(jnp.exp(-jnp.abs(x)))


def _rmsnorm_kernel(x_ref, w_ref, h_ref):
    x = x_ref[...]
    y = x * lax.rsqrt(jnp.mean(x * x, -1, keepdims=True) + EPS) * w_ref[...]
    h_ref[...] = y.astype(h_ref.dtype)


def _rmsnorm(x, w, tile):
    rows = x.shape[0]
    return pl.pallas_call(
        _rmsnorm_kernel,
        grid=(rows // tile,),
        in_specs=[pl.BlockSpec((tile, D_MODEL), lambda i: (i, 0)),
                  pl.BlockSpec((1, D_MODEL), lambda i: (0, 0))],
        out_specs=pl.BlockSpec((tile, D_MODEL), lambda i: (i, 0)),
        out_shape=jax.ShapeDtypeStruct((rows, D_MODEL), BF16),
        compiler_params=_cparams("arbitrary"),
        name="rmsnorm",
    )(x, w.reshape(1, D_MODEL))


def _embed_prompt_kernel(x_ref, meta_ref, w_ref, xo_ref, ho_ref):
    w = w_ref[...]

    def emit(row0, x):
        xo_ref[row0:row0 + CHUNK, :] = x
        y = x * lax.rsqrt(jnp.mean(x * x, -1, keepdims=True) + EPS) * w
        ho_ref[row0:row0 + CHUNK, :] = y.astype(ho_ref.dtype)

    emit(0, jnp.concatenate([jnp.zeros((FRONT_PAD, D_MODEL), F32), meta_ref[...]], axis=0))
    for c in range(x_ref.shape[1] // CHUNK):
        emit((c + 1) * CHUNK, x_ref[0, c * CHUNK:(c + 1) * CHUNK, :])


def _embed_prompt(x_prompt, meta_tokens, w):
    bp, seq, _ = x_prompt.shape
    t_pad = FRONT_PAD + N_META + seq
    row_spec = pl.BlockSpec((t_pad, D_MODEL), lambda b: (b, 0))
    return pl.pallas_call(
        _embed_prompt_kernel,
        grid=(bp,),
        in_specs=[pl.BlockSpec((1, seq, D_MODEL), lambda b: (b, 0, 0)),
                  pl.BlockSpec((N_META, D_MODEL), lambda b: (0, 0)),
                  pl.BlockSpec((1, D_MODEL), lambda b: (0, 0))],
        out_specs=[row_spec, row_spec],
        out_shape=[jax.ShapeDtypeStruct((bp * t_pad, D_MODEL), F32),
                   jax.ShapeDtypeStruct((bp * t_pad, D_MODEL), BF16)],
        compiler_params=_cparams("arbitrary"),
        name="embed_prompt",
    )(x_prompt, meta_tokens, w.reshape(1, D_MODEL))


PROJ_ROW_BLOCKS = 8
PROJ_LOOKAHEAD = 1


def _proj_conv_kernel(h_ref, w_ref, cw_ref, cb_ref, out_ref, tail_ref, *scratch, t_pad, bias):
    z_refs, y_ref = scratch[:-1], scratch[-1]
    ncol = out_ref.shape[2]
    ntile = ncol // LANES
    rb = t_pad // PROJ_ROW_BLOCKS
    stride = rb // SUBLANES
    cw = cw_ref[...]
    cb = cb_ref[...]
    first_row = lax.broadcasted_iota(jnp.int32, (SUBLANES, LANES), 0) == 0
    zero = jnp.zeros((SUBLANES, LANES), F32)
    dyn_zero = jnp.minimum(pl.program_id(0), 0)

    def matmul(r):
        z = jnp.dot(h_ref[0, r * rb:(r + 1) * rb, :], w_ref[...], preferred_element_type=F32)
        for c in range(ntile):
            z_refs[r][c] = z[:, c * LANES:(c + 1) * LANES]

    def shift_in(x, carry):
        return jnp.where(first_row, carry, pltpu.roll(x, 1, 0))

    def conv(r):
        for c in range(ntile):
            taps = [cw[j:j + 1, c * LANES:(c + 1) * LANES] for j in range(CONV_W)]
            carries = [zero] * (CONV_W - 1)
            if r > 0:
                hist = z_refs[r - 1][c, rb - SUBLANES:rb, :]
                part = taps[0] * hist
                for tap in range(1, CONV_W):
                    carries[tap - 1] = pltpu.roll(part, 1, 0)
                    part = taps[tap] * hist + carries[tap - 1]
            v = [z_refs[r][c, pl.ds(dyn_zero + i, SUBLANES, stride=stride), :] for i in range(stride)]
            acc = [taps[0] * x for x in v]
            for tap in range(1, CONV_W):
                wrapped = shift_in(acc[stride - 1], carries[tap - 1])
                acc = [taps[tap] * x + (wrapped if i == 0 else acc[i - 1]) for i, x in enumerate(v)]
            for i, y in enumerate(acc):
                if bias:
                    y = y + cb[:, c * LANES:(c + 1) * LANES]
                y_ref[c, pl.ds(i, SUBLANES, stride=stride), :] = _silu_of_twice(y)
        for c in range(ntile):
            out_ref[0, r * rb:(r + 1) * rb, c * LANES:(c + 1) * LANES] = y_ref[c].astype(out_ref.dtype)

    for r in range(PROJ_ROW_BLOCKS + PROJ_LOOKAHEAD):
        if r < PROJ_ROW_BLOCKS:
            matmul(r)
        if r >= PROJ_LOOKAHEAD:
            conv(r - PROJ_LOOKAHEAD)
    for c in range(ntile):
        tail_ref[0, :, c * LANES:(c + 1) * LANES] = z_refs[PROJ_ROW_BLOCKS - 1][c, rb - SUBLANES:rb, :]


def _proj_silu_kernel(h_ref, w_ref, out_ref):
    out_ref[0] = _silu_of_twice(
        jnp.dot(h_ref[0], w_ref[...], preferred_element_type=F32)).astype(out_ref.dtype)


def _proj_conv(h3, w, cw, cb):
    bsz, t_pad, _ = h3.shape
    ncols = w.shape[1]
    bias = cb is not None
    if not bias:
        cb = jnp.zeros((ncols,), F32)
    return pl.pallas_call(
        functools.partial(_proj_conv_kernel, t_pad=t_pad, bias=bias),
        grid=(bsz, ncols // COL_BLOCK),
        in_specs=[pl.BlockSpec((1, t_pad, D_MODEL), lambda b, j: (b, 0, 0)),
                  pl.BlockSpec((D_MODEL, COL_BLOCK), lambda b, j: (0, j)),
                  pl.BlockSpec((CONV_W, COL_BLOCK), lambda b, j: (0, j)),
                  pl.BlockSpec((1, COL_BLOCK), lambda b, j: (0, j))],
        out_specs=[pl.BlockSpec((1, t_pad, COL_BLOCK), lambda b, j: (b, 0, j)),
                   pl.BlockSpec((1, SUBLANES, COL_BLOCK), lambda b, j: (b, 0, j))],
        out_shape=[jax.ShapeDtypeStruct((bsz, t_pad, ncols), ACT_DTYPE),
                   jax.ShapeDtypeStruct((bsz, SUBLANES, ncols), F32)],
        scratch_shapes=[pltpu.VMEM((COL_BLOCK // LANES, t_pad // PROJ_ROW_BLOCKS, LANES), F32)]
        * (PROJ_ROW_BLOCKS + 1),
        compiler_params=_cparams("arbitrary", "arbitrary"),
        name="proj_conv",
    )(h3, w, cw, cb.reshape(1, ncols))


def _proj_silu(h3, w):
    bsz, t_pad, _ = h3.shape
    ncols = w.shape[1]
    return pl.pallas_call(
        _proj_silu_kernel,
        grid=(bsz, ncols // COL_BLOCK),
        in_specs=[pl.BlockSpec((1, t_pad, D_MODEL), lambda b, j: (b, 0, 0)),
                  pl.BlockSpec((D_MODEL, COL_BLOCK), lambda b, j: (0, j))],
        out_specs=pl.BlockSpec((1, t_pad, COL_BLOCK), lambda b, j: (b, 0, j)),
        out_shape=jax.ShapeDtypeStruct((bsz, t_pad, ncols), ACT_DTYPE),
        compiler_params=_cparams("arbitrary", "arbitrary"),
        name="proj_silu",
    )(h3, w)


def _features(z, p_bias, p_scale):
    lane = lax.broadcasted_iota(jnp.int32, z.shape, 1)
    sp = p_scale * _softplus(z + p_bias)
    return jnp.where(lane < FEAT_G, jax.nn.sigmoid(z), jnp.where(lane < FEAT_END, sp, 0.0))


def _split3(x):
    x1 = x.astype(BF16)
    r1 = x - x1.astype(F32)
    x2 = r1.astype(BF16)
    x3 = (r1 - x2.astype(F32)).astype(BF16)
    return x1, x2, x3


def _feat_prompt_kernel(h_ref, w_ref, pb_ref, ps_ref, feat_ref, featt_ref, cs_ref, cst_ref, *,
                        t_pad):
    z = jnp.dot(h_ref[0], w_ref[...], preferred_element_type=F32)
    feat = _features(z, pb_ref[...], ps_ref[...])
    row = lax.broadcasted_iota(jnp.int32, feat.shape, 0)
    feat = jnp.where(row < FRONT_PAD, 0.0, feat)
    feat_ref[0] = feat
    r = lax.broadcasted_iota(jnp.int32, (CHUNK, CHUNK), 0)
    c = lax.broadcasted_iota(jnp.int32, (CHUNK, CHUNK), 1)
    tri = (r >= c).astype(BF16)
    for ch in range(t_pad // CHUNK):
        tile = feat[ch * CHUNK:(ch + 1) * CHUNK]
        featt_ref[0, ch] = tile.T
        x1, x2, x3 = _split3(tile)
        cs = (jnp.dot(tri, x1, preferred_element_type=F32)
              + jnp.dot(tri, x2, preferred_element_type=F32)
              + jnp.dot(tri, x3, preferred_element_type=F32))
        cs_ref[0, ch * CHUNK:(ch + 1) * CHUNK, :] = cs
        cst_ref[0, ch] = cs.T


def _feat_prompt(h3, w_small, p_bias, p_scale):
    bsz, t_pad, _ = h3.shape
    nch = t_pad // CHUNK
    return pl.pallas_call(
        functools.partial(_feat_prompt_kernel, t_pad=t_pad),
        grid=(bsz,),
        in_specs=[pl.BlockSpec((1, t_pad, D_MODEL), lambda b: (b, 0, 0)),
                  pl.BlockSpec((D_MODEL, LANES), lambda b: (0, 0)),
                  pl.BlockSpec((1, LANES), lambda b: (0, 0)),
                  pl.BlockSpec((1, LANES), lambda b: (0, 0))],
        out_specs=[pl.BlockSpec((1, t_pad, LANES), lambda b: (b, 0, 0)),
                   pl.BlockSpec((1, nch, CHUNK, LANES), lambda b: (b, 0, 0, 0)),
                   pl.BlockSpec((1, t_pad, LANES), lambda b: (b, 0, 0)),
                   pl.BlockSpec((1, nch, CHUNK, LANES), lambda b: (b, 0, 0, 0))],
        out_shape=[jax.ShapeDtypeStruct((bsz, t_pad, LANES), F32),
                   jax.ShapeDtypeStruct((bsz, nch, CHUNK, LANES), F32),
                   jax.ShapeDtypeStruct((bsz, t_pad, LANES), F32),
                   jax.ShapeDtypeStruct((bsz, nch, CHUNK, LANES), F32)],
        compiler_params=_cparams("arbitrary"),
        name="feat_prompt",
    )(h3, w_small, p_bias, p_scale)


def _feat_decode_kernel(h_ref, w_ref, pb_ref, ps_ref, feat_ref):
    z = jnp.dot(h_ref[...], w_ref[...], preferred_element_type=F32)
    feat_ref[...] = _features(z, pb_ref[...], ps_ref[...])


def _feat_decode(h, w_small, p_bias, p_scale):
    rows = h.shape[0]
    return pl.pallas_call(
        _feat_decode_kernel,
        out_shape=jax.ShapeDtypeStruct((rows, LANES), F32),
        compiler_params=pltpu.CompilerParams(vmem_limit_bytes=VMEM_LIMIT),
        name="feat_decode",
    )(h, w_small, p_bias, p_scale)


def _gdn_chunk_kernel(q_ref, k_ref, v_ref, zg_ref, feat_ref, cs_ref, cst_ref, nw_ref,
                      o_ref, sfin_ref, s_ref, *, nseq):
    ch = pl.program_id(1)

    @pl.when(ch == 0)
    def _():
        s_ref[...] = jnp.zeros(s_ref.shape, F32)

    row = lax.broadcasted_iota(jnp.int32, (CHUNK, CHUNK), 0)
    col = lax.broadcasted_iota(jnp.int32, (CHUNK, CHUNK), 1)
    incl = row >= col
    strict = row > col
    row_c = lax.broadcasted_iota(jnp.int32, (HALF, CHUNK), 0)
    col_c = lax.broadcasted_iota(jnp.int32, (HALF, CHUNK), 1)
    lo_c = col_c < HALF
    eye_c = (jnp.bitwise_and(col_c, HALF - 1) == row_c).astype(F32)
    zero_c = jnp.zeros((HALF, CHUNK), BF16)

    def bdiag(x):
        return jnp.concatenate([jnp.where(lo_c, x, zero_c), jnp.where(lo_c, zero_c, x)], axis=0)

    nw = nw_ref[...]
    chains = [(j, h) for j in range(nseq) for h in range(GDN_HEADS)]

    def lanes(h):
        return slice(h * GDN_DK, (h + 1) * GDN_DK)

    def unit(x, scale):
        return x * (lax.rsqrt(jnp.sum(x * x, -1, keepdims=True) + EPS) * scale)

    q = [unit(q_ref[j, :, lanes(h)].astype(F32), GDN_DK ** -0.5) for j, h in chains]
    k = [unit(k_ref[j, :, lanes(h)].astype(F32), 1.0) for j, h in chains]
    beta = [feat_ref[j, :, FEAT_BETA + h:FEAT_BETA + h + 1] for j, h in chains]
    gc = [cs_ref[j, :, FEAT_G + h:FEAT_G + h + 1] for j, h in chains]
    gr = [cst_ref[j, 0, FEAT_G + h:FEAT_G + h + 1, :] for j, h in chains]
    decay = [jnp.exp(jnp.where(incl, c - r, NEG_BIG)) for c, r in zip(gc, gr)]
    kb = [x * b for x, b in zip(k, beta)]
    kbf = [x.astype(BF16) for x in k]
    a = [jnp.where(strict, _dot_nt(x, y) * d, 0.0) for x, y, d in zip(kb, kbf, decay)]
    p = [jnp.where(lo_c, -x[:HALF], -x[HALF:]) for x in a]
    inv = [eye_c + x for x in p]
    r_p = [bdiag(x.astype(BF16)) for x in p]
    for _ in range(5):
        p = [_dot(x, r) for x, r in zip(p, r_p)]
        r_p = [bdiag(x.astype(BF16)) for x in p]
        inv = [x + _dot(x, r) for x, r in zip(inv, r_p)]
    invb = [x.astype(BF16) for x in inv]
    r_a21 = [jnp.concatenate([zero_c, jnp.where(lo_c, x[HALF:].astype(BF16), zero_c)], axis=0)
             for x in a]
    fold = [_dot(x, r) for x, r in zip(invb, r_a21)]
    r_i11 = [jnp.concatenate([jnp.where(lo_c, x, zero_c), zero_c], axis=0) for x in invb]
    fold = [_dot(x, r) for x, r in zip(fold, r_i11)]
    t_inv = [jnp.concatenate([jnp.where(lo_c, x, 0.0), jnp.where(lo_c, -y, x)], axis=0)
             for x, y in zip(inv, fold)]
    eg = [jnp.exp(c) for c in gc]
    rhs = [jnp.concatenate([v_ref[j, :, lanes(h)].astype(F32) * b, x * e], axis=1)
           for (j, h), b, x, e in zip(chains, beta, kb, eg)]
    uw = [_dot(t, r) for t, r in zip(t_inv, rhs)]
    attn = [_dot_nt(x, y) * d for x, y, d in zip(q, kbf, decay)]
    g_last = [c[CHUNK - 1:CHUNK, :] for c in gc]
    k_dec = [x * jnp.exp(l - c) for x, l, c in zip(k, g_last, gc)]
    s = [s_ref[i] for i in range(len(chains))]
    ws_qs = [_dot(jnp.concatenate([x[:, GDN_DK:], y * e], axis=0), z)
             for x, y, e, z in zip(uw, q, eg, s)]
    v_new = [(x[:, :GDN_DK] - y[:CHUNK]).astype(BF16) for x, y in zip(uw, ws_qs)]
    o = [y[CHUNK:] + _dot(x, z) for y, x, z in zip(ws_qs, attn, v_new)]
    for i, (j, h) in enumerate(chains):
        s_ref[i] = s[i] * jnp.exp(g_last[i]) + _dot_tn(k_dec[i], v_new[i])
        on = o[i] * lax.rsqrt(jnp.mean(o[i] * o[i], -1, keepdims=True) + EPS)
        o_ref[j, :, lanes(h)] = (on * nw * zg_ref[j, :, lanes(h)].astype(F32)).astype(o_ref.dtype)

    @pl.when(ch == pl.num_programs(1) - 1)
    def _():
        for j in range(nseq):
            sfin_ref[j] = s_ref[j * GDN_HEADS:(j + 1) * GDN_HEADS]


GDN_SEQ_PER_STEP = 2


def _gdn_chunk(qkv_act, z_act, feat, cs, cst, norm_w):
    bsz, t_pad, _ = qkv_act.shape
    nch = t_pad // CHUNK
    nseq = GDN_SEQ_PER_STEP if bsz % GDN_SEQ_PER_STEP == 0 else 1

    def qkv_spec(part):
        return pl.BlockSpec((nseq, CHUNK, GDN_QK), lambda b, c: (b, c, part))

    return pl.pallas_call(
        functools.partial(_gdn_chunk_kernel, nseq=nseq),
        grid=(bsz // nseq, nch),
        in_specs=[qkv_spec(0), qkv_spec(1), qkv_spec(2),
                  pl.BlockSpec((nseq, CHUNK, D_GDN), lambda b, c: (b, c, 0)),
                  pl.BlockSpec((nseq, CHUNK, LANES), lambda b, c: (b, c, 0)),
                  pl.BlockSpec((nseq, CHUNK, LANES), lambda b, c: (b, c, 0)),
                  pl.BlockSpec((nseq, 1, CHUNK, LANES), lambda b, c: (b, c, 0, 0)),
                  pl.BlockSpec((1, GDN_DK), lambda b, c: (0, 0))],
        out_specs=[pl.BlockSpec((nseq, CHUNK, D_GDN), lambda b, c: (b, c, 0)),
                   pl.BlockSpec((nseq, GDN_HEADS, GDN_DK, GDN_DK), lambda b, c: (b, 0, 0, 0))],
        out_shape=[jax.ShapeDtypeStruct((bsz, t_pad, D_GDN), ACT_DTYPE),
                   jax.ShapeDtypeStruct((bsz, GDN_HEADS, GDN_DK, GDN_DK), F32)],
        scratch_shapes=[pltpu.VMEM((nseq * GDN_HEADS, GDN_DK, GDN_DK), F32)],
        compiler_params=_cparams("arbitrary", "arbitrary"),
        name="gdn_chunk",
    )(qkv_act, qkv_act, qkv_act, z_act, feat, cs, cst, norm_w.reshape(1, GDN_DK))


def _ssd_finish(y, zs, nw):
    yg = y * zs
    gw = D_SSM // SSM_GROUPS
    parts = []
    for g in range(SSM_GROUPS):
        blk = yg[:, g * gw:(g + 1) * gw]
        parts.append(blk * lax.rsqrt(jnp.mean(blk * blk, -1, keepdims=True) + EPS))
    return jnp.concatenate(parts, axis=1) * nw


def _ssd_chunk_kernel(x_ref, b_ref, c_ref, zs_ref, featt_ref, cs_ref, cst_ref, d_ref, nw_ref,
                      y_ref, sfin_ref, s_ref, *, nseq):
    ch = pl.program_id(1)

    @pl.when(ch == 0)
    def _():
        s_ref[...] = jnp.zeros(s_ref.shape, F32)

    row = lax.broadcasted_iota(jnp.int32, (CHUNK, CHUNK), 0)
    col = lax.broadcasted_iota(jnp.int32, (CHUNK, CHUNK), 1)
    upper = col >= row
    top = row < SSM_HEAD_DIM
    heads_per_group = SSM_HEADS // SSM_GROUPS
    npair = SSM_HEADS // 2
    ecs, edl = [], []
    for j in range(nseq):
        cst = cst_ref[j, 0]
        ecs.append(jnp.exp(cst))
        edl.append(jnp.exp(cst[:, CHUNK - 1:CHUNK] - cst))
    grp_slices = [slice(g * SSM_STATE, (g + 1) * SSM_STATE) for g in range(SSM_GROUPS)]
    cbt = [[_dot_nt(b_ref[j, :, gs], c_ref[j, :, gs]) for gs in grp_slices] for j in range(nseq)]
    chains = [(j, p) for j in range(nseq) for p in range(npair)]

    def grp_of(p):
        return grp_slices[(2 * p) // heads_per_group]

    def rows2(tile, base, p):
        return jnp.where(top, tile[base + 2 * p:base + 2 * p + 1, :],
                         tile[base + 2 * p + 1:base + 2 * p + 2, :])

    def decay_t(j, h):
        ac_row = cst_ref[j, 0, FEAT_LA + h:FEAT_LA + h + 1, :]
        ac_col = cs_ref[j, :, FEAT_LA + h:FEAT_LA + h + 1]
        return jnp.exp(jnp.where(upper, ac_row - ac_col, NEG_BIG))

    xt = [x_ref[j, :, p * LANES:(p + 1) * LANES].astype(F32).T for j, p in chains]
    xdt = [x * rows2(featt_ref[j, 0], FEAT_DT, p) for x, (j, p) in zip(xt, chains)]
    st = [jnp.concatenate([cbt[j][(2 * p) // heads_per_group] * decay_t(j, 2 * p),
                           cbt[j][(2 * p) // heads_per_group] * decay_t(j, 2 * p + 1)], axis=0)
          for j, p in chains]
    lhs = [jnp.concatenate([jnp.where(top, x, 0.0), jnp.where(top, 0.0, x)], axis=1) for x in xdt]
    y_diag = [_dot(l, r) for l, r in zip(lhs, st)]
    s = [s_ref[i * LANES:(i + 1) * LANES, :] for i in range(len(chains))]
    y_off = [_dot_nt(z, c_ref[j, :, grp_of(p)]) * rows2(ecs[j], FEAT_LA, p)
             for z, (j, p) in zip(s, chains)]
    ds = [_dot(x * rows2(edl[j], FEAT_LA, p), b_ref[j, :, grp_of(p)])
          for x, (j, p) in zip(xdt, chains)]
    for i, (j, p) in enumerate(chains):
        last = rows2(ecs[j], FEAT_LA, p)[:, CHUNK - 1:CHUNK]
        s_ref[i * LANES:(i + 1) * LANES, :] = s[i] * last + ds[i]
    yt = [a + b + d_ref[p * LANES:(p + 1) * LANES, :] * x
          for a, b, x, (j, p) in zip(y_diag, y_off, xt, chains)]
    for j in range(nseq):
        y = jnp.concatenate([yt[j * npair + p].T for p in range(npair)], axis=1)
        y_ref[j] = _ssd_finish(y, zs_ref[j].astype(F32), nw_ref[...]).astype(y_ref.dtype)

    @pl.when(ch == pl.num_programs(1) - 1)
    def _():
        rows = SSM_HEADS * SSM_HEAD_DIM
        for j in range(nseq):
            sfin_ref[j] = s_ref[j * rows:(j + 1) * rows, :]


SSD_SEQ_PER_STEP = 2


def _ssd_chunk(xbc_act, z_act, featt, cs, cst, d_rows, norm_w):
    bsz, t_pad, _ = xbc_act.shape
    nch = t_pad // CHUNK
    bc_w = SSM_GROUPS * SSM_STATE
    nseq = SSD_SEQ_PER_STEP if bsz % SSD_SEQ_PER_STEP == 0 else 1
    rows = SSM_HEADS * SSM_HEAD_DIM
    return pl.pallas_call(
        functools.partial(_ssd_chunk_kernel, nseq=nseq),
        grid=(bsz // nseq, nch),
        in_specs=[pl.BlockSpec((nseq, CHUNK, D_SSM), lambda b, c: (b, c, 0)),
                  pl.BlockSpec((nseq, CHUNK, bc_w), lambda b, c: (b, c, D_SSM // bc_w)),
                  pl.BlockSpec((nseq, CHUNK, bc_w), lambda b, c: (b, c, D_SSM // bc_w + 1)),
                  pl.BlockSpec((nseq, CHUNK, D_SSM), lambda b, c: (b, c, 1)),
                  pl.BlockSpec((nseq, 1, CHUNK, LANES), lambda b, c: (b, c, 0, 0)),
                  pl.BlockSpec((nseq, CHUNK, LANES), lambda b, c: (b, c, 0)),
                  pl.BlockSpec((nseq, 1, CHUNK, LANES), lambda b, c: (b, c, 0, 0)),
                  pl.BlockSpec((rows, LANES), lambda b, c: (0, 0)),
                  pl.BlockSpec((1, D_SSM), lambda b, c: (0, 0))],
        out_specs=[pl.BlockSpec((nseq, CHUNK, D_SSM), lambda b, c: (b, c, 0)),
                   pl.BlockSpec((nseq, rows, SSM_STATE), lambda b, c: (b, 0, 0))],
        out_shape=[jax.ShapeDtypeStruct((bsz, t_pad, D_SSM), ACT_DTYPE),
                   jax.ShapeDtypeStruct((bsz, rows, SSM_STATE), F32)],
        scratch_shapes=[pltpu.VMEM((nseq * rows, SSM_STATE), F32)],
        compiler_params=_cparams("arbitrary", "arbitrary"),
        name="ssd_chunk",
    )(xbc_act, xbc_act, xbc_act, z_act, featt, cs, cst, d_rows, norm_w.reshape(1, D_SSM))


def _out_proj_kernel(mg_ref, ms_ref, x_ref, w_ref, nw_ref, xo_ref, ho_ref, *, seq_rows):
    acc = (jnp.dot(mg_ref[...].astype(BF16), w_ref[0:D_GDN, :], preferred_element_type=F32)
           + jnp.dot(ms_ref[...].astype(BF16), w_ref[D_GDN:, :], preferred_element_type=F32))
    xn = x_ref[...] + acc
    if seq_rows:
        tile = xn.shape[0]
        start = lax.rem(pl.program_id(0) * tile, seq_rows)
        r = start + lax.broadcasted_iota(jnp.int32, xn.shape, 0)
        pad = (r < FRONT_PAD) | ((r >= seq_rows) & (r < seq_rows + FRONT_PAD))
        xn = jnp.where(pad, 0.0, xn)
    xo_ref[...] = xn
    hn = xn * lax.rsqrt(jnp.mean(xn * xn, -1, keepdims=True) + EPS) * nw_ref[...]
    ho_ref[...] = hn.astype(ho_ref.dtype)


def _out_proj(mix_g, mix_s, x, w_out, next_norm_w, h_dtype, tile, seq_rows):
    rows = x.shape[0]
    row_spec = pl.BlockSpec((tile, D_MODEL), lambda i: (i, 0))
    return pl.pallas_call(
        functools.partial(_out_proj_kernel, seq_rows=seq_rows),
        grid=(rows // tile,),
        in_specs=[row_spec, row_spec, row_spec,
                  pl.BlockSpec((D_GDN + D_SSM, D_MODEL), lambda i: (0, 0)),
                  pl.BlockSpec((1, D_MODEL), lambda i: (0, 0))],
        out_specs=[row_spec, row_spec],
        out_shape=[jax.ShapeDtypeStruct((rows, D_MODEL), F32),
                   jax.ShapeDtypeStruct((rows, D_MODEL), h_dtype)],
        compiler_params=_cparams("arbitrary"),
        name="out_proj",
    )(mix_g, mix_s, x, w_out, next_norm_w.reshape(1, D_MODEL))


def _proj_conv_decode_kernel(h_ref, w_ref, buf_ref, cw_ref, cb_ref, out_ref, nbuf_ref):
    z = jnp.dot(h_ref[...], w_ref[...], preferred_element_type=F32)
    cw = cw_ref[...]
    acc = cb_ref[...] + cw[3:4] * z
    for tap in range(CONV_W - 1):
        acc = acc + cw[tap:tap + 1] * buf_ref[tap]
    out_ref[...] = _silu_of_twice(acc)
    nbuf_ref[0] = buf_ref[1]
    nbuf_ref[1] = buf_ref[2]
    nbuf_ref[2] = z


def _proj_silu_decode_kernel(h_ref, w_ref, out_ref):
    out_ref[...] = _silu_of_twice(jnp.dot(h_ref[...], w_ref[...], preferred_element_type=F32))


def _proj_conv_decode(h, w, buf, cw, cb):
    rows = h.shape[0]
    ncols = w.shape[1]
    return pl.pallas_call(
        _proj_conv_decode_kernel,
        grid=(ncols // COL_BLOCK,),
        in_specs=[pl.BlockSpec((rows, D_MODEL), lambda j: (0, 0)),
                  pl.BlockSpec((D_MODEL, COL_BLOCK), lambda j: (0, j)),
                  pl.BlockSpec((CONV_W - 1, rows, COL_BLOCK), lambda j: (0, 0, j)),
                  pl.BlockSpec((CONV_W, COL_BLOCK), lambda j: (0, j)),
                  pl.BlockSpec((1, COL_BLOCK), lambda j: (0, j))],
        out_specs=[pl.BlockSpec((rows, COL_BLOCK), lambda j: (0, j)),
                   pl.BlockSpec((CONV_W - 1, rows, COL_BLOCK), lambda j: (0, 0, j))],
        out_shape=[jax.ShapeDtypeStruct((rows, ncols), F32),
                   jax.ShapeDtypeStruct((CONV_W - 1, rows, ncols), F32)],
        compiler_params=_cparams("arbitrary"),
        name="proj_conv_decode",
    )(h, w, buf, cw, cb.reshape(1, ncols))


def _proj_silu_decode(h, w):
    rows = h.shape[0]
    ncols = w.shape[1]
    return pl.pallas_call(
        _proj_silu_decode_kernel,
        grid=(ncols // COL_BLOCK,),
        in_specs=[pl.BlockSpec((rows, D_MODEL), lambda j: (0, 0)),
                  pl.BlockSpec((D_MODEL, COL_BLOCK), lambda j: (0, j))],
        out_specs=pl.BlockSpec((rows, COL_BLOCK), lambda j: (0, j)),
        out_shape=jax.ShapeDtypeStruct((rows, ncols), F32),
        compiler_params=_cparams("arbitrary"),
        name="proj_silu_decode",
    )(h, w)


TOK_BLOCK = 8


def _gdn_step_kernel(qt_ref, kt_ref, v_ref, zg_ref, feat_ref, nw_ref, s_ref, *rest):
    o_ref, so_ref = rest[-2:]
    feat = feat_ref[...]
    vrows = v_ref[...]
    for h in range(GDN_HEADS):
        sl = slice(h * GDN_DK, (h + 1) * GDN_DK)
        qt = qt_ref[0, h]
        kt = kt_ref[0, h]
        qt = qt * (lax.rsqrt(jnp.sum(qt * qt, 0, keepdims=True) + EPS) * (GDN_DK ** -0.5))
        kt = kt * lax.rsqrt(jnp.sum(kt * kt, 0, keepdims=True) + EPS)
        for j in range(TOK_BLOCK):
            kc = kt[:, j:j + 1]
            qc = qt[:, j:j + 1]
            beta = feat[j:j + 1, FEAT_BETA + h:FEAT_BETA + h + 1]
            g = feat[j:j + 1, FEAT_G + h:FEAT_G + h + 1]
            s = s_ref[j, h] * jnp.exp(g)
            kv = jnp.sum(s * kc, 0, keepdims=True)
            d = (vrows[j:j + 1, sl] - kv) * beta
            s = s + kc * d
            so_ref[j, h] = s
            o_ref[j:j + 1, sl] = jnp.sum(s * qc, 0, keepdims=True)
    nw = nw_ref[...]
    for h in range(GDN_HEADS):
        sl = slice(h * GDN_DK, (h + 1) * GDN_DK)
        o = o_ref[:, sl]
        o_ref[:, sl] = o * lax.rsqrt(jnp.mean(o * o, -1, keepdims=True) + EPS) * nw * zg_ref[:, sl]


def _layer_state_call(kernel_fn, name, layer, states, acc, in_specs, out_specs, out_shapes, args):
    rows = states.shape[1]
    blk = (None, TOK_BLOCK) + states.shape[2:]
    zeros = (0,) * (states.ndim - 2)
    state_spec = pl.BlockSpec(blk, lambda i: (layer, i) + zeros)
    in_specs = in_specs + [state_spec]
    args = args + [states]
    aliases = {}
    if acc is not None:
        in_specs = in_specs + [pl.BlockSpec(memory_space=pl.ANY)]
        args = args + [acc]
        aliases = {len(args) - 1: len(out_shapes)}
    return pl.pallas_call(
        kernel_fn,
        grid=(rows // TOK_BLOCK,),
        in_specs=in_specs,
        out_specs=out_specs + [state_spec],
        out_shape=out_shapes + [jax.ShapeDtypeStruct(states.shape, F32)],
        input_output_aliases=aliases,
        compiler_params=_cparams("arbitrary"),
        name=name,
    )(*args)


def _gdn_step(layer, qt, kt, qkv_act, z_act, feat, norm_w, states, acc):
    rows = qkv_act.shape[0]
    return _layer_state_call(
        _gdn_step_kernel, "gdn_step", layer, states, acc,
        [pl.BlockSpec((1, GDN_HEADS, GDN_DK, TOK_BLOCK), lambda i: (i, 0, 0, 0)),
         pl.BlockSpec((1, GDN_HEADS, GDN_DK, TOK_BLOCK), lambda i: (i, 0, 0, 0)),
         pl.BlockSpec((TOK_BLOCK, D_GDN), lambda i: (i, 2)),
         pl.BlockSpec((TOK_BLOCK, D_GDN), lambda i: (i, 0)),
         pl.BlockSpec((TOK_BLOCK, LANES), lambda i: (i, 0)),
         pl.BlockSpec((1, GDN_DK), lambda i: (0, 0))],
        [pl.BlockSpec((TOK_BLOCK, D_GDN), lambda i: (i, 0))],
        [jax.ShapeDtypeStruct((rows, D_GDN), F32)],
        [qt, kt, qkv_act, z_act, feat, norm_w.reshape(1, GDN_DK)])


SSD_STEP_HEADS = 4


def _ssd_step_kernel(xt_ref, b_ref, c_ref, feat_ref, s_ref, *rest):
    yt_ref, so_ref = rest[-2:]
    feat = feat_ref[...]
    heads_per_group = SSM_HEADS // SSM_GROUPS
    ela = jnp.exp(feat)
    for h0 in range(0, SSM_HEADS, SSD_STEP_HEADS):
        hj = [(h, j) for h in range(h0, h0 + SSD_STEP_HEADS) for j in range(TOK_BLOCK)]

        def grp_lanes(h):
            grp = h // heads_per_group
            return slice(grp * SSM_STATE, (grp + 1) * SSM_STATE)

        xdt = [xt_ref[0, h, :, j:j + 1] * feat[j:j + 1, FEAT_DT + h:FEAT_DT + h + 1] for h, j in hj]
        xb = [jnp.broadcast_to(x, (SSM_HEAD_DIM, SSM_STATE)) for x in xdt]
        s = [s_ref[j, h] * ela[j:j + 1, FEAT_LA + h:FEAT_LA + h + 1] + x * b_ref[j:j + 1, grp_lanes(h)]
             for (h, j), x in zip(hj, xb)]
        for (h, j), x in zip(hj, s):
            so_ref[j, h] = x
        y = [jnp.sum(x * c_ref[j:j + 1, grp_lanes(h)], -1, keepdims=True) for (h, j), x in zip(hj, s)]
        for i in range(SSD_STEP_HEADS):
            yt_ref[0, h0 + i] = jnp.concatenate(y[i * TOK_BLOCK:(i + 1) * TOK_BLOCK], axis=1)


def _ssd_step(layer, xt, xbc_act, feat, states, acc):
    rows = xbc_act.shape[0]
    ngrp = rows // TOK_BLOCK
    bc_w = SSM_GROUPS * SSM_STATE
    return _layer_state_call(
        _ssd_step_kernel, "ssd_step", layer, states, acc,
        [pl.BlockSpec((1, SSM_HEADS, SSM_HEAD_DIM, TOK_BLOCK), lambda i: (i, 0, 0, 0)),
         pl.BlockSpec((TOK_BLOCK, bc_w), lambda i: (i, D_SSM // bc_w)),
         pl.BlockSpec((TOK_BLOCK, bc_w), lambda i: (i, D_SSM // bc_w + 1)),
         pl.BlockSpec((TOK_BLOCK, LANES), lambda i: (i, 0))],
        [pl.BlockSpec((1, SSM_HEADS, SSM_HEAD_DIM, TOK_BLOCK), lambda i: (i, 0, 0, 0))],
        [jax.ShapeDtypeStruct((ngrp, SSM_HEADS, SSM_HEAD_DIM, TOK_BLOCK), F32)],
        [xt, xbc_act, xbc_act, feat])


def _ssd_finish_decode_kernel(y_ref, x_ref, zs_ref, d_ref, nw_ref, out_ref):
    out_ref[...] = _ssd_finish(y_ref[...] + d_ref[...] * x_ref[...], zs_ref[...], nw_ref[...])


def _ssd_finish_decode(y, xbc_act, z_act, d_cols, norm_w):
    rows = y.shape[0]
    return pl.pallas_call(
        _ssd_finish_decode_kernel,
        grid=(1,),
        in_specs=[pl.BlockSpec((rows, D_SSM), lambda i: (0, 0)),
                  pl.BlockSpec((rows, D_SSM), lambda i: (0, 0)),
                  pl.BlockSpec((rows, D_SSM), lambda i: (0, 1)),
                  pl.BlockSpec((1, D_SSM), lambda i: (0, 0)),
                  pl.BlockSpec((1, D_SSM), lambda i: (0, 0))],
        out_specs=pl.BlockSpec((rows, D_SSM), lambda i: (0, 0)),
        out_shape=jax.ShapeDtypeStruct((rows, D_SSM), F32),
        compiler_params=_cparams("arbitrary"),
        name="ssd_finish_decode",
    )(y, xbc_act, z_act, d_cols, norm_w.reshape(1, D_SSM))


def _prepare_weights(w_in, gdn_A_log, gdn_dt_bias, ssm_A_log, ssm_dt_bias, ssm_D, w_out):
    depth = w_in.shape[0]
    w_qkv = w_in[..., :OFF_Z_G].astype(BF16)
    w_xbc = w_in[..., OFF_XBC:OFF_Z_S].astype(BF16)
    w_z = (0.5 * jnp.concatenate([w_in[..., OFF_Z_G:OFF_B], w_in[..., OFF_Z_S:OFF_DT]], axis=-1)
           ).astype(BF16)
    w_dt = w_in[..., OFF_DT:]
    w_small = jnp.concatenate(
        [w_in[..., OFF_B:OFF_XBC], w_dt, w_dt, jnp.zeros((depth, D_MODEL, LANES - FEAT_END), F32)],
        axis=-1).astype(BF16)
    zeros8 = jnp.zeros((depth, GDN_HEADS), F32)
    tail = jnp.zeros((depth, LANES - FEAT_END), F32)
    p_bias = jnp.concatenate([zeros8, gdn_dt_bias.astype(F32), ssm_dt_bias.astype(F32),
                              ssm_dt_bias.astype(F32), tail], axis=-1).reshape(depth, 1, LANES)
    p_scale = jnp.concatenate([zeros8, -jnp.exp(gdn_A_log.astype(F32)),
                               jnp.ones((depth, SSM_HEADS), F32), -jnp.exp(ssm_A_log.astype(F32)),
                               tail], axis=-1).reshape(depth, 1, LANES)
    d_cols = jnp.repeat(ssm_D.astype(F32), SSM_HEAD_DIM, axis=-1).reshape(depth, 1, D_SSM)
    return w_qkv, w_xbc, w_z, w_small, p_bias, p_scale, d_cols, w_out.astype(BF16)


def kernel(x_prompt, x_sample, state_gdn, state_gdn_conv, state_ssm, state_ssm_conv, meta_tokens,
           norm_w, w_in, gdn_conv_w, gdn_A_log, gdn_dt_bias, gdn_norm_w, ssm_conv_w, ssm_conv_b,
           ssm_A_log, ssm_dt_bias, ssm_D, ssm_norm_w, w_out, final_norm_w):
    bp, seq, _ = x_prompt.shape
    bd = x_sample.shape[0]
    t_pad = FRONT_PAD + N_META + seq
    xp, hp = _embed_prompt(x_prompt.astype(F32), meta_tokens.astype(F32), norm_w[0].astype(F32))
    xs = x_sample.astype(F32).reshape(bd, D_MODEL)
    hs = _rmsnorm(xs, norm_w[0], bd)
    zero_bias = jnp.zeros((GDN_CONV_DIM,), F32)
    ngrp = bd // TOK_BLOCK
    p_g, p_gc, p_s, p_sc = [], [], [], []
    d_g, d_s = None, None
    d_gc, d_sc = [], []
    prepared = _prepare_weights(w_in, gdn_A_log, gdn_dt_bias, ssm_A_log, ssm_dt_bias, ssm_D, w_out)
    for l in range(DEPTH):
        w_qkv, w_xbc, w_z, w_small, p_bias, p_scale, d_cols, w_o = [a[l] for a in prepared]
        last = l == DEPTH - 1
        nxt_w = final_norm_w if last else norm_w[l + 1]
        h_dtype = F32 if last else BF16
        gcw = 0.5 * gdn_conv_w[l].astype(F32)
        scw = 0.5 * ssm_conv_w[l].astype(F32)
        scb = 0.5 * ssm_conv_b[l].astype(F32)

        h3 = hp.reshape(bp, t_pad, D_MODEL)
        qkv_act, qkv_tail = _proj_conv(h3, w_qkv, gcw, None)
        xbc_act, xbc_tail = _proj_conv(h3, w_xbc, scw, scb)
        z_act = _proj_silu(h3, w_z)
        feat, featt, cs, cst = _feat_prompt(h3, w_small, p_bias, p_scale)
        d_rows = jnp.broadcast_to(d_cols.reshape(D_SSM, 1), (D_SSM, LANES))
        mix_g, sg = _gdn_chunk(qkv_act, z_act, feat, cs, cst, gdn_norm_w[l].astype(F32))
        mix_s, ss = _ssd_chunk(xbc_act, z_act, featt, cs, cst, d_rows, ssm_norm_w[l].astype(F32))
        xp, hp = _out_proj(mix_g.reshape(bp * t_pad, D_GDN), mix_s.reshape(bp * t_pad, D_SSM), xp,
                           w_o, nxt_w.astype(F32), h_dtype, ROW_TILE, t_pad)
        p_g.append(sg)
        p_gc.append(qkv_tail[:, SUBLANES - (CONV_W - 1):, :])
        p_s.append(ss.reshape(bp, SSM_HEADS, SSM_HEAD_DIM, SSM_STATE))
        p_sc.append(xbc_tail[:, SUBLANES - (CONV_W - 1):, :])

        gbuf = jnp.transpose(state_gdn_conv[l].astype(F32), (1, 0, 2))
        sbuf = jnp.transpose(state_ssm_conv[l].astype(F32), (1, 0, 2))
        qkv_d, gbuf_n = _proj_conv_decode(hs, w_qkv, gbuf, gcw, zero_bias)
        xbc_d, sbuf_n = _proj_conv_decode(hs, w_xbc, sbuf, scw, scb)
        z_d = _proj_silu_decode(hs, w_z)
        feat_d = _feat_decode(hs, w_small, p_bias, p_scale)

        def head_major(a, heads, dim):
            return jnp.transpose(a.reshape(ngrp, TOK_BLOCK, heads, dim), (0, 2, 3, 1))

        qt = head_major(qkv_d[:, :GDN_QK], GDN_HEADS, GDN_DK)
        kt = head_major(qkv_d[:, GDN_QK:2 * GDN_QK], GDN_HEADS, GDN_DK)
        mix_gd, d_g = _gdn_step(l, qt, kt, qkv_d, z_d, feat_d, gdn_norm_w[l].astype(F32),
                                state_gdn.astype(F32), d_g)
        xt = head_major(xbc_d[:, :D_SSM], SSM_HEADS, SSM_HEAD_DIM)
        yt, d_s = _ssd_step(l, xt, xbc_d, feat_d, state_ssm.astype(F32), d_s)
        y_d = jnp.transpose(yt, (0, 3, 1, 2)).reshape(bd, D_SSM)
        mix_sd = _ssd_finish_decode(y_d, xbc_d, z_d, d_cols, ssm_norm_w[l].astype(F32))
        xs, hs = _out_proj(mix_gd, mix_sd, xs, w_o, nxt_w.astype(F32), h_dtype, bd, 0)
        d_gc.append(jnp.transpose(gbuf_n, (1, 0, 2)))
        d_sc.append(jnp.transpose(sbuf_n, (1, 0, 2)))

    y_prompt = hp.reshape(bp, t_pad, D_MODEL)[:, FRONT_PAD + N_META:]
    y_sample = hs.reshape(bd, 1, D_MODEL)
    return (y_prompt, y_sample,
            jnp.stack(p_g), jnp.stack(p_gc), jnp.stack(p_s), jnp.stack(p_sc),
            d_g, jnp.stack(d_gc), d_s, jnp.stack(d_sc))
```

```python
import functools

import jax
import jax.numpy as jnp
from jax import lax
from jax.experimental import pallas as pl
from jax.experimental.pallas import tpu as pltpu

F32 = jnp.float32
BF16 = jnp.bfloat16

D_MODEL = 1024
DEPTH = 4
N_META = 16
GDN_HEADS = 8
GDN_DK = 128
GDN_QK = 1024
D_GDN = 1024
GDN_CONV_DIM = 3072
D_SSM = 1024
SSM_HEADS = 16
SSM_HEAD_DIM = 64
SSM_GROUPS = 2
SSM_STATE = 128
SSM_CONV_DIM = 1536
CONV_W = 4
EPS = 1e-6
OFF_Z_G = GDN_CONV_DIM
OFF_B = OFF_Z_G + D_GDN
OFF_A = OFF_B + GDN_HEADS
OFF_XBC = OFF_A + GDN_HEADS
OFF_Z_S = OFF_XBC + SSM_CONV_DIM
OFF_DT = OFF_Z_S + D_SSM

CHUNK = 128
HALF = CHUNK // 2
FRONT_PAD = 112
SUBLANES = 8
LANES = 128
COL_BLOCK = 512
ROW_TILE = 512
NEG_BIG = -1e30
ACT_DTYPE = BF16
VMEM_LIMIT = 56 * 1024 * 1024

FEAT_BETA = 0
FEAT_G = 8
FEAT_DT = 16
FEAT_LA = 32
FEAT_END = 48


def _cparams(*sem, flags=None):
    return pltpu.CompilerParams(dimension_semantics=sem, vmem_limit_bytes=VMEM_LIMIT, flags=flags)


def _dot(a, b):
    return jnp.dot(a.astype(BF16), b.astype(BF16), preferred_element_type=F32)


def _dot_nt(a, b):
    return lax.dot_general(a.astype(BF16), b.astype(BF16), (((1,), (1,)), ((), ())),
                           preferred_element_type=F32)


def _dot_tn(a, b):
    return lax.dot_general(a.astype(BF16), b.astype(BF16), (((0,), (0,)), ((), ())),
                           preferred_element_type=F32)


def _silu_of_twice(h):
    return h * jnp.tanh(h) + h


def _softplus(x):
    return jnp.maximum(x, 0.0) + jnp.log1p(jnp.exp(-jnp.abs(x)))


def _rmsnorm_kernel(x_ref, w_ref, h_ref):
    x = x_ref[...]
    y = x * lax.rsqrt(jnp.mean(x * x, -1, keepdims=True) + EPS) * w_ref[...]
    h_ref[...] = y.astype(h_ref.dtype)


def _rmsnorm(x, w, tile):
    rows = x.shape[0]
    return pl.pallas_call(
        _rmsnorm_kernel,
        grid=(rows // tile,),
        in_specs=[pl.BlockSpec((tile, D_MODEL), lambda i: (i, 0)),
                  pl.BlockSpec((1, D_MODEL), lambda i: (0, 0))],
        out_specs=pl.BlockSpec((tile, D_MODEL), lambda i: (i, 0)),
        out_shape=jax.ShapeDtypeStruct((rows, D_MODEL), BF16),
        compiler_params=_cparams("arbitrary"),
        name="rmsnorm",
    )(x, w.reshape(1, D_MODEL))


def _embed_prompt_kernel(x_ref, meta_ref, w_ref, xo_ref, ho_ref):
    w = w_ref[...]

    def emit(row0, x):
        xo_ref[row0:row0 + CHUNK, :] = x
        y = x * lax.rsqrt(jnp.mean(x * x, -1, keepdims=True) + EPS) * w
        ho_ref[row0:row0 + CHUNK, :] = y.astype(ho_ref.dtype)

    emit(0, jnp.concatenate([jnp.zeros((FRONT_PAD, D_MODEL), F32), meta_ref[...]], axis=0))
    for c in range(x_ref.shape[1] // CHUNK):
        emit((c + 1) * CHUNK, x_ref[0, c * CHUNK:(c + 1) * CHUNK, :])


def _embed_prompt(x_prompt, meta_tokens, w):
    bp, seq, _ = x_prompt.shape
    t_pad = FRONT_PAD + N_META + seq
    row_spec = pl.BlockSpec((t_pad, D_MODEL), lambda b: (b, 0))
    return pl.pallas_call(
        _embed_prompt_kernel,
        grid=(bp,),
        in_specs=[pl.BlockSpec((1, seq, D_MODEL), lambda b: (b, 0, 0)),
                  pl.BlockSpec((N_META, D_MODEL), lambda b: (0, 0)),
                  pl.BlockSpec((1, D_MODEL), lambda b: (0, 0))],
        out_specs=[row_spec, row_spec],
        out_shape=[jax.ShapeDtypeStruct((bp * t_pad, D_MODEL), F32),
                   jax.ShapeDtypeStruct((bp * t_pad, D_MODEL), BF16)],
        compiler_params=_cparams("arbitrary"),
        name="embed_prompt",
    )(x_prompt, meta_tokens, w.reshape(1, D_MODEL))


PROJ_ROW_BLOCKS = 8
PROJ_LOOKAHEAD = 1


def _proj_conv_kernel(h_ref, w_ref, cw_ref, cb_ref, out_ref, tail_ref, *scratch, t_pad, bias):
    z_refs, y_ref = scratch[:-1], scratch[-1]
    ncol = out_ref.shape[2]
    ntile = ncol // LANES
    rb = t_pad // PROJ_ROW_BLOCKS
    stride = rb // SUBLANES
    cw = cw_ref[...]
    cb = cb_ref[...]
    first_row = lax.broadcasted_iota(jnp.int32, (SUBLANES, LANES), 0) == 0
    zero = jnp.zeros((SUBLANES, LANES), F32)
    dyn_zero = jnp.minimum(pl.program_id(0), 0)

    def matmul(r):
        z = jnp.dot(h_ref[0, r * rb:(r + 1) * rb, :], w_ref[...], preferred_element_type=F32)
        for c in range(ntile):
            z_refs[r][c] = z[:, c * LANES:(c + 1) * LANES]

    def shift_in(x, carry):
        return jnp.where(first_row, carry, pltpu.roll(x, 1, 0))

    def conv(r):
        for c in range(ntile):
            taps = [cw[j:j + 1, c * LANES:(c + 1) * LANES] for j in range(CONV_W)]
            carries = [zero] * (CONV_W - 1)
            if r > 0:
                hist = z_refs[r - 1][c, rb - SUBLANES:rb, :]
                part = taps[0] * hist
                for tap in range(1, CONV_W):
                    carries[tap - 1] = pltpu.roll(part, 1, 0)
                    part = taps[tap] * hist + carries[tap - 1]
            v = [z_refs[r][c, pl.ds(dyn_zero + i, SUBLANES, stride=stride), :] for i in range(stride)]
            acc = [taps[0] * x for x in v]
            for tap in range(1, CONV_W):
                wrapped = shift_in(acc[stride - 1], carries[tap - 1])
                acc = [taps[tap] * x + (wrapped if i == 0 else acc[i - 1]) for i, x in enumerate(v)]
            for i, y in enumerate(acc):
                if bias:
                    y = y + cb[:, c * LANES:(c + 1) * LANES]
                y_ref[c, pl.ds(i, SUBLANES, stride=stride), :] = _silu_of_twice(y)
        for c in range(ntile):
            out_ref[0, r * rb:(r + 1) * rb, c * LANES:(c + 1) * LANES] = y_ref[c].astype(out_ref.dtype)

    for r in range(PROJ_ROW_BLOCKS + PROJ_LOOKAHEAD):
        if r < PROJ_ROW_BLOCKS:
            matmul(r)
        if r >= PROJ_LOOKAHEAD:
            conv(r - PROJ_LOOKAHEAD)
    for c in range(ntile):
        tail_ref[0, :, c * LANES:(c + 1) * LANES] = z_refs[PROJ_ROW_BLOCKS - 1][c, rb - SUBLANES:rb, :]


def _proj_silu_kernel(h_ref, w_ref, out_ref):
    out_ref[0] = _silu_of_twice(
        jnp.dot(h_ref[0], w_ref[...], preferred_element_type=F32)).astype(out_ref.dtype)


def _proj_conv(h3, w, cw, cb):
    bsz, t_pad, _ = h3.shape
    ncols = w.shape[1]
    bias = cb is not None
    if not bias:
        cb = jnp.zeros((ncols,), F32)
    return pl.pallas_call(
        functools.partial(_proj_conv_kernel, t_pad=t_pad, bias=bias),
        grid=(bsz, ncols // COL_BLOCK),
        in_specs=[pl.BlockSpec((1, t_pad, D_MODEL), lambda b, j: (b, 0, 0)),
                  pl.BlockSpec((D_MODEL, COL_BLOCK), lambda b, j: (0, j)),
                  pl.BlockSpec((CONV_W, COL_BLOCK), lambda b, j: (0, j)),
                  pl.BlockSpec((1, COL_BLOCK), lambda b, j: (0, j))],
        out_specs=[pl.BlockSpec((1, t_pad, COL_BLOCK), lambda b, j: (b, 0, j)),
                   pl.BlockSpec((1, SUBLANES, COL_BLOCK), lambda b, j: (b, 0, j))],
        out_shape=[jax.ShapeDtypeStruct((bsz, t_pad, ncols), ACT_DTYPE),
                   jax.ShapeDtypeStruct((bsz, SUBLANES, ncols), F32)],
        scratch_shapes=[pltpu.VMEM((COL_BLOCK // LANES, t_pad // PROJ_ROW_BLOCKS, LANES), F32)]
        * (PROJ_ROW_BLOCKS + 1),
        compiler_params=_cparams("arbitrary", "arbitrary"),
        name="proj_conv",
    )(h3, w, cw, cb.reshape(1, ncols))


def _proj_silu(h3, w):
    bsz, t_pad, _ = h3.shape
    ncols = w.shape[1]
    return pl.pallas_call(
        _proj_silu_kernel,
        grid=(bsz, ncols // COL_BLOCK),
        in_specs=[pl.BlockSpec((1, t_pad, D_MODEL), lambda b, j: (b, 0, 0)),
                  pl.BlockSpec((D_MODEL, COL_BLOCK), lambda b, j: (0, j))],
        out_specs=pl.BlockSpec((1, t_pad, COL_BLOCK), lambda b, j: (b, 0, j)),
        out_shape=jax.ShapeDtypeStruct((bsz, t_pad, ncols), ACT_DTYPE),
        compiler_params=_cparams("arbitrary", "arbitrary"),
        name="proj_silu",
    )(h3, w)


def _features(z, p_bias, p_scale):
    lane = lax.broadcasted_iota(jnp.int32, z.shape, 1)
    sp = p_scale * _softplus(z + p_bias)
    return jnp.where(lane < FEAT_G, jax.nn.sigmoid(z), jnp.where(lane < FEAT_END, sp, 0.0))


def _split3(x):
    x1 = x.astype(BF16)
    r1 = x - x1.astype(F32)
    x2 = r1.astype(BF16)
    x3 = (r1 - x2.astype(F32)).astype(BF16)
    return x1, x2, x3


def _feat_prompt_kernel(h_ref, w_ref, pb_ref, ps_ref, feat_ref, featt_ref, cs_ref, cst_ref, *,
                        t_pad):
    z = jnp.dot(h_ref[0], w_ref[...], preferred_element_type=F32)
    feat = _features(z, pb_ref[...], ps_ref[...])
    row = lax.broadcasted_iota(jnp.int32, feat.shape, 0)
    feat = jnp.where(row < FRONT_PAD, 0.0, feat)
    feat_ref[0] = feat
    r = lax.broadcasted_iota(jnp.int32, (CHUNK, CHUNK), 0)
    c = lax.broadcasted_iota(jnp.int32, (CHUNK, CHUNK), 1)
    tri = (r >= c).astype(BF16)
    for ch in range(t_pad // CHUNK):
        tile = feat[ch * CHUNK:(ch + 1) * CHUNK]
        featt_ref[0, ch] = tile.T
        x1, x2, x3 = _split3(tile)
        cs = (jnp.dot(tri, x1, preferred_element_type=F32)
              + jnp.dot(tri, x2, preferred_element_type=F32)
              + jnp.dot(tri, x3, preferred_element_type=F32))
        cs_ref[0, ch * CHUNK:(ch + 1) * CHUNK, :] = cs
        cst_ref[0, ch] = cs.T


def _feat_prompt(h3, w_small, p_bias, p_scale):
    bsz, t_pad, _ = h3.shape
    nch = t_pad // CHUNK
    return pl.pallas_call(
        functools.partial(_feat_prompt_kernel, t_pad=t_pad),
        grid=(bsz,),
        in_specs=[pl.BlockSpec((1, t_pad, D_MODEL), lambda b: (b, 0, 0)),
                  pl.BlockSpec((D_MODEL, LANES), lambda b: (0, 0)),
                  pl.BlockSpec((1, LANES), lambda b: (0, 0)),
                  pl.BlockSpec((1, LANES), lambda b: (0, 0))],
        out_specs=[pl.BlockSpec((1, t_pad, LANES), lambda b: (b, 0, 0)),
                   pl.BlockSpec((1, nch, CHUNK, LANES), lambda b: (b, 0, 0, 0)),
                   pl.BlockSpec((1, t_pad, LANES), lambda b: (b, 0, 0)),
                   pl.BlockSpec((1, nch, CHUNK, LANES), lambda b: (b, 0, 0, 0))],
        out_shape=[jax.ShapeDtypeStruct((bsz, t_pad, LANES), F32),
                   jax.ShapeDtypeStruct((bsz, nch, CHUNK, LANES), F32),
                   jax.ShapeDtypeStruct((bsz, t_pad, LANES), F32),
                   jax.ShapeDtypeStruct((bsz, nch, CHUNK, LANES), F32)],
        compiler_params=_cparams("arbitrary"),
        name="feat_prompt",
    )(h3, w_small, p_bias, p_scale)


def _feat_decode_kernel(h_ref, w_ref, pb_ref, ps_ref, feat_ref):
    z = jnp.dot(h_ref[...], w_ref[...], preferred_element_type=F32)
    feat_ref[...] = _features(z, pb_ref[...], ps_ref[...])


def _feat_decode(h, w_small, p_bias, p_scale):
    rows = h.shape[0]
    return pl.pallas_call(
        _feat_decode_kernel,
        out_shape=jax.ShapeDtypeStruct((rows, LANES), F32),
        compiler_params=pltpu.CompilerParams(vmem_limit_bytes=VMEM_LIMIT),
        name="feat_decode",
    )(h, w_small, p_bias, p_scale)


def _gdn_chunk_kernel(q_ref, k_ref, v_ref, zg_ref, feat_ref, cs_ref, cst_ref, nw_ref,
                      o_ref, sfin_ref, s_ref, cb_ref, cdec_ref, cgl_ref, *, nseq):
    ch = pl.program_id(1)

    @pl.when(ch == 0)
    def _():
        s_ref[...] = jnp.zeros(s_ref.shape, F32)
        cb_ref[...] = jnp.zeros(cb_ref.shape, BF16)
        cdec_ref[...] = jnp.zeros(cdec_ref.shape, F32)
        cgl_ref[...] = jnp.zeros(cgl_ref.shape, F32)

    row = lax.broadcasted_iota(jnp.int32, (CHUNK, CHUNK), 0)
    col = lax.broadcasted_iota(jnp.int32, (CHUNK, CHUNK), 1)
    incl = row >= col
    strict = row > col
    row_c = lax.broadcasted_iota(jnp.int32, (HALF, CHUNK), 0)
    col_c = lax.broadcasted_iota(jnp.int32, (HALF, CHUNK), 1)
    lo_c = col_c < HALF
    eye_c = (jnp.bitwise_and(col_c, HALF - 1) == row_c).astype(F32)
    zero_c = jnp.zeros((HALF, CHUNK), BF16)

    def bdiag(x):
        return jnp.concatenate([jnp.where(lo_c, x, zero_c), jnp.where(lo_c, zero_c, x)], axis=0)

    nw = nw_ref[...]
    chains = [(j, h) for j in range(nseq) for h in range(GDN_HEADS)]
    nchain = len(chains)
    Q, QD, K, KB, KD, VB, KBE = range(7)

    def lanes(h):
        return slice(h * GDN_DK, (h + 1) * GDN_DK)

    kbf = [cb_ref[K, i] for i in range(nchain)]
    decay = [cdec_ref[i] for i in range(nchain)]
    a = [jnp.where(strict, _dot_nt(cb_ref[KB, i], y) * d, 0.0)
         for i, (y, d) in enumerate(zip(kbf, decay))]
    p = [jnp.where(lo_c, -x[:HALF], -x[HALF:]) for x in a]
    inv = [eye_c + x for x in p]
    r_p = [bdiag(x.astype(BF16)) for x in p]
    for _ in range(5):
        p = [_dot(x, r) for x, r in zip(p, r_p)]
        r_p = [bdiag(x.astype(BF16)) for x in p]
        inv = [x + _dot(x, r) for x, r in zip(inv, r_p)]
    invb = [x.astype(BF16) for x in inv]
    r_a21 = [jnp.concatenate([zero_c, jnp.where(lo_c, x[HALF:].astype(BF16), zero_c)], axis=0)
             for x in a]
    fold = [_dot(x, r) for x, r in zip(invb, r_a21)]
    r_i11 = [jnp.concatenate([jnp.where(lo_c, x, zero_c), zero_c], axis=0) for x in invb]
    fold = [_dot(x, r) for x, r in zip(fold, r_i11)]
    t_inv = [jnp.concatenate([jnp.where(lo_c, x, 0.0), jnp.where(lo_c, -y, x)], axis=0)
             for x, y in zip(inv, fold)]
    uw = [_dot(t, jnp.concatenate([cb_ref[VB, i], cb_ref[KBE, i]], axis=1))
          for i, t in enumerate(t_inv)]
    attn = [_dot_nt(cb_ref[Q, i], y) * d for i, (y, d) in enumerate(zip(kbf, decay))]
    s = [s_ref[i] for i in range(nchain)]
    ws_qs = [_dot(jnp.concatenate([x[:, GDN_DK:].astype(BF16), cb_ref[QD, i]], axis=0), z)
             for i, (x, z) in enumerate(zip(uw, s))]
    v_new = [(x[:, :GDN_DK] - y[:CHUNK]).astype(BF16) for x, y in zip(uw, ws_qs)]
    o = [y[CHUNK:] + _dot(x, z) for y, x, z in zip(ws_qs, attn, v_new)]
    for i, (j, h) in enumerate(chains):
        s_ref[i] = s[i] * cgl_ref[i, 0:1, :] + _dot_tn(cb_ref[KD, i], v_new[i])
        on = o[i] * lax.rsqrt(jnp.mean(o[i] * o[i], -1, keepdims=True) + EPS)
        o_ref[j, :, lanes(h)] = (on * nw * zg_ref[j, :, lanes(h)].astype(F32)).astype(o_ref.dtype)

    def unit(x, scale):
        return x * (lax.rsqrt(jnp.sum(x * x, -1, keepdims=True) + EPS) * scale)

    for i, (j, h) in enumerate(chains):
        q = unit(q_ref[j, :, lanes(h)].astype(F32), GDN_DK ** -0.5)
        k = unit(k_ref[j, :, lanes(h)].astype(F32), 1.0)
        beta = feat_ref[j, :, FEAT_BETA + h:FEAT_BETA + h + 1]
        gc = cs_ref[j, :, FEAT_G + h:FEAT_G + h + 1]
        gr = cst_ref[j, 0, FEAT_G + h:FEAT_G + h + 1, :]
        g_last = gc[CHUNK - 1:CHUNK, :]
        eg = jnp.exp(gc)
        kb = k * beta
        cdec_ref[i] = jnp.exp(jnp.where(incl, gc - gr, NEG_BIG))
        cgl_ref[i] = jnp.broadcast_to(jnp.exp(g_last), (SUBLANES, GDN_DK))
        cb_ref[Q, i] = q.astype(BF16)
        cb_ref[QD, i] = (q * eg).astype(BF16)
        cb_ref[K, i] = k.astype(BF16)
        cb_ref[KB, i] = kb.astype(BF16)
        cb_ref[KD, i] = (k * jnp.exp(g_last - gc)).astype(BF16)
        cb_ref[VB, i] = (v_ref[j, :, lanes(h)].astype(F32) * beta).astype(BF16)
        cb_ref[KBE, i] = (kb * eg).astype(BF16)

    @pl.when(ch == pl.num_programs(1) - 1)
    def _():
        for j in range(nseq):
            sfin_ref[j] = s_ref[j * GDN_HEADS:(j + 1) * GDN_HEADS]


GDN_SEQ_PER_STEP = 2


def _gdn_chunk(qkv_act, z_act, feat, cs, cst, norm_w):
    bsz, t_pad, _ = qkv_act.shape
    nch = t_pad // CHUNK
    nseq = GDN_SEQ_PER_STEP if bsz % GDN_SEQ_PER_STEP == 0 else 1
    nchain = nseq * GDN_HEADS

    def cur(c):
        return jnp.minimum(c, nch - 1)

    def prev(c):
        return jnp.maximum(c - 1, 0)

    def qkv_spec(part):
        return pl.BlockSpec((nseq, CHUNK, GDN_QK), lambda b, c: (b, cur(c), part))

    return pl.pallas_call(
        functools.partial(_gdn_chunk_kernel, nseq=nseq),
        grid=(bsz // nseq, nch + 1),
        in_specs=[qkv_spec(0), qkv_spec(1), qkv_spec(2),
                  pl.BlockSpec((nseq, CHUNK, D_GDN), lambda b, c: (b, prev(c), 0)),
                  pl.BlockSpec((nseq, CHUNK, LANES), lambda b, c: (b, cur(c), 0)),
                  pl.BlockSpec((nseq, CHUNK, LANES), lambda b, c: (b, cur(c), 0)),
                  pl.BlockSpec((nseq, 1, CHUNK, LANES), lambda b, c: (b, cur(c), 0, 0)),
                  pl.BlockSpec((1, GDN_DK), lambda b, c: (0, 0))],
        out_specs=[pl.BlockSpec((nseq, CHUNK, D_GDN), lambda b, c: (b, prev(c), 0)),
                   pl.BlockSpec((nseq, GDN_HEADS, GDN_DK, GDN_DK), lambda b, c: (b, 0, 0, 0))],
        out_shape=[jax.ShapeDtypeStruct((bsz, t_pad, D_GDN), ACT_DTYPE),
                   jax.ShapeDtypeStruct((bsz, GDN_HEADS, GDN_DK, GDN_DK), F32)],
        scratch_shapes=[pltpu.VMEM((nchain, GDN_DK, GDN_DK), F32),
                        pltpu.VMEM((7, nchain, CHUNK, GDN_DK), BF16),
                        pltpu.VMEM((nchain, CHUNK, CHUNK), F32),
                        pltpu.VMEM((nchain, SUBLANES, GDN_DK), F32)],
        compiler_params=_cparams("arbitrary", "arbitrary"),
        name="gdn_chunk",
    )(qkv_act, qkv_act, qkv_act, z_act, feat, cs, cst, norm_w.reshape(1, GDN_DK))


def _ssd_finish(y, zs, nw):
    yg = y * zs
    gw = D_SSM // SSM_GROUPS
    parts = []
    for g in range(SSM_GROUPS):
        blk = yg[:, g * gw:(g + 1) * gw]
        parts.append(blk * lax.rsqrt(jnp.mean(blk * blk, -1, keepdims=True) + EPS))
    return jnp.concatenate(parts, axis=1) * nw


def _ssd_chunk_kernel(x_ref, b_ref, c_ref, zs_ref, featt_ref, cs_ref, cst_ref, d_ref, nw_ref,
                      y_ref, sfin_ref, s_ref, *, nseq):
    ch = pl.program_id(1)

    @pl.when(ch == 0)
    def _():
        s_ref[...] = jnp.zeros(s_ref.shape, F32)

    row = lax.broadcasted_iota(jnp.int32, (CHUNK, CHUNK), 0)
    col = lax.broadcasted_iota(jnp.int32, (CHUNK, CHUNK), 1)
    upper = col >= row
    top = row < SSM_HEAD_DIM
    heads_per_group = SSM_HEADS // SSM_GROUPS
    npair = SSM_HEADS // 2
    ecs, edl = [], []
    for j in range(nseq):
        cst = cst_ref[j, 0]
        ecs.append(jnp.exp(cst))
        edl.append(jnp.exp(cst[:, CHUNK - 1:CHUNK] - cst))
    grp_slices = [slice(g * SSM_STATE, (g + 1) * SSM_STATE) for g in range(SSM_GROUPS)]
    cbt = [[_dot_nt(b_ref[j, :, gs], c_ref[j, :, gs]) for gs in grp_slices] for j in range(nseq)]
    chains = [(j, p) for j in range(nseq) for p in range(npair)]

    def grp_of(p):
        return grp_slices[(2 * p) // heads_per_group]

    def rows2(tile, base, p):
        return jnp.where(top, tile[base + 2 * p:base + 2 * p + 1, :],
                         tile[base + 2 * p + 1:base + 2 * p + 2, :])

    def decay_t(j, h):
        ac_row = cst_ref[j, 0, FEAT_LA + h:FEAT_LA + h + 1, :]
        ac_col = cs_ref[j, :, FEAT_LA + h:FEAT_LA + h + 1]
        return jnp.exp(jnp.where(upper, ac_row - ac_col, NEG_BIG))

    xt = [x_ref[j, :, p * LANES:(p + 1) * LANES].astype(F32).T for j, p in chains]
    xdt = [x * rows2(featt_ref[j, 0], FEAT_DT, p) for x, (j, p) in zip(xt, chains)]
    st = [jnp.concatenate([cbt[j][(2 * p) // heads_per_group] * decay_t(j, 2 * p),
                           cbt[j][(2 * p) // heads_per_group] * decay_t(j, 2 * p + 1)], axis=0)
          for j, p in chains]
    lhs = [jnp.concatenate([jnp.where(top, x, 0.0), jnp.where(top, 0.0, x)], axis=1) for x in xdt]
    y_diag = [_dot(l, r) for l, r in zip(lhs, st)]
    s = [s_ref[i * LANES:(i + 1) * LANES, :] for i in range(len(chains))]
    y_off = [_dot_nt(z, c_ref[j, :, grp_of(p)]) * rows2(ecs[j], FEAT_LA, p)
             for z, (j, p) in zip(s, chains)]
    ds = [_dot(x * rows2(edl[j], FEAT_LA, p), b_ref[j, :, grp_of(p)])
          for x, (j, p) in zip(xdt, chains)]
    for i, (j, p) in enumerate(chains):
        last = rows2(ecs[j], FEAT_LA, p)[:, CHUNK - 1:CHUNK]
        s_ref[i * LANES:(i + 1) * LANES, :] = s[i] * last + ds[i]
    yt = [a + b + d_ref[p * LANES:(p + 1) * LANES, :] * x
          for a, b, x, (j, p) in zip(y_diag, y_off, xt, chains)]
    for j in range(nseq):
        y = jnp.concatenate([yt[j * npair + p].T for p in range(npair)], axis=1)
        y_ref[j] = _ssd_finish(y, zs_ref[j].astype(F32), nw_ref[...]).astype(y_ref.dtype)

    @pl.when(ch == pl.num_programs(1) - 1)
    def _():
        rows = SSM_HEADS * SSM_HEAD_DIM
        for j in range(nseq):
            sfin_ref[j] = s_ref[j * rows:(j + 1) * rows, :]


SSD_SEQ_PER_STEP = 2


def _ssd_chunk(xbc_act, z_act, featt, cs, cst, d_rows, norm_w):
    bsz, t_pad, _ = xbc_act.shape
    nch = t_pad // CHUNK
    bc_w = SSM_GROUPS * SSM_STATE
    nseq = SSD_SEQ_PER_STEP if bsz % SSD_SEQ_PER_STEP == 0 else 1
    rows = SSM_HEADS * SSM_HEAD_DIM
    return pl.pallas_call(
        functools.partial(_ssd_chunk_kernel, nseq=nseq),
        grid=(bsz // nseq, nch),
        in_specs=[pl.BlockSpec((nseq, CHUNK, D_SSM), lambda b, c: (b, c, 0)),
                  pl.BlockSpec((nseq, CHUNK, bc_w), lambda b, c: (b, c, D_SSM // bc_w)),
                  pl.BlockSpec((nseq, CHUNK, bc_w), lambda b, c: (b, c, D_SSM // bc_w + 1)),
                  pl.BlockSpec((nseq, CHUNK, D_SSM), lambda b, c: (b, c, 1)),
                  pl.BlockSpec((nseq, 1, CHUNK, LANES), lambda b, c: (b, c, 0, 0)),
                  pl.BlockSpec((nseq, CHUNK, LANES), lambda b, c: (b, c, 0)),
                  pl.BlockSpec((nseq, 1, CHUNK, LANES), lambda b, c: (b, c, 0, 0)),
                  pl.BlockSpec((rows, LANES), lambda b, c: (0, 0)),
                  pl.BlockSpec((1, D_SSM), lambda b, c: (0, 0))],
        out_specs=[pl.BlockSpec((nseq, CHUNK, D_SSM), lambda b, c: (b, c, 0)),
                   pl.BlockSpec((nseq, rows, SSM_STATE), lambda b, c: (b, 0, 0))],
        out_shape=[jax.ShapeDtypeStruct((bsz, t_pad, D_SSM), ACT_DTYPE),
                   jax.ShapeDtypeStruct((bsz, rows, SSM_STATE), F32)],
        scratch_shapes=[pltpu.VMEM((nseq * rows, SSM_STATE), F32)],
        compiler_params=_cparams("arbitrary", "arbitrary"),
        name="ssd_chunk",
    )(xbc_act, xbc_act, xbc_act, z_act, featt, cs, cst, d_rows, norm_w.reshape(1, D_SSM))


def _out_proj_kernel(mg_ref, ms_ref, x_ref, w_ref, nw_ref, xo_ref, ho_ref, *, seq_rows):
    acc = (jnp.dot(mg_ref[...].astype(BF16), w_ref[0:D_GDN, :], preferred_element_type=F32)
           + jnp.dot(ms_ref[...].astype(BF16), w_ref[D_GDN:, :], preferred_element_type=F32))
    xn = x_ref[...] + acc
    if seq_rows:
        tile = xn.shape[0]
        start = lax.rem(pl.program_id(0) * tile, seq_rows)
        r = start + lax.broadcasted_iota(jnp.int32, xn.shape, 0)
        pad = (r < FRONT_PAD) | ((r >= seq_rows) & (r < seq_rows + FRONT_PAD))
        xn = jnp.where(pad, 0.0, xn)
    xo_ref[...] = xn
    hn = xn * lax.rsqrt(jnp.mean(xn * xn, -1, keepdims=True) + EPS) * nw_ref[...]
    ho_ref[...] = hn.astype(ho_ref.dtype)


def _out_proj(mix_g, mix_s, x, w_out, next_norm_w, h_dtype, tile, seq_rows):
    rows = x.shape[0]
    row_spec = pl.BlockSpec((tile, D_MODEL), lambda i: (i, 0))
    return pl.pallas_call(
        functools.partial(_out_proj_kernel, seq_rows=seq_rows),
        grid=(rows // tile,),
        in_specs=[row_spec, row_spec, row_spec,
                  pl.BlockSpec((D_GDN + D_SSM, D_MODEL), lambda i: (0, 0)),
                  pl.BlockSpec((1, D_MODEL), lambda i: (0, 0))],
        out_specs=[row_spec, row_spec],
        out_shape=[jax.ShapeDtypeStruct((rows, D_MODEL), F32),
                   jax.ShapeDtypeStruct((rows, D_MODEL), h_dtype)],
        compiler_params=_cparams("arbitrary"),
        name="out_proj",
    )(mix_g, mix_s, x, w_out, next_norm_w.reshape(1, D_MODEL))


def _proj_conv_decode_kernel(h_ref, w_ref, buf_ref, cw_ref, cb_ref, out_ref, nbuf_ref):
    z = jnp.dot(h_ref[...], w_ref[...], preferred_element_type=F32)
    cw = cw_ref[...]
    acc = cb_ref[...] + cw[3:4] * z
    for tap in range(CONV_W - 1):
        acc = acc + cw[tap:tap + 1] * buf_ref[tap]
    out_ref[...] = _silu_of_twice(acc)
    nbuf_ref[0] = buf_ref[1]
    nbuf_ref[1] = buf_ref[2]
    nbuf_ref[2] = z


def _proj_silu_decode_kernel(h_ref, w_ref, out_ref):
    out_ref[...] = _silu_of_twice(jnp.dot(h_ref[...], w_ref[...], preferred_element_type=F32))


def _proj_conv_decode(h, w, buf, cw, cb):
    rows = h.shape[0]
    ncols = w.shape[1]
    return pl.pallas_call(
        _proj_conv_decode_kernel,
        grid=(ncols // COL_BLOCK,),
        in_specs=[pl.BlockSpec((rows, D_MODEL), lambda j: (0, 0)),
                  pl.BlockSpec((D_MODEL, COL_BLOCK), lambda j: (0, j)),
                  pl.BlockSpec((CONV_W - 1, rows, COL_BLOCK), lambda j: (0, 0, j)),
                  pl.BlockSpec((CONV_W, COL_BLOCK), lambda j: (0, j)),
                  pl.BlockSpec((1, COL_BLOCK), lambda j: (0, j))],
        out_specs=[pl.BlockSpec((rows, COL_BLOCK), lambda j: (0, j)),
                   pl.BlockSpec((CONV_W - 1, rows, COL_BLOCK), lambda j: (0, 0, j))],
        out_shape=[jax.ShapeDtypeStruct((rows, ncols), F32),
                   jax.ShapeDtypeStruct((CONV_W - 1, rows, ncols), F32)],
        compiler_params=_cparams("arbitrary"),
        name="proj_conv_decode",
    )(h, w, buf, cw, cb.reshape(1, ncols))


def _proj_silu_decode(h, w):
    rows = h.shape[0]
    ncols = w.shape[1]
    return pl.pallas_call(
        _proj_silu_decode_kernel,
        grid=(ncols // COL_BLOCK,),
        in_specs=[pl.BlockSpec((rows, D_MODEL), lambda j: (0, 0)),
                  pl.BlockSpec((D_MODEL, COL_BLOCK), lambda j: (0, j))],
        out_specs=pl.BlockSpec((rows, COL_BLOCK), lambda j: (0, j)),
        out_shape=jax.ShapeDtypeStruct((rows, ncols), F32),
        compiler_params=_cparams("arbitrary"),
        name="proj_silu_decode",
    )(h, w)


TOK_BLOCK = 8


def _gdn_step_kernel(qt_ref, kt_ref, v_ref, zg_ref, feat_ref, nw_ref, s_ref, *rest):
    o_ref, so_ref = rest[-2:]
    feat = feat_ref[...]
    vrows = v_ref[...]
    for h in range(GDN_HEADS):
        sl = slice(h * GDN_DK, (h + 1) * GDN_DK)
        qt = qt_ref[0, h]
        kt = kt_ref[0, h]
        qt = qt * (lax.rsqrt(jnp.sum(qt * qt, 0, keepdims=True) + EPS) * (GDN_DK ** -0.5))
        kt = kt * lax.rsqrt(jnp.sum(kt * kt, 0, keepdims=True) + EPS)
        for j in range(TOK_BLOCK):
            kc = kt[:, j:j + 1]
            qc = qt[:, j:j + 1]
            beta = feat[j:j + 1, FEAT_BETA + h:FEAT_BETA + h + 1]
            g = feat[j:j + 1, FEAT_G + h:FEAT_G + h + 1]
            s = s_ref[j, h] * jnp.exp(g)
            kv = jnp.sum(s * kc, 0, keepdims=True)
            d = (vrows[j:j + 1, sl] - kv) * beta
            s = s + kc * d
            so_ref[j, h] = s
            o_ref[j:j + 1, sl] = jnp.sum(s * qc, 0, keepdims=True)
    nw = nw_ref[...]
    for h in range(GDN_HEADS):
        sl = slice(h * GDN_DK, (h + 1) * GDN_DK)
        o = o_ref[:, sl]
        o_ref[:, sl] = o * lax.rsqrt(jnp.mean(o * o, -1, keepdims=True) + EPS) * nw * zg_ref[:, sl]


def _layer_state_call(kernel_fn, name, layer, states, acc, in_specs, out_specs, out_shapes, args):
    rows = states.shape[1]
    blk = (None, TOK_BLOCK) + states.shape[2:]
    zeros = (0,) * (states.ndim - 2)
    state_spec = pl.BlockSpec(blk, lambda i: (layer, i) + zeros)
    in_specs = in_specs + [state_spec]
    args = args + [states]
    aliases = {}
    if acc is not None:
        in_specs = in_specs + [pl.BlockSpec(memory_space=pl.ANY)]
        args = args + [acc]
        aliases = {len(args) - 1: len(out_shapes)}
    return pl.pallas_call(
        kernel_fn,
        grid=(rows // TOK_BLOCK,),
        in_specs=in_specs,
        out_specs=out_specs + [state_spec],
        out_shape=out_shapes + [jax.ShapeDtypeStruct(states.shape, F32)],
        input_output_aliases=aliases,
        compiler_params=_cparams("arbitrary"),
        name=name,
    )(*args)


def _gdn_step(layer, qt, kt, qkv_act, z_act, feat, norm_w, states, acc):
    rows = qkv_act.shape[0]
    return _layer_state_call(
        _gdn_step_kernel, "gdn_step", layer, states, acc,
        [pl.BlockSpec((1, GDN_HEADS, GDN_DK, TOK_BLOCK), lambda i: (i, 0, 0, 0)),
         pl.BlockSpec((1, GDN_HEADS, GDN_DK, TOK_BLOCK), lambda i: (i, 0, 0, 0)),
         pl.BlockSpec((TOK_BLOCK, D_GDN), lambda i: (i, 2)),
         pl.BlockSpec((TOK_BLOCK, D_GDN), lambda i: (i, 0)),
         pl.BlockSpec((TOK_BLOCK, LANES), lambda i: (i, 0)),
         pl.BlockSpec((1, GDN_DK), lambda i: (0, 0))],
        [pl.BlockSpec((TOK_BLOCK, D_GDN), lambda i: (i, 0))],
        [jax.ShapeDtypeStruct((rows, D_GDN), F32)],
        [qt, kt, qkv_act, z_act, feat, norm_w.reshape(1, GDN_DK)])


SSD_STEP_HEADS = 4


def _ssd_step_kernel(xt_ref, b_ref, c_ref, feat_ref, s_ref, *rest):
    yt_ref, so_ref = rest[-2:]
    feat = feat_ref[...]
    heads_per_group = SSM_HEADS // SSM_GROUPS
    ela = jnp.exp(feat)
    for h0 in range(0, SSM_HEADS, SSD_STEP_HEADS):
        hj = [(h, j) for h in range(h0, h0 + SSD_STEP_HEADS) for j in range(TOK_BLOCK)]

        def grp_lanes(h):
            grp = h // heads_per_group
            return slice(grp * SSM_STATE, (grp + 1) * SSM_STATE)

        xdt = [xt_ref[0, h, :, j:j + 1] * feat[j:j + 1, FEAT_DT + h:FEAT_DT + h + 1] for h, j in hj]
        xb = [jnp.broadcast_to(x, (SSM_HEAD_DIM, SSM_STATE)) for x in xdt]
        s = [s_ref[j, h] * ela[j:j + 1, FEAT_LA + h:FEAT_LA + h + 1] + x * b_ref[j:j + 1, grp_lanes(h)]
             for (h, j), x in zip(hj, xb)]
        for (h, j), x in zip(hj, s):
            so_ref[j, h] = x
        y = [jnp.sum(x * c_ref[j:j + 1, grp_lanes(h)], -1, keepdims=True) for (h, j), x in zip(hj, s)]
        for i in range(SSD_STEP_HEADS):
            yt_ref[0, h0 + i] = jnp.concatenate(y[i * TOK_BLOCK:(i + 1) * TOK_BLOCK], axis=1)


def _ssd_step(layer, xt, xbc_act, feat, states, acc):
    rows = xbc_act.shape[0]
    ngrp = rows // TOK_BLOCK
    bc_w = SSM_GROUPS * SSM_STATE
    return _layer_state_call(
        _ssd_step_kernel, "ssd_step", layer, states, acc,
        [pl.BlockSpec((1, SSM_HEADS, SSM_HEAD_DIM, TOK_BLOCK), lambda i: (i, 0, 0, 0)),
         pl.BlockSpec((TOK_BLOCK, bc_w), lambda i: (i, D_SSM // bc_w)),
         pl.BlockSpec((TOK_BLOCK, bc_w), lambda i: (i, D_SSM // bc_w + 1)),
         pl.BlockSpec((TOK_BLOCK, LANES), lambda i: (i, 0))],
        [pl.BlockSpec((1, SSM_HEADS, SSM_HEAD_DIM, TOK_BLOCK), lambda i: (i, 0, 0, 0))],
        [jax.ShapeDtypeStruct((ngrp, SSM_HEADS, SSM_HEAD_DIM, TOK_BLOCK), F32)],
        [xt, xbc_act, xbc_act, feat])


def _ssd_finish_decode_kernel(y_ref, x_ref, zs_ref, d_ref, nw_ref, out_ref):
    out_ref[...] = _ssd_finish(y_ref[...] + d_ref[...] * x_ref[...], zs_ref[...], nw_ref[...])


def _ssd_finish_decode(y, xbc_act, z_act, d_cols, norm_w):
    rows = y.shape[0]
    return pl.pallas_call(
        _ssd_finish_decode_kernel,
        grid=(1,),
        in_specs=[pl.BlockSpec((rows, D_SSM), lambda i: (0, 0)),
                  pl.BlockSpec((rows, D_SSM), lambda i: (0, 0)),
                  pl.BlockSpec((rows, D_SSM), lambda i: (0, 1)),
                  pl.BlockSpec((1, D_SSM), lambda i: (0, 0)),
                  pl.BlockSpec((1, D_SSM), lambda i: (0, 0))],
        out_specs=pl.BlockSpec((rows, D_SSM), lambda i: (0, 0)),
        out_shape=jax.ShapeDtypeStruct((rows, D_SSM), F32),
        compiler_params=_cparams("arbitrary"),
        name="ssd_finish_decode",
    )(y, xbc_act, z_act, d_cols, norm_w.reshape(1, D_SSM))


def _prepare_weights(w_in, gdn_A_log, gdn_dt_bias, ssm_A_log, ssm_dt_bias, ssm_D, w_out):
    depth = w_in.shape[0]
    w_qkv = w_in[..., :OFF_Z_G].astype(BF16)
    w_xbc = w_in[..., OFF_XBC:OFF_Z_S].astype(BF16)
    w_z = (0.5 * jnp.concatenate([w_in[..., OFF_Z_G:OFF_B], w_in[..., OFF_Z_S:OFF_DT]], axis=-1)
           ).astype(BF16)
    w_dt = w_in[..., OFF_DT:]
    w_small = jnp.concatenate(
        [w_in[..., OFF_B:OFF_XBC], w_dt, w_dt, jnp.zeros((depth, D_MODEL, LANES - FEAT_END), F32)],
        axis=-1).astype(BF16)
    zeros8 = jnp.zeros((depth, GDN_HEADS), F32)
    tail = jnp.zeros((depth, LANES - FEAT_END), F32)
    p_bias = jnp.concatenate([zeros8, gdn_dt_bias.astype(F32), ssm_dt_bias.astype(F32),
                              ssm_dt_bias.astype(F32), tail], axis=-1).reshape(depth, 1, LANES)
    p_scale = jnp.concatenate([zeros8, -jnp.exp(gdn_A_log.astype(F32)),
                               jnp.ones((depth, SSM_HEADS), F32), -jnp.exp(ssm_A_log.astype(F32)),
                               tail], axis=-1).reshape(depth, 1, LANES)
    d_cols = jnp.repeat(ssm_D.astype(F32), SSM_HEAD_DIM, axis=-1).reshape(depth, 1, D_SSM)
    return w_qkv, w_xbc, w_z, w_small, p_bias, p_scale, d_cols, w_out.astype(BF16)


def kernel(x_prompt, x_sample, state_gdn, state_gdn_conv, state_ssm, state_ssm_conv, meta_tokens,
           norm_w, w_in, gdn_conv_w, gdn_A_log, gdn_dt_bias, gdn_norm_w, ssm_conv_w, ssm_conv_b,
           ssm_A_log, ssm_dt_bias, ssm_D, ssm_norm_w, w_out, final_norm_w):
    bp, seq, _ = x_prompt.shape
    bd = x_sample.shape[0]
    t_pad = FRONT_PAD + N_META + seq
    xp, hp = _embed_prompt(x_prompt.astype(F32), meta_tokens.astype(F32), norm_w[0].astype(F32))
    xs = x_sample.astype(F32).reshape(bd, D_MODEL)
    hs = _rmsnorm(xs, norm_w[0], bd)
    zero_bias = jnp.zeros((GDN_CONV_DIM,), F32)
    ngrp = bd // TOK_BLOCK
    p_g, p_gc, p_s, p_sc = [], [], [], []
    d_g, d_s = None, None
    d_gc, d_sc = [], []
    prepared = _prepare_weights(w_in, gdn_A_log, gdn_dt_bias, ssm_A_log, ssm_dt_bias, ssm_D, w_out)
    for l in range(DEPTH):
        w_qkv, w_xbc, w_z, w_small, p_bias, p_scale, d_cols, w_o = [a[l] for a in prepared]
        last = l == DEPTH - 1
        nxt_w = final_norm_w if last else norm_w[l + 1]
        h_dtype = F32 if last else BF16
        gcw = 0.5 * gdn_conv_w[l].astype(F32)
        scw = 0.5 * ssm_conv_w[l].astype(F32)
        scb = 0.5 * ssm_conv_b[l].astype(F32)

        h3 = hp.reshape(bp, t_pad, D_MODEL)
        qkv_act, qkv_tail = _proj_conv(h3, w_qkv, gcw, None)
        xbc_act, xbc_tail = _proj_conv(h3, w_xbc, scw, scb)
        z_act = _proj_silu(h3, w_z)
        feat, featt, cs, cst = _feat_prompt(h3, w_small, p_bias, p_scale)
        d_rows = jnp.broadcast_to(d_cols.reshape(D_SSM, 1), (D_SSM, LANES))
        mix_g, sg = _gdn_chunk(qkv_act, z_act, feat, cs, cst, gdn_norm_w[l].astype(F32))
        mix_s, ss = _ssd_chunk(xbc_act, z_act, featt, cs, cst, d_rows, ssm_norm_w[l].astype(F32))
        xp, hp = _out_proj(mix_g.reshape(bp * t_pad, D_GDN), mix_s.reshape(bp * t_pad, D_SSM), xp,
                           w_o, nxt_w.astype(F32), h_dtype, ROW_TILE, t_pad)
        p_g.append(sg)
        p_gc.append(qkv_tail[:, SUBLANES - (CONV_W - 1):, :])
        p_s.append(ss.reshape(bp, SSM_HEADS, SSM_HEAD_DIM, SSM_STATE))
        p_sc.append(xbc_tail[:, SUBLANES - (CONV_W - 1):, :])

        gbuf = jnp.transpose(state_gdn_conv[l].astype(F32), (1, 0, 2))
        sbuf = jnp.transpose(state_ssm_conv[l].astype(F32), (1, 0, 2))
        qkv_d, gbuf_n = _proj_conv_decode(hs, w_qkv, gbuf, gcw, zero_bias)
        xbc_d, sbuf_n = _proj_conv_decode(hs, w_xbc, sbuf, scw, scb)
        z_d = _proj_silu_decode(hs, w_z)
        feat_d = _feat_decode(hs, w_small, p_bias, p_scale)

        def head_major(a, heads, dim):
            return jnp.transpose(a.reshape(ngrp, TOK_BLOCK, heads, dim), (0, 2, 3, 1))

        qt = head_major(qkv_d[:, :GDN_QK], GDN_HEADS, GDN_DK)
        kt = head_major(qkv_d[:, GDN_QK:2 * GDN_QK], GDN_HEADS, GDN_DK)
        mix_gd, d_g = _gdn_step(l, qt, kt, qkv_d, z_d, feat_d, gdn_norm_w[l].astype(F32),
                                state_gdn.astype(F32), d_g)
        xt = head_major(xbc_d[:, :D_SSM], SSM_HEADS, SSM_HEAD_DIM)
        yt, d_s = _ssd_step(l, xt, xbc_d, feat_d, state_ssm.astype(F32), d_s)
        y_d = jnp.transpose(yt, (0, 3, 1, 2)).reshape(bd, D_SSM)
        mix_sd = _ssd_finish_decode(y_d, xbc_d, z_d, d_cols, ssm_norm_w[l].astype(F32))
        xs, hs = _out_proj(mix_gd, mix_sd, xs, w_o, nxt_w.astype(F32), h_dtype, bd, 0)
        d_gc.append(jnp.transpose(gbuf_n, (1, 0, 2)))
        d_sc.append(jnp.transpose(sbuf_n, (1, 0, 2)))

    y_prompt = hp.reshape(bp, t_pad, D_MODEL)[:, FRONT_PAD + N_META:]
    y_sample = hs.reshape(bd, 1, D_MODEL)
    return (y_prompt, y_sample,
            jnp.stack(p_g), jnp.stack(p_gc), jnp.stack(p_s), jnp.stack(p_sc),
            d_g, jnp.stack(d_gc), d_s, jnp.stack(d_sc))
```

```python
import functools

import jax
import jax.numpy as jnp
from jax import lax
from jax.experimental import pallas as pl
from jax.experimental.pallas import tpu as pltpu

F32 = jnp.float32
BF16 = jnp.bfloat16

D_MODEL = 1024
DEPTH = 4
N_META = 16
GDN_HEADS = 8
GDN_DK = 128
GDN_QK = 1024
D_GDN = 1024
GDN_CONV_DIM = 3072
D_SSM = 1024
SSM_HEADS = 16
SSM_HEAD_DIM = 64
SSM_GROUPS = 2
SSM_STATE = 128
SSM_CONV_DIM = 1536
CONV_W = 4
EPS = 1e-6
OFF_Z_G = GDN_CONV_DIM
OFF_B = OFF_Z_G + D_GDN
OFF_A = OFF_B + GDN_HEADS
OFF_XBC = OFF_A + GDN_HEADS
OFF_Z_S = OFF_XBC + SSM_CONV_DIM
OFF_DT = OFF_Z_S + D_SSM

CHUNK = 128
HALF = CHUNK // 2
FRONT_PAD = 112
SUBLANES = 8
LANES = 128
COL_BLOCK = 512
ROW_TILE = 512
NEG_BIG = -1e30
ACT_DTYPE = BF16
VMEM_LIMIT = 56 * 1024 * 1024

FEAT_BETA = 0
FEAT_G = 8
FEAT_DT = 16
FEAT_LA = 32
FEAT_END = 48


def _cparams(*sem):
    return pltpu.CompilerParams(dimension_semantics=sem, vmem_limit_bytes=VMEM_LIMIT)


def _dot(a, b):
    return jnp.dot(a.astype(BF16), b.astype(BF16), preferred_element_type=F32)


def _dot_nt(a, b):
    return lax.dot_general(a.astype(BF16), b.astype(BF16), (((1,), (1,)), ((), ())),
                           preferred_element_type=F32)


def _dot_tn(a, b):
    return lax.dot_general(a.astype(BF16), b.astype(BF16), (((0,), (0,)), ((), ())),
                           preferred_element_type=F32)


def _silu_of_twice(h):
    return h * jnp.tanh(h) + h


def _softplus(x):
    return jnp.maximum(x, 0.0) + jnp.log1p(jnp.exp(-jnp.abs(x)))


def _rmsnorm_kernel(x_ref, w_ref, h_ref):
    x = x_ref[...]
    y = x * lax.rsqrt(jnp.mean(x * x, -1, keepdims=True) + EPS) * w_ref[...]
    h_ref[...] = y.astype(h_ref.dtype)


def _rmsnorm(x, w, tile):
    rows = x.shape[0]
    return pl.pallas_call(
        _rmsnorm_kernel,
        grid=(rows // tile,),
        in_specs=[pl.BlockSpec((tile, D_MODEL), lambda i: (i, 0)),
                  pl.BlockSpec((1, D_MODEL), lambda i: (0, 0))],
        out_specs=pl.BlockSpec((tile, D_MODEL), lambda i: (i, 0)),
        out_shape=jax.ShapeDtypeStruct((rows, D_MODEL), BF16),
        compiler_params=_cparams("arbitrary"),
        name="rmsnorm",
    )(x, w.reshape(1, D_MODEL))


def _embed_prompt_kernel(x_ref, meta_ref, w_ref, xo_ref, ho_ref):
    w = w_ref[...]

    def emit(row0, x):
        xo_ref[row0:row0 + CHUNK, :] = x
        y = x * lax.rsqrt(jnp.mean(x * x, -1, keepdims=True) + EPS) * w
        ho_ref[row0:row0 + CHUNK, :] = y.astype(ho_ref.dtype)

    emit(0, jnp.concatenate([jnp.zeros((FRONT_PAD, D_MODEL), F32), meta_ref[...]], axis=0))
    for c in range(x_ref.shape[1] // CHUNK):
        emit((c + 1) * CHUNK, x_ref[0, c * CHUNK:(c + 1) * CHUNK, :])


def _embed_prompt(x_prompt, meta_tokens, w):
    bp, seq, _ = x_prompt.shape
    t_pad = FRONT_PAD + N_META + seq
    row_spec = pl.BlockSpec((t_pad, D_MODEL), lambda b: (b, 0))
    return pl.pallas_call(
        _embed_prompt_kernel,
        grid=(bp,),
        in_specs=[pl.BlockSpec((1, seq, D_MODEL), lambda b: (b, 0, 0)),
                  pl.BlockSpec((N_META, D_MODEL), lambda b: (0, 0)),
                  pl.BlockSpec((1, D_MODEL), lambda b: (0, 0))],
        out_specs=[row_spec, row_spec],
        out_shape=[jax.ShapeDtypeStruct((bp * t_pad, D_MODEL), F32),
                   jax.ShapeDtypeStruct((bp * t_pad, D_MODEL), BF16)],
        compiler_params=_cparams("arbitrary"),
        name="embed_prompt",
    )(x_prompt, meta_tokens, w.reshape(1, D_MODEL))


PROJ_ROW_BLOCKS = 8


def _proj_conv_kernel(h_ref, w_ref, cw_ref, cb_ref, out_ref, tail_ref, *scratch, t_pad, bias):
    z_refs, y_ref = scratch[:-1], scratch[-1]
    ncol = out_ref.shape[2]
    ntile = ncol // LANES
    rb = t_pad // PROJ_ROW_BLOCKS
    stride = rb // SUBLANES
    cw = cw_ref[...]
    cb = cb_ref[...]
    first_row = lax.broadcasted_iota(jnp.int32, (SUBLANES, LANES), 0) == 0
    zero = jnp.zeros((SUBLANES, LANES), F32)
    dyn_zero = jnp.minimum(pl.program_id(0), 0)

    def matmul(r):
        z = jnp.dot(h_ref[0, r * rb:(r + 1) * rb, :], w_ref[...], preferred_element_type=F32)
        for c in range(ntile):
            z_refs[r][c] = z[:, c * LANES:(c + 1) * LANES]

    def shift_in(x, carry):
        return jnp.where(first_row, carry, pltpu.roll(x, 1, 0))

    def conv(r):
        for c in range(ntile):
            taps = [cw[j:j + 1, c * LANES:(c + 1) * LANES] for j in range(CONV_W)]
            carries = [zero] * (CONV_W - 1)
            if r > 0:
                hist = z_refs[r - 1][c, rb - SUBLANES:rb, :]
                part = taps[0] * hist
                for tap in range(1, CONV_W):
                    carries[tap - 1] = pltpu.roll(part, 1, 0)
                    part = taps[tap] * hist + carries[tap - 1]
            v = [z_refs[r][c, pl.ds(dyn_zero + i, SUBLANES, stride=stride), :] for i in range(stride)]
            acc = [taps[0] * x for x in v]
            for tap in range(1, CONV_W):
                wrapped = shift_in(acc[stride - 1], carries[tap - 1])
                acc = [taps[tap] * x + (wrapped if i == 0 else acc[i - 1]) for i, x in enumerate(v)]
            for i, y in enumerate(acc):
                if bias:
                    y = y + cb[:, c * LANES:(c + 1) * LANES]
                y_ref[c, pl.ds(i, SUBLANES, stride=stride), :] = _silu_of_twice(y)
        for c in range(ntile):
            out_ref[0, r * rb:(r + 1) * rb, c * LANES:(c + 1) * LANES] = y_ref[c].astype(out_ref.dtype)

    for r in range(PROJ_ROW_BLOCKS):
        matmul(r)
    for r in range(PROJ_ROW_BLOCKS):
        conv(r)
    for c in range(ntile):
        tail_ref[0, :, c * LANES:(c + 1) * LANES] = z_refs[PROJ_ROW_BLOCKS - 1][c, rb - SUBLANES:rb, :]


def _proj_silu_kernel(h_ref, w_ref, out_ref):
    out_ref[0] = _silu_of_twice(
        jnp.dot(h_ref[0], w_ref[...], preferred_element_type=F32)).astype(out_ref.dtype)


def _proj_conv(h3, w, cw, cb):
    bsz, t_pad, _ = h3.shape
    ncols = w.shape[1]
    bias = cb is not None
    if not bias:
        cb = jnp.zeros((ncols,), F32)
    return pl.pallas_call(
        functools.partial(_proj_conv_kernel, t_pad=t_pad, bias=bias),
        grid=(bsz, ncols // COL_BLOCK),
        in_specs=[pl.BlockSpec((1, t_pad, D_MODEL), lambda b, j: (b, 0, 0)),
                  pl.BlockSpec((D_MODEL, COL_BLOCK), lambda b, j: (0, j)),
                  pl.BlockSpec((CONV_W, COL_BLOCK), lambda b, j: (0, j)),
                  pl.BlockSpec((1, COL_BLOCK), lambda b, j: (0, j))],
        out_specs=[pl.BlockSpec((1, t_pad, COL_BLOCK), lambda b, j: (b, 0, j)),
                   pl.BlockSpec((1, SUBLANES, COL_BLOCK), lambda b, j: (b, 0, j))],
        out_shape=[jax.ShapeDtypeStruct((bsz, t_pad, ncols), ACT_DTYPE),
                   jax.ShapeDtypeStruct((bsz, SUBLANES, ncols), F32)],
        scratch_shapes=[pltpu.VMEM((COL_BLOCK // LANES, t_pad // PROJ_ROW_BLOCKS, LANES), F32)]
        * (PROJ_ROW_BLOCKS + 1),
        compiler_params=_cparams("arbitrary", "arbitrary"),
        name="proj_conv",
    )(h3, w, cw, cb.reshape(1, ncols))


def _proj_silu(h3, w):
    bsz, t_pad, _ = h3.shape
    ncols = w.shape[1]
    return pl.pallas_call(
        _proj_silu_kernel,
        grid=(bsz, ncols // COL_BLOCK),
        in_specs=[pl.BlockSpec((1, t_pad, D_MODEL), lambda b, j: (b, 0, 0)),
                  pl.BlockSpec((D_MODEL, COL_BLOCK), lambda b, j: (0, j))],
        out_specs=pl.BlockSpec((1, t_pad, COL_BLOCK), lambda b, j: (b, 0, j)),
        out_shape=jax.ShapeDtypeStruct((bsz, t_pad, ncols), ACT_DTYPE),
        compiler_params=_cparams("arbitrary", "arbitrary"),
        name="proj_silu",
    )(h3, w)


def _features(z, p_bias, p_scale):
    lane = lax.broadcasted_iota(jnp.int32, z.shape, 1)
    sp = p_scale * _softplus(z + p_bias)
    return jnp.where(lane < FEAT_G, jax.nn.sigmoid(z), jnp.where(lane < FEAT_END, sp, 0.0))


def _split3(x):
    x1 = x.astype(BF16)
    r1 = x - x1.astype(F32)
    x2 = r1.astype(BF16)
    x3 = (r1 - x2.astype(F32)).astype(BF16)
    return x1, x2, x3


def _feat_prompt_kernel(h_ref, w_ref, pb_ref, ps_ref, feat_ref, featt_ref, cs_ref, cst_ref, *,
                        t_pad):
    z = jnp.dot(h_ref[0], w_ref[...], preferred_element_type=F32)
    feat = _features(z, pb_ref[...], ps_ref[...])
    row = lax.broadcasted_iota(jnp.int32, feat.shape, 0)
    feat = jnp.where(row < FRONT_PAD, 0.0, feat)
    feat_ref[0] = feat
    r = lax.broadcasted_iota(jnp.int32, (CHUNK, CHUNK), 0)
    c = lax.broadcasted_iota(jnp.int32, (CHUNK, CHUNK), 1)
    tri = (r >= c).astype(BF16)
    for ch in range(t_pad // CHUNK):
        tile = feat[ch * CHUNK:(ch + 1) * CHUNK]
        featt_ref[0, ch] = tile.T
        x1, x2, x3 = _split3(tile)
        cs = (jnp.dot(tri, x1, preferred_element_type=F32)
              + jnp.dot(tri, x2, preferred_element_type=F32)
              + jnp.dot(tri, x3, preferred_element_type=F32))
        cs_ref[0, ch * CHUNK:(ch + 1) * CHUNK, :] = cs
        cst_ref[0, ch] = cs.T


def _feat_prompt(h3, w_small, p_bias, p_scale):
    bsz, t_pad, _ = h3.shape
    nch = t_pad // CHUNK
    return pl.pallas_call(
        functools.partial(_feat_prompt_kernel, t_pad=t_pad),
        grid=(bsz,),
        in_specs=[pl.BlockSpec((1, t_pad, D_MODEL), lambda b: (b, 0, 0)),
                  pl.BlockSpec((D_MODEL, LANES), lambda b: (0, 0)),
                  pl.BlockSpec((1, LANES), lambda b: (0, 0)),
                  pl.BlockSpec((1, LANES), lambda b: (0, 0))],
        out_specs=[pl.BlockSpec((1, t_pad, LANES), lambda b: (b, 0, 0)),
                   pl.BlockSpec((1, nch, CHUNK, LANES), lambda b: (b, 0, 0, 0)),
                   pl.BlockSpec((1, t_pad, LANES), lambda b: (b, 0, 0)),
                   pl.BlockSpec((1, nch, CHUNK, LANES), lambda b: (b, 0, 0, 0))],
        out_shape=[jax.ShapeDtypeStruct((bsz, t_pad, LANES), F32),
                   jax.ShapeDtypeStruct((bsz, nch, CHUNK, LANES), F32),
                   jax.ShapeDtypeStruct((bsz, t_pad, LANES), F32),
                   jax.ShapeDtypeStruct((bsz, nch, CHUNK, LANES), F32)],
        compiler_params=_cparams("arbitrary"),
        name="feat_prompt",
    )(h3, w_small, p_bias, p_scale)


def _feat_decode_kernel(h_ref, w_ref, pb_ref, ps_ref, feat_ref):
    z = jnp.dot(h_ref[...], w_ref[...], preferred_element_type=F32)
    feat_ref[...] = _features(z, pb_ref[...], ps_ref[...])


def _feat_decode(h, w_small, p_bias, p_scale):
    rows = h.shape[0]
    return pl.pallas_call(
        _feat_decode_kernel,
        out_shape=jax.ShapeDtypeStruct((rows, LANES), F32),
        compiler_params=pltpu.CompilerParams(vmem_limit_bytes=VMEM_LIMIT),
        name="feat_decode",
    )(h, w_small, p_bias, p_scale)


def _gdn_chunk_kernel(q_ref, k_ref, v_ref, zg_ref, feat_ref, cs_ref, cst_ref, nw_ref,
                      o_ref, sfin_ref, s_ref, cb_ref, cdec_ref, cgl_ref, *, nseq):
    ch = pl.program_id(1)

    @pl.when(ch == 0)
    def _():
        s_ref[...] = jnp.zeros(s_ref.shape, F32)
        cb_ref[...] = jnp.zeros(cb_ref.shape, BF16)
        cdec_ref[...] = jnp.zeros(cdec_ref.shape, F32)
        cgl_ref[...] = jnp.zeros(cgl_ref.shape, F32)

    row = lax.broadcasted_iota(jnp.int32, (CHUNK, CHUNK), 0)
    col = lax.broadcasted_iota(jnp.int32, (CHUNK, CHUNK), 1)
    incl = row >= col
    strict = row > col
    row_c = lax.broadcasted_iota(jnp.int32, (HALF, CHUNK), 0)
    col_c = lax.broadcasted_iota(jnp.int32, (HALF, CHUNK), 1)
    lo_c = col_c < HALF
    eye_c = (jnp.bitwise_and(col_c, HALF - 1) == row_c).astype(F32)
    zero_c = jnp.zeros((HALF, CHUNK), BF16)

    def bdiag(x):
        return jnp.concatenate([jnp.where(lo_c, x, zero_c), jnp.where(lo_c, zero_c, x)], axis=0)

    nw = nw_ref[...]
    chains = [(j, h) for j in range(nseq) for h in range(GDN_HEADS)]
    nchain = len(chains)
    Q, QD, K, KB, KD, VB, KBE = range(7)

    def lanes(h):
        return slice(h * GDN_DK, (h + 1) * GDN_DK)

    kbf = [cb_ref[K, i] for i in range(nchain)]
    decay = [cdec_ref[i] for i in range(nchain)]
    a = [jnp.where(strict, _dot_nt(cb_ref[KB, i], y) * d, 0.0)
         for i, (y, d) in enumerate(zip(kbf, decay))]
    p = [jnp.where(lo_c, -x[:HALF], -x[HALF:]) for x in a]
    inv = [eye_c + x for x in p]
    r_p = [bdiag(x.astype(BF16)) for x in p]
    for _ in range(5):
        p = [_dot(x, r) for x, r in zip(p, r_p)]
        r_p = [bdiag(x.astype(BF16)) for x in p]
        inv = [x + _dot(x, r) for x, r in zip(inv, r_p)]
    invb = [x.astype(BF16) for x in inv]
    r_a21 = [jnp.concatenate([zero_c, jnp.where(lo_c, x[HALF:].astype(BF16), zero_c)], axis=0)
             for x in a]
    fold = [_dot(x, r) for x, r in zip(invb, r_a21)]
    r_i11 = [jnp.concatenate([jnp.where(lo_c, x, zero_c), zero_c], axis=0) for x in invb]
    fold = [_dot(x, r) for x, r in zip(fold, r_i11)]
    t_inv = [jnp.concatenate([jnp.where(lo_c, x, 0.0), jnp.where(lo_c, -y, x)], axis=0)
             for x, y in zip(inv, fold)]
    uw = [_dot(t, jnp.concatenate([cb_ref[VB, i], cb_ref[KBE, i]], axis=1))
          for i, t in enumerate(t_inv)]
    attn = [_dot_nt(cb_ref[Q, i], y) * d for i, (y, d) in enumerate(zip(kbf, decay))]
    s = [s_ref[i] for i in range(nchain)]
    ws_qs = [_dot(jnp.concatenate([x[:, GDN_DK:].astype(BF16), cb_ref[QD, i]], axis=0), z)
             for i, (x, z) in enumerate(zip(uw, s))]
    v_new = [(x[:, :GDN_DK] - y[:CHUNK]).astype(BF16) for x, y in zip(uw, ws_qs)]
    o = [y[CHUNK:] + _dot(x, z) for y, x, z in zip(ws_qs, attn, v_new)]
    for i, (j, h) in enumerate(chains):
        s_ref[i] = s[i] * cgl_ref[i, 0:1, :] + _dot_tn(cb_ref[KD, i], v_new[i])
        on = o[i] * lax.rsqrt(jnp.mean(o[i] * o[i], -1, keepdims=True) + EPS)
        o_ref[j, :, lanes(h)] = (on * nw * zg_ref[j, :, lanes(h)].astype(F32)).astype(o_ref.dtype)

    def unit(x, scale):
        return x * (lax.rsqrt(jnp.sum(x * x, -1, keepdims=True) + EPS) * scale)

    for i, (j, h) in enumerate(chains):
        q = unit(q_ref[j, :, lanes(h)].astype(F32), GDN_DK ** -0.5)
        k = unit(k_ref[j, :, lanes(h)].astype(F32), 1.0)
        beta = feat_ref[j, :, FEAT_BETA + h:FEAT_BETA + h + 1]
        gc = cs_ref[j, :, FEAT_G + h:FEAT_G + h + 1]
        gr = cst_ref[j, 0, FEAT_G + h:FEAT_G + h + 1, :]
        g_last = gc[CHUNK - 1:CHUNK, :]
        eg = jnp.exp(gc)
        kb = k * beta
        cdec_ref[i] = jnp.exp(jnp.where(incl, gc - gr, NEG_BIG))
        cgl_ref[i] = jnp.broadcast_to(jnp.exp(g_last), (SUBLANES, GDN_DK))
        cb_ref[Q, i] = q.astype(BF16)
        cb_ref[QD, i] = (q * eg).astype(BF16)
        cb_ref[K, i] = k.astype(BF16)
        cb_ref[KB, i] = kb.astype(BF16)
        cb_ref[KD, i] = (k * jnp.exp(g_last - gc)).astype(BF16)
        cb_ref[VB, i] = (v_ref[j, :, lanes(h)].astype(F32) * beta).astype(BF16)
        cb_ref[KBE, i] = (kb * eg).astype(BF16)

    @pl.when(ch == pl.num_programs(1) - 1)
    def _():
        for j in range(nseq):
            sfin_ref[j] = s_ref[j * GDN_HEADS:(j + 1) * GDN_HEADS]


GDN_SEQ_PER_STEP = 2


def _gdn_chunk(qkv_act, z_act, feat, cs, cst, norm_w):
    bsz, t_pad, _ = qkv_act.shape
    nch = t_pad // CHUNK
    nseq = GDN_SEQ_PER_STEP if bsz % GDN_SEQ_PER_STEP == 0 else 1
    nchain = nseq * GDN_HEADS

    def cur(c):
        return jnp.minimum(c, nch - 1)

    def prev(c):
        return jnp.maximum(c - 1, 0)

    def qkv_spec(part):
        return pl.BlockSpec((nseq, CHUNK, GDN_QK), lambda b, c: (b, cur(c), part))

    return pl.pallas_call(
        functools.partial(_gdn_chunk_kernel, nseq=nseq),
        grid=(bsz // nseq, nch + 1),
        in_specs=[qkv_spec(0), qkv_spec(1), qkv_spec(2),
                  pl.BlockSpec((nseq, CHUNK, D_GDN), lambda b, c: (b, prev(c), 0)),
                  pl.BlockSpec((nseq, CHUNK, LANES), lambda b, c: (b, cur(c), 0)),
                  pl.BlockSpec((nseq, CHUNK, LANES), lambda b, c: (b, cur(c), 0)),
                  pl.BlockSpec((nseq, 1, CHUNK, LANES), lambda b, c: (b, cur(c), 0, 0)),
                  pl.BlockSpec((1, GDN_DK), lambda b, c: (0, 0))],
        out_specs=[pl.BlockSpec((nseq, CHUNK, D_GDN), lambda b, c: (b, prev(c), 0)),
                   pl.BlockSpec((nseq, GDN_HEADS, GDN_DK, GDN_DK), lambda b, c: (b, 0, 0, 0))],
        out_shape=[jax.ShapeDtypeStruct((bsz, t_pad, D_GDN), ACT_DTYPE),
                   jax.ShapeDtypeStruct((bsz, GDN_HEADS, GDN_DK, GDN_DK), F32)],
        scratch_shapes=[pltpu.VMEM((nchain, GDN_DK, GDN_DK), F32),
                        pltpu.VMEM((7, nchain, CHUNK, GDN_DK), BF16),
                        pltpu.VMEM((nchain, CHUNK, CHUNK), F32),
                        pltpu.VMEM((nchain, SUBLANES, GDN_DK), F32)],
        compiler_params=_cparams("arbitrary", "arbitrary"),
        name="gdn_chunk",
    )(qkv_act, qkv_act, qkv_act, z_act, feat, cs, cst, norm_w.reshape(1, GDN_DK))


def _ssd_finish(y, zs, nw):
    yg = y * zs
    gw = D_SSM // SSM_GROUPS
    parts = []
    for g in range(SSM_GROUPS):
        blk = yg[:, g * gw:(g + 1) * gw]
        parts.append(blk * lax.rsqrt(jnp.mean(blk * blk, -1, keepdims=True) + EPS))
    return jnp.concatenate(parts, axis=1) * nw


def _ssd_chunk_kernel(x_ref, b_ref, c_ref, zs_ref, featt_ref, cs_ref, cst_ref, d_ref, nw_ref,
                      y_ref, sfin_ref, s_ref, *, nseq):
    ch = pl.program_id(1)

    @pl.when(ch == 0)
    def _():
        s_ref[...] = jnp.zeros(s_ref.shape, F32)

    row = lax.broadcasted_iota(jnp.int32, (CHUNK, CHUNK), 0)
    col = lax.broadcasted_iota(jnp.int32, (CHUNK, CHUNK), 1)
    upper = col >= row
    top = row < SSM_HEAD_DIM
    heads_per_group = SSM_HEADS // SSM_GROUPS
    npair = SSM_HEADS // 2
    ecs, edl = [], []
    for j in range(nseq):
        cst = cst_ref[j, 0]
        ecs.append(jnp.exp(cst))
        edl.append(jnp.exp(cst[:, CHUNK - 1:CHUNK] - cst))
    grp_slices = [slice(g * SSM_STATE, (g + 1) * SSM_STATE) for g in range(SSM_GROUPS)]
    cbt = [[_dot_nt(b_ref[j, :, gs], c_ref[j, :, gs]) for gs in grp_slices] for j in range(nseq)]
    chains = [(j, p) for j in range(nseq) for p in range(npair)]

    def grp_of(p):
        return grp_slices[(2 * p) // heads_per_group]

    def rows2(tile, base, p):
        return jnp.where(top, tile[base + 2 * p:base + 2 * p + 1, :],
                         tile[base + 2 * p + 1:base + 2 * p + 2, :])

    def decay_t(j, h):
        ac_row = cst_ref[j, 0, FEAT_LA + h:FEAT_LA + h + 1, :]
        ac_col = cs_ref[j, :, FEAT_LA + h:FEAT_LA + h + 1]
        return jnp.exp(jnp.where(upper, ac_row - ac_col, NEG_BIG))

    xt = [x_ref[j, :, p * LANES:(p + 1) * LANES].astype(F32).T for j, p in chains]
    xdt = [x * rows2(featt_ref[j, 0], FEAT_DT, p) for x, (j, p) in zip(xt, chains)]
    st = [jnp.concatenate([cbt[j][(2 * p) // heads_per_group] * decay_t(j, 2 * p),
                           cbt[j][(2 * p) // heads_per_group] * decay_t(j, 2 * p + 1)], axis=0)
          for j, p in chains]
    lhs = [jnp.concatenate([jnp.where(top, x, 0.0), jnp.where(top, 0.0, x)], axis=1) for x in xdt]
    y_diag = [_dot(l, r) for l, r in zip(lhs, st)]
    s = [s_ref[i * LANES:(i + 1) * LANES, :] for i in range(len(chains))]
    y_off = [_dot_nt(z, c_ref[j, :, grp_of(p)]) * rows2(ecs[j], FEAT_LA, p)
             for z, (j, p) in zip(s, chains)]
    ds = [_dot(x * rows2(edl[j], FEAT_LA, p), b_ref[j, :, grp_of(p)])
          for x, (j, p) in zip(xdt, chains)]
    for i, (j, p) in enumerate(chains):
        last = rows2(ecs[j], FEAT_LA, p)[:, CHUNK - 1:CHUNK]
        s_ref[i * LANES:(i + 1) * LANES, :] = s[i] * last + ds[i]
    yt = [a + b + d_ref[p * LANES:(p + 1) * LANES, :] * x
          for a, b, x, (j, p) in zip(y_diag, y_off, xt, chains)]
    for j in range(nseq):
        y = jnp.concatenate([yt[j * npair + p].T for p in range(npair)], axis=1)
        y_ref[j] = _ssd_finish(y, zs_ref[j].astype(F32), nw_ref[...]).astype(y_ref.dtype)

    @pl.when(ch == pl.num_programs(1) - 1)
    def _():
        rows = SSM_HEADS * SSM_HEAD_DIM
        for j in range(nseq):
            sfin_ref[j] = s_ref[j * rows:(j + 1) * rows, :]


SSD_SEQ_PER_STEP = 2


def _ssd_chunk(xbc_act, z_act, featt, cs, cst, d_rows, norm_w):
    bsz, t_pad, _ = xbc_act.shape
    nch = t_pad // CHUNK
    bc_w = SSM_GROUPS * SSM_STATE
    nseq = SSD_SEQ_PER_STEP if bsz % SSD_SEQ_PER_STEP == 0 else 1
    rows = SSM_HEADS * SSM_HEAD_DIM
    return pl.pallas_call(
        functools.partial(_ssd_chunk_kernel, nseq=nseq),
        grid=(bsz // nseq, nch),
        in_specs=[pl.BlockSpec((nseq, CHUNK, D_SSM), lambda b, c: (b, c, 0)),
                  pl.BlockSpec((nseq, CHUNK, bc_w), lambda b, c: (b, c, D_SSM // bc_w)),
                  pl.BlockSpec((nseq, CHUNK, bc_w), lambda b, c: (b, c, D_SSM // bc_w + 1)),
                  pl.BlockSpec((nseq, CHUNK, D_SSM), lambda b, c: (b, c, 1)),
                  pl.BlockSpec((nseq, 1, CHUNK, LANES), lambda b, c: (b, c, 0, 0)),
                  pl.BlockSpec((nseq, CHUNK, LANES), lambda b, c: (b, c, 0)),
                  pl.BlockSpec((nseq, 1, CHUNK, LANES), lambda b, c: (b, c, 0, 0)),
                  pl.BlockSpec((rows, LANES), lambda b, c: (0, 0)),
                  pl.BlockSpec((1, D_SSM), lambda b, c: (0, 0))],
        out_specs=[pl.BlockSpec((nseq, CHUNK, D_SSM), lambda b, c: (b, c, 0)),
                   pl.BlockSpec((nseq, rows, SSM_STATE), lambda b, c: (b, 0, 0))],
        out_shape=[jax.ShapeDtypeStruct((bsz, t_pad, D_SSM), ACT_DTYPE),
                   jax.ShapeDtypeStruct((bsz, rows, SSM_STATE), F32)],
        scratch_shapes=[pltpu.VMEM((nseq * rows, SSM_STATE), F32)],
        compiler_params=_cparams("arbitrary", "arbitrary"),
        name="ssd_chunk",
    )(xbc_act, xbc_act, xbc_act, z_act, featt, cs, cst, d_rows, norm_w.reshape(1, D_SSM))


def _out_proj_kernel(mg_ref, ms_ref, x_ref, w_ref, nw_ref, xo_ref, ho_ref, *, seq_rows):
    acc = (jnp.dot(mg_ref[...].astype(BF16), w_ref[0:D_GDN, :], preferred_element_type=F32)
           + jnp.dot(ms_ref[...].astype(BF16), w_ref[D_GDN:, :], preferred_element_type=F32))
    xn = x_ref[...] + acc
    if seq_rows:
        tile = xn.shape[0]
        start = lax.rem(pl.program_id(0) * tile, seq_rows)
        r = start + lax.broadcasted_iota(jnp.int32, xn.shape, 0)
        pad = (r < FRONT_PAD) | ((r >= seq_rows) & (r < seq_rows + FRONT_PAD))
        xn = jnp.where(pad, 0.0, xn)
    xo_ref[...] = xn
    hn = xn * lax.rsqrt(jnp.mean(xn * xn, -1, keepdims=True) + EPS) * nw_ref[...]
    ho_ref[...] = hn.astype(ho_ref.dtype)


def _out_proj(mix_g, mix_s, x, w_out, next_norm_w, h_dtype, tile, seq_rows):
    rows = x.shape[0]
    row_spec = pl.BlockSpec((tile, D_MODEL), lambda i: (i, 0))
    return pl.pallas_call(
        functools.partial(_out_proj_kernel, seq_rows=seq_rows),
        grid=(rows // tile,),
        in_specs=[row_spec, row_spec, row_spec,
                  pl.BlockSpec((D_GDN + D_SSM, D_MODEL), lambda i: (0, 0)),
                  pl.BlockSpec((1, D_MODEL), lambda i: (0, 0))],
        out_specs=[row_spec, row_spec],
        out_shape=[jax.ShapeDtypeStruct((rows, D_MODEL), F32),
                   jax.ShapeDtypeStruct((rows, D_MODEL), h_dtype)],
        compiler_params=_cparams("arbitrary"),
        name="out_proj",
    )(mix_g, mix_s, x, w_out, next_norm_w.reshape(1, D_MODEL))


def _proj_conv_decode_kernel(h_ref, w_ref, buf_ref, cw_ref, cb_ref, out_ref, nbuf_ref):
    z = jnp.dot(h_ref[...], w_ref[...], preferred_element_type=F32)
    cw = cw_ref[...]
    acc = cb_ref[...] + cw[3:4] * z
    for tap in range(CONV_W - 1):
        acc = acc + cw[tap:tap + 1] * buf_ref[tap]
    out_ref[...] = _silu_of_twice(acc)
    nbuf_ref[0] = buf_ref[1]
    nbuf_ref[1] = buf_ref[2]
    nbuf_ref[2] = z


def _proj_silu_decode_kernel(h_ref, w_ref, out_ref):
    out_ref[...] = _silu_of_twice(jnp.dot(h_ref[...], w_ref[...], preferred_element_type=F32))


def _proj_conv_decode(h, w, buf, cw, cb):
    rows = h.shape[0]
    ncols = w.shape[1]
    return pl.pallas_call(
        _proj_conv_decode_kernel,
        grid=(ncols // COL_BLOCK,),
        in_specs=[pl.BlockSpec((rows, D_MODEL), lambda j: (0, 0)),
                  pl.BlockSpec((D_MODEL, COL_BLOCK), lambda j: (0, j)),
                  pl.BlockSpec((CONV_W - 1, rows, COL_BLOCK), lambda j: (0, 0, j)),
                  pl.BlockSpec((CONV_W, COL_BLOCK), lambda j: (0, j)),
                  pl.BlockSpec((1, COL_BLOCK), lambda j: (0, j))],
        out_specs=[pl.BlockSpec((rows, COL_BLOCK), lambda j: (0, j)),
                   pl.BlockSpec((CONV_W - 1, rows, COL_BLOCK), lambda j: (0, 0, j))],
        out_shape=[jax.ShapeDtypeStruct((rows, ncols), F32),
                   jax.ShapeDtypeStruct((CONV_W - 1, rows, ncols), F32)],
        compiler_params=_cparams("arbitrary"),
        name="proj_conv_decode",
    )(h, w, buf, cw, cb.reshape(1, ncols))


def _proj_silu_decode(h, w):
    rows = h.shape[0]
    ncols = w.shape[1]
    return pl.pallas_call(
        _proj_silu_decode_kernel,
        grid=(ncols // COL_BLOCK,),
        in_specs=[pl.BlockSpec((rows, D_MODEL), lambda j: (0, 0)),
                  pl.BlockSpec((D_MODEL, COL_BLOCK), lambda j: (0, j))],
        out_specs=pl.BlockSpec((rows, COL_BLOCK), lambda j: (0, j)),
        out_shape=jax.ShapeDtypeStruct((rows, ncols), F32),
        compiler_params=_cparams("arbitrary"),
        name="proj_silu_decode",
    )(h, w)


TOK_BLOCK = 8


def _gdn_step_kernel(qt_ref, kt_ref, v_ref, zg_ref, feat_ref, nw_ref, s_ref, *rest):
    o_ref, so_ref = rest[-2:]
    feat = feat_ref[...]
    vrows = v_ref[...]
    for h in range(GDN_HEADS):
        sl = slice(h * GDN_DK, (h + 1) * GDN_DK)
        qt = qt_ref[0, h]
        kt = kt_ref[0, h]
        qt = qt * (lax.rsqrt(jnp.sum(qt * qt, 0, keepdims=True) + EPS) * (GDN_DK ** -0.5))
        kt = kt * lax.rsqrt(jnp.sum(kt * kt, 0, keepdims=True) + EPS)
        for j in range(TOK_BLOCK):
            kc = kt[:, j:j + 1]
            qc = qt[:, j:j + 1]
            beta = feat[j:j + 1, FEAT_BETA + h:FEAT_BETA + h + 1]
            g = feat[j:j + 1, FEAT_G + h:FEAT_G + h + 1]
            s = s_ref[j, h] * jnp.exp(g)
            kv = jnp.sum(s * kc, 0, keepdims=True)
            d = (vrows[j:j + 1, sl] - kv) * beta
            s = s + kc * d
            so_ref[j, h] = s
            o_ref[j:j + 1, sl] = jnp.sum(s * qc, 0, keepdims=True)
    nw = nw_ref[...]
    for h in range(GDN_HEADS):
        sl = slice(h * GDN_DK, (h + 1) * GDN_DK)
        o = o_ref[:, sl]
        o_ref[:, sl] = o * lax.rsqrt(jnp.mean(o * o, -1, keepdims=True) + EPS) * nw * zg_ref[:, sl]


def _layer_state_call(kernel_fn, name, layer, states, acc, in_specs, out_specs, out_shapes, args):
    rows = states.shape[1]
    blk = (None, TOK_BLOCK) + states.shape[2:]
    zeros = (0,) * (states.ndim - 2)
    state_spec = pl.BlockSpec(blk, lambda i: (layer, i) + zeros)
    in_specs = in_specs + [state_spec]
    args = args + [states]
    aliases = {}
    if acc is not None:
        in_specs = in_specs + [pl.BlockSpec(memory_space=pl.ANY)]
        args = args + [acc]
        aliases = {len(args) - 1: len(out_shapes)}
    return pl.pallas_call(
        kernel_fn,
        grid=(rows // TOK_BLOCK,),
        in_specs=in_specs,
        out_specs=out_specs + [state_spec],
        out_shape=out_shapes + [jax.ShapeDtypeStruct(states.shape, F32)],
        input_output_aliases=aliases,
        compiler_params=_cparams("arbitrary"),
        name=name,
    )(*args)


def _gdn_step(layer, qt, kt, qkv_act, z_act, feat, norm_w, states, acc):
    rows = qkv_act.shape[0]
    return _layer_state_call(
        _gdn_step_kernel, "gdn_step", layer, states, acc,
        [pl.BlockSpec((1, GDN_HEADS, GDN_DK, TOK_BLOCK), lambda i: (i, 0, 0, 0)),
         pl.BlockSpec((1, GDN_HEADS, GDN_DK, TOK_BLOCK), lambda i: (i, 0, 0, 0)),
         pl.BlockSpec((TOK_BLOCK, D_GDN), lambda i: (i, 2)),
         pl.BlockSpec((TOK_BLOCK, D_GDN), lambda i: (i, 0)),
         pl.BlockSpec((TOK_BLOCK, LANES), lambda i: (i, 0)),
         pl.BlockSpec((1, GDN_DK), lambda i: (0, 0))],
        [pl.BlockSpec((TOK_BLOCK, D_GDN), lambda i: (i, 0))],
        [jax.ShapeDtypeStruct((rows, D_GDN), F32)],
        [qt, kt, qkv_act, z_act, feat, norm_w.reshape(1, GDN_DK)])


SSD_STEP_HEADS = 4


def _ssd_step_kernel(xt_ref, b_ref, c_ref, feat_ref, s_ref, *rest):
    yt_ref, so_ref = rest[-2:]
    feat = feat_ref[...]
    heads_per_group = SSM_HEADS // SSM_GROUPS
    ela = jnp.exp(feat)
    for h0 in range(0, SSM_HEADS, SSD_STEP_HEADS):
        hj = [(h, j) for h in range(h0, h0 + SSD_STEP_HEADS) for j in range(TOK_BLOCK)]

        def grp_lanes(h):
            grp = h // heads_per_group
            return slice(grp * SSM_STATE, (grp + 1) * SSM_STATE)

        xdt = [xt_ref[0, h, :, j:j + 1] * feat[j:j + 1, FEAT_DT + h:FEAT_DT + h + 1] for h, j in hj]
        xb = [jnp.broadcast_to(x, (SSM_HEAD_DIM, SSM_STATE)) for x in xdt]
        s = [s_ref[j, h] * ela[j:j + 1, FEAT_LA + h:FEAT_LA + h + 1] + x * b_ref[j:j + 1, grp_lanes(h)]
             for (h, j), x in zip(hj, xb)]
        for (h, j), x in zip(hj, s):
            so_ref[j, h] = x
        y = [jnp.sum(x * c_ref[j:j + 1, grp_lanes(h)], -1, keepdims=True) for (h, j), x in zip(hj, s)]
        for i in range(SSD_STEP_HEADS):
            yt_ref[0, h0 + i] = jnp.concatenate(y[i * TOK_BLOCK:(i + 1) * TOK_BLOCK], axis=1)


def _ssd_step(layer, xt, xbc_act, feat, states, acc):
    rows = xbc_act.shape[0]
    ngrp = rows // TOK_BLOCK
    bc_w = SSM_GROUPS * SSM_STATE
    return _layer_state_call(
        _ssd_step_kernel, "ssd_step", layer, states, acc,
        [pl.BlockSpec((1, SSM_HEADS, SSM_HEAD_DIM, TOK_BLOCK), lambda i: (i, 0, 0, 0)),
         pl.BlockSpec((TOK_BLOCK, bc_w), lambda i: (i, D_SSM // bc_w)),
         pl.BlockSpec((TOK_BLOCK, bc_w), lambda i: (i, D_SSM // bc_w + 1)),
         pl.BlockSpec((TOK_BLOCK, LANES), lambda i: (i, 0))],
        [pl.BlockSpec((1, SSM_HEADS, SSM_HEAD_DIM, TOK_BLOCK), lambda i: (i, 0, 0, 0))],
        [jax.ShapeDtypeStruct((ngrp, SSM_HEADS, SSM_HEAD_DIM, TOK_BLOCK), F32)],
        [xt, xbc_act, xbc_act, feat])


def _ssd_finish_decode_kernel(y_ref, x_ref, zs_ref, d_ref, nw_ref, out_ref):
    out_ref[...] = _ssd_finish(y_ref[...] + d_ref[...] * x_ref[...], zs_ref[...], nw_ref[...])


def _ssd_finish_decode(y, xbc_act, z_act, d_cols, norm_w):
    rows = y.shape[0]
    return pl.pallas_call(
        _ssd_finish_decode_kernel,
        grid=(1,),
        in_specs=[pl.BlockSpec((rows, D_SSM), lambda i: (0, 0)),
                  pl.BlockSpec((rows, D_SSM), lambda i: (0, 0)),
                  pl.BlockSpec((rows, D_SSM), lambda i: (0, 1)),
                  pl.BlockSpec((1, D_SSM), lambda i: (0, 0)),
                  pl.BlockSpec((1, D_SSM), lambda i: (0, 0))],
        out_specs=pl.BlockSpec((rows, D_SSM), lambda i: (0, 0)),
        out_shape=jax.ShapeDtypeStruct((rows, D_SSM), F32),
        compiler_params=_cparams("arbitrary"),
        name="ssd_finish_decode",
    )(y, xbc_act, z_act, d_cols, norm_w.reshape(1, D_SSM))


def _prepare_weights(w_in, gdn_A_log, gdn_dt_bias, ssm_A_log, ssm_dt_bias, ssm_D, w_out):
    depth = w_in.shape[0]
    w_qkv = w_in[..., :OFF_Z_G].astype(BF16)
    w_xbc = w_in[..., OFF_XBC:OFF_Z_S].astype(BF16)
    w_z = (0.5 * jnp.concatenate([w_in[..., OFF_Z_G:OFF_B], w_in[..., OFF_Z_S:OFF_DT]], axis=-1)
           ).astype(BF16)
    w_dt = w_in[..., OFF_DT:]
    w_small = jnp.concatenate(
        [w_in[..., OFF_B:OFF_XBC], w_dt, w_dt, jnp.zeros((depth, D_MODEL, LANES - FEAT_END), F32)],
        axis=-1).astype(BF16)
    zeros8 = jnp.zeros((depth, GDN_HEADS), F32)
    tail = jnp.zeros((depth, LANES - FEAT_END), F32)
    p_bias = jnp.concatenate([zeros8, gdn_dt_bias.astype(F32), ssm_dt_bias.astype(F32),
                              ssm_dt_bias.astype(F32), tail], axis=-1).reshape(depth, 1, LANES)
    p_scale = jnp.concatenate([zeros8, -jnp.exp(gdn_A_log.astype(F32)),
                               jnp.ones((depth, SSM_HEADS), F32), -jnp.exp(ssm_A_log.astype(F32)),
                               tail], axis=-1).reshape(depth, 1, LANES)
    d_cols = jnp.repeat(ssm_D.astype(F32), SSM_HEAD_DIM, axis=-1).reshape(depth, 1, D_SSM)
    return w_qkv, w_xbc, w_z, w_small, p_bias, p_scale, d_cols, w_out.astype(BF16)


def kernel(x_prompt, x_sample, state_gdn, state_gdn_conv, state_ssm, state_ssm_conv, meta_tokens,
           norm_w, w_in, gdn_conv_w, gdn_A_log, gdn_dt_bias, gdn_norm_w, ssm_conv_w, ssm_conv_b,
           ssm_A_log, ssm_dt_bias, ssm_D, ssm_norm_w, w_out, final_norm_w):
    bp, seq, _ = x_prompt.shape
    bd = x_sample.shape[0]
    t_pad = FRONT_PAD + N_META + seq
    xp, hp = _embed_prompt(x_prompt.astype(F32), meta_tokens.astype(F32), norm_w[0].astype(F32))
    xs = x_sample.astype(F32).reshape(bd, D_MODEL)
    hs = _rmsnorm(xs, norm_w[0], bd)
    zero_bias = jnp.zeros((GDN_CONV_DIM,), F32)
    ngrp = bd // TOK_BLOCK
    p_g, p_gc, p_s, p_sc = [], [], [], []
    d_g, d_s = None, None
    d_gc, d_sc = [], []
    prepared = _prepare_weights(w_in, gdn_A_log, gdn_dt_bias, ssm_A_log, ssm_dt_bias, ssm_D, w_out)
    for l in range(DEPTH):
        w_qkv, w_xbc, w_z, w_small, p_bias, p_scale, d_cols, w_o = [a[l] for a in prepared]
        last = l == DEPTH - 1
        nxt_w = final_norm_w if last else norm_w[l + 1]
        h_dtype = F32 if last else BF16
        gcw = 0.5 * gdn_conv_w[l].astype(F32)
        scw = 0.5 * ssm_conv_w[l].astype(F32)
        scb = 0.5 * ssm_conv_b[l].astype(F32)

        h3 = hp.reshape(bp, t_pad, D_MODEL)
        qkv_act, qkv_tail = _proj_conv(h3, w_qkv, gcw, None)
        xbc_act, xbc_tail = _proj_conv(h3, w_xbc, scw, scb)
        z_act = _proj_silu(h3, w_z)
        feat, featt, cs, cst = _feat_prompt(h3, w_small, p_bias, p_scale)
        d_rows = jnp.broadcast_to(d_cols.reshape(D_SSM, 1), (D_SSM, LANES))
        mix_g, sg = _gdn_chunk(qkv_act, z_act, feat, cs, cst, gdn_norm_w[l].astype(F32))
        mix_s, ss = _ssd_chunk(xbc_act, z_act, featt, cs, cst, d_rows, ssm_norm_w[l].astype(F32))
        xp, hp = _out_proj(mix_g.reshape(bp * t_pad, D_GDN), mix_s.reshape(bp * t_pad, D_SSM), xp,
                           w_o, nxt_w.astype(F32), h_dtype, ROW_TILE, t_pad)
        p_g.append(sg)
        p_gc.append(qkv_tail[:, SUBLANES - (CONV_W - 1):, :])
        p_s.append(ss.reshape(bp, SSM_HEADS, SSM_HEAD_DIM, SSM_STATE))
        p_sc.append(xbc_tail[:, SUBLANES - (CONV_W - 1):, :])

        gbuf = jnp.transpose(state_gdn_conv[l].astype(F32), (1, 0, 2))
        sbuf = jnp.transpose(state_ssm_conv[l].astype(F32), (1, 0, 2))
        qkv_d, gbuf_n = _proj_conv_decode(hs, w_qkv, gbuf, gcw, zero_bias)
        xbc_d, sbuf_n = _proj_conv_decode(hs, w_xbc, sbuf, scw, scb)
        z_d = _proj_silu_decode(hs, w_z)
        feat_d = _feat_decode(hs, w_small, p_bias, p_scale)

        def head_major(a, heads, dim):
            return jnp.transpose(a.reshape(ngrp, TOK_BLOCK, heads, dim), (0, 2, 3, 1))

        qt = head_major(qkv_d[:, :GDN_QK], GDN_HEADS, GDN_DK)
        kt = head_major(qkv_d[:, GDN_QK:2 * GDN_QK], GDN_HEADS, GDN_DK)
        mix_gd, d_g = _gdn_step(l, qt, kt, qkv_d, z_d, feat_d, gdn_norm_w[l].astype(F32),
                                state_gdn.astype(F32), d_g)
        xt = head_major(xbc_d[:, :D_SSM], SSM_HEADS, SSM_HEAD_DIM)
        yt, d_s = _ssd_step(l, xt, xbc_d, feat_d, state_ssm.astype(F32), d_s)
        y_d = jnp.transpose(yt, (0, 3, 1, 2)).reshape(bd, D_SSM)
        mix_sd = _ssd_finish_decode(y_d, xbc_d, z_d, d_cols, ssm_norm_w[l].astype(F32))
        xs, hs = _out_proj(mix_gd, mix_sd, xs, w_o, nxt_w.astype(F32), h_dtype, bd, 0)
        d_gc.append(jnp.transpose(gbuf_n, (1, 0, 2)))
        d_sc.append(jnp.transpose(sbuf_n, (1, 0, 2)))

    y_prompt = hp.reshape(bp, t_pad, D_MODEL)[:, FRONT_PAD + N_META:]
    y_sample = hs.reshape(bd, 1, D_MODEL)
    return (y_prompt, y_sample,
            jnp.stack(p_g), jnp.stack(p_gc), jnp.stack(p_s), jnp.stack(p_sc),
            d_g, jnp.stack(d_gc), d_s, jnp.stack(d_sc))
```

```python
import functools

import jax
import jax.numpy as jnp
from jax import lax
from jax.experimental import pallas as pl
from jax.experimental.pallas import tpu as pltpu

F32 = jnp.float32
BF16 = jnp.bfloat16

D_MODEL = 1024
DEPTH = 4
N_META = 16
GDN_HEADS = 8
GDN_DK = 128
GDN_QK = 1024
D_GDN = 1024
GDN_CONV_DIM = 3072
D_SSM = 1024
SSM_HEADS = 16
SSM_HEAD_DIM = 64
SSM_GROUPS = 2
SSM_STATE = 128
SSM_CONV_DIM = 1536
CONV_W = 4
EPS = 1e-6
OFF_Z_G = GDN_CONV_DIM
OFF_B = OFF_Z_G + D_GDN
OFF_A = OFF_B + GDN_HEADS
OFF_XBC = OFF_A + GDN_HEADS
OFF_Z_S = OFF_XBC + SSM_CONV_DIM
OFF_DT = OFF_Z_S + D_SSM

CHUNK = 128
HALF = CHUNK // 2
FRONT_PAD = 112
SUBLANES = 8
LANES = 128
COL_BLOCK = 512
ROW_TILE = 512
NEG_BIG = -1e30
ACT_DTYPE = BF16
VMEM_LIMIT = 56 * 1024 * 1024

FEAT_BETA = 0
FEAT_G = 8
FEAT_DT = 16
FEAT_LA = 32
FEAT_END = 48


def _cparams(*sem):
    return pltpu.CompilerParams(dimension_semantics=sem, vmem_limit_bytes=VMEM_LIMIT)


def _dot(a, b):
    return jnp.dot(a.astype(BF16), b.astype(BF16), preferred_element_type=F32)


def _dot_nt(a, b):
    return lax.dot_general(a.astype(BF16), b.astype(BF16), (((1,), (1,)), ((), ())),
                           preferred_element_type=F32)


def _dot_tn(a, b):
    return lax.dot_general(a.astype(BF16), b.astype(BF16), (((0,), (0,)), ((), ())),
                           preferred_element_type=F32)


def _silu_of_twice(h):
    return h * jnp.tanh(h) + h


def _softplus(x):
    return jnp.maximum(x, 0.0) + jnp.log1p(jnp.exp(-jnp.abs(x)))


def _rmsnorm_kernel(x_ref, w_ref, h_ref):
    x = x_ref[...]
    y = x * lax.rsqrt(jnp.mean(x * x, -1, keepdims=True) + EPS) * w_ref[...]
    h_ref[...] = y.astype(h_ref.dtype)


def _rmsnorm(x, w, tile):
    rows = x.shape[0]
    return pl.pallas_call(
        _rmsnorm_kernel,
        grid=(rows // tile,),
        in_specs=[pl.BlockSpec((tile, D_MODEL), lambda i: (i, 0)),
                  pl.BlockSpec((1, D_MODEL), lambda i: (0, 0))],
        out_specs=pl.BlockSpec((tile, D_MODEL), lambda i: (i, 0)),
        out_shape=jax.ShapeDtypeStruct((rows, D_MODEL), BF16),
        compiler_params=_cparams("arbitrary"),
        name="rmsnorm",
    )(x, w.reshape(1, D_MODEL))


def _embed_prompt_kernel(x_ref, meta_ref, w_ref, xo_ref, ho_ref):
    w = w_ref[...]

    def emit(row0, x):
        xo_ref[row0:row0 + CHUNK, :] = x
        y = x * lax.rsqrt(jnp.mean(x * x, -1, keepdims=True) + EPS) * w
        ho_ref[row0:row0 + CHUNK, :] = y.astype(ho_ref.dtype)

    emit(0, jnp.concatenate([jnp.zeros((FRONT_PAD, D_MODEL), F32), meta_ref[...]], axis=0))
    for c in range(x_ref.shape[1] // CHUNK):
        emit((c + 1) * CHUNK, x_ref[0, c * CHUNK:(c + 1) * CHUNK, :])


def _embed_prompt(x_prompt, meta_tokens, w):
    bp, seq, _ = x_prompt.shape
    t_pad = FRONT_PAD + N_META + seq
    row_spec = pl.BlockSpec((t_pad, D_MODEL), lambda b: (b, 0))
    return pl.pallas_call(
        _embed_prompt_kernel,
        grid=(bp,),
        in_specs=[pl.BlockSpec((1, seq, D_MODEL), lambda b: (b, 0, 0)),
                  pl.BlockSpec((N_META, D_MODEL), lambda b: (0, 0)),
                  pl.BlockSpec((1, D_MODEL), lambda b: (0, 0))],
        out_specs=[row_spec, row_spec],
        out_shape=[jax.ShapeDtypeStruct((bp * t_pad, D_MODEL), F32),
                   jax.ShapeDtypeStruct((bp * t_pad, D_MODEL), BF16)],
        compiler_params=_cparams("arbitrary"),
        name="embed_prompt",
    )(x_prompt, meta_tokens, w.reshape(1, D_MODEL))


PROJ_ROW_BLOCKS = 8


def _proj_conv_kernel(h_ref, w_ref, cw_ref, cb_ref, out_ref, tail_ref, *scratch, t_pad, bias):
    z_refs, y_ref = scratch[:-1], scratch[-1]
    ncol = out_ref.shape[2]
    ntile = ncol // LANES
    rb = t_pad // PROJ_ROW_BLOCKS
    stride = rb // SUBLANES
    cw = cw_ref[...]
    cb = cb_ref[...]
    first_row = lax.broadcasted_iota(jnp.int32, (SUBLANES, LANES), 0) == 0
    zero = jnp.zeros((SUBLANES, LANES), F32)
    dyn_zero = jnp.minimum(pl.program_id(0), 0)

    def matmul(r):
        z = jnp.dot(h_ref[0, r * rb:(r + 1) * rb, :], w_ref[...], preferred_element_type=F32)
        for c in range(ntile):
            z_refs[r][c] = z[:, c * LANES:(c + 1) * LANES]

    def shift_in(x, carry):
        return jnp.where(first_row, carry, pltpu.roll(x, 1, 0))

    def conv(r):
        for c in range(ntile):
            taps = [cw[j:j + 1, c * LANES:(c + 1) * LANES] for j in range(CONV_W)]
            carries = [zero] * (CONV_W - 1)
            if r > 0:
                hist = z_refs[r - 1][c, rb - SUBLANES:rb, :]
                part = taps[0] * hist
                for tap in range(1, CONV_W):
                    carries[tap - 1] = pltpu.roll(part, 1, 0)
                    part = taps[tap] * hist + carries[tap - 1]
            v = [z_refs[r][c, pl.ds(dyn_zero + i, SUBLANES, stride=stride), :] for i in range(stride)]
            acc = [taps[0] * x for x in v]
            for tap in range(1, CONV_W):
                wrapped = shift_in(acc[stride - 1], carries[tap - 1])
                acc = [taps[tap] * x + (wrapped if i == 0 else acc[i - 1]) for i, x in enumerate(v)]
            for i, y in enumerate(acc):
                if bias:
                    y = y + cb[:, c * LANES:(c + 1) * LANES]
                y_ref[c, pl.ds(i, SUBLANES, stride=stride), :] = _silu_of_twice(y)
        for c in range(ntile):
            out_ref[0, r * rb:(r + 1) * rb, c * LANES:(c + 1) * LANES] = y_ref[c].astype(out_ref.dtype)

    for r in range(PROJ_ROW_BLOCKS):
        matmul(r)
    for r in range(PROJ_ROW_BLOCKS):
        conv(r)
    for c in range(ntile):
        tail_ref[0, :, c * LANES:(c + 1) * LANES] = z_refs[PROJ_ROW_BLOCKS - 1][c, rb - SUBLANES:rb, :]


def _proj_silu_kernel(h_ref, w_ref, out_ref):
    out_ref[0] = _silu_of_twice(
        jnp.dot(h_ref[0], w_ref[...], preferred_element_type=F32)).astype(out_ref.dtype)


def _proj_conv(h3, w, cw, cb):
    bsz, t_pad, _ = h3.shape
    ncols = w.shape[1]
    bias = cb is not None
    if not bias:
        cb = jnp.zeros((ncols,), F32)
    return pl.pallas_call(
        functools.partial(_proj_conv_kernel, t_pad=t_pad, bias=bias),
        grid=(bsz, ncols // COL_BLOCK),
        in_specs=[pl.BlockSpec((1, t_pad, D_MODEL), lambda b, j: (b, 0, 0)),
                  pl.BlockSpec((D_MODEL, COL_BLOCK), lambda b, j: (0, j)),
                  pl.BlockSpec((CONV_W, COL_BLOCK), lambda b, j: (0, j)),
                  pl.BlockSpec((1, COL_BLOCK), lambda b, j: (0, j))],
        out_specs=[pl.BlockSpec((1, t_pad, COL_BLOCK), lambda b, j: (b, 0, j)),
                   pl.BlockSpec((1, SUBLANES, COL_BLOCK), lambda b, j: (b, 0, j))],
        out_shape=[jax.ShapeDtypeStruct((bsz, t_pad, ncols), ACT_DTYPE),
                   jax.ShapeDtypeStruct((bsz, SUBLANES, ncols), F32)],
        scratch_shapes=[pltpu.VMEM((COL_BLOCK // LANES, t_pad // PROJ_ROW_BLOCKS, LANES), F32)]
        * (PROJ_ROW_BLOCKS + 1),
        compiler_params=_cparams("arbitrary", "arbitrary"),
        name="proj_conv",
    )(h3, w, cw, cb.reshape(1, ncols))


def _proj_silu(h3, w):
    bsz, t_pad, _ = h3.shape
    ncols = w.shape[1]
    return pl.pallas_call(
        _proj_silu_kernel,
        grid=(bsz, ncols // COL_BLOCK),
        in_specs=[pl.BlockSpec((1, t_pad, D_MODEL), lambda b, j: (b, 0, 0)),
                  pl.BlockSpec((D_MODEL, COL_BLOCK), lambda b, j: (0, j))],
        out_specs=pl.BlockSpec((1, t_pad, COL_BLOCK), lambda b, j: (b, 0, j)),
        out_shape=jax.ShapeDtypeStruct((bsz, t_pad, ncols), ACT_DTYPE),
        compiler_params=_cparams("arbitrary", "arbitrary"),
        name="proj_silu",
    )(h3, w)


def _features(z, p_bias, p_scale):
    lane = lax.broadcasted_iota(jnp.int32, z.shape, 1)
    sp = p_scale * _softplus(z + p_bias)
    return jnp.where(lane < FEAT_G, jax.nn.sigmoid(z), jnp.where(lane < FEAT_END, sp, 0.0))


def _split3(x):
    x1 = x.astype(BF16)
    r1 = x - x1.astype(F32)
    x2 = r1.astype(BF16)
    x3 = (r1 - x2.astype(F32)).astype(BF16)
    return x1, x2, x3


def _feat_prompt_kernel(h_ref, w_ref, pb_ref, ps_ref, feat_ref, featt_ref, cs_ref, cst_ref, *,
                        t_pad):
    z = jnp.dot(h_ref[0], w_ref[...], preferred_element_type=F32)
    feat = _features(z, pb_ref[...], ps_ref[...])
    row = lax.broadcasted_iota(jnp.int32, feat.shape, 0)
    feat = jnp.where(row < FRONT_PAD, 0.0, feat)
    feat_ref[0] = feat
    r = lax.broadcasted_iota(jnp.int32, (CHUNK, CHUNK), 0)
    c = lax.broadcasted_iota(jnp.int32, (CHUNK, CHUNK), 1)
    tri = (r >= c).astype(BF16)
    for ch in range(t_pad // CHUNK):
        tile = feat[ch * CHUNK:(ch + 1) * CHUNK]
        featt_ref[0, ch] = tile.T
        x1, x2, x3 = _split3(tile)
        cs = (jnp.dot(tri, x1, preferred_element_type=F32)
              + jnp.dot(tri, x2, preferred_element_type=F32)
              + jnp.dot(tri, x3, preferred_element_type=F32))
        cs_ref[0, ch * CHUNK:(ch + 1) * CHUNK, :] = cs
        cst_ref[0, ch] = cs.T


def _feat_prompt(h3, w_small, p_bias, p_scale):
    bsz, t_pad, _ = h3.shape
    nch = t_pad // CHUNK
    return pl.pallas_call(
        functools.partial(_feat_prompt_kernel, t_pad=t_pad),
        grid=(bsz,),
        in_specs=[pl.BlockSpec((1, t_pad, D_MODEL), lambda b: (b, 0, 0)),
                  pl.BlockSpec((D_MODEL, LANES), lambda b: (0, 0)),
                  pl.BlockSpec((1, LANES), lambda b: (0, 0)),
                  pl.BlockSpec((1, LANES), lambda b: (0, 0))],
        out_specs=[pl.BlockSpec((1, t_pad, LANES), lambda b: (b, 0, 0)),
                   pl.BlockSpec((1, nch, CHUNK, LANES), lambda b: (b, 0, 0, 0)),
                   pl.BlockSpec((1, t_pad, LANES), lambda b: (b, 0, 0)),
                   pl.BlockSpec((1, nch, CHUNK, LANES), lambda b: (b, 0, 0, 0))],
        out_shape=[jax.ShapeDtypeStruct((bsz, t_pad, LANES), F32),
                   jax.ShapeDtypeStruct((bsz, nch, CHUNK, LANES), F32),
                   jax.ShapeDtypeStruct((bsz, t_pad, LANES), F32),
                   jax.ShapeDtypeStruct((bsz, nch, CHUNK, LANES), F32)],
        compiler_params=_cparams("arbitrary"),
        name="feat_prompt",
    )(h3, w_small, p_bias, p_scale)


def _feat_decode_kernel(h_ref, w_ref, pb_ref, ps_ref, feat_ref):
    z = jnp.dot(h_ref[...], w_ref[...], preferred_element_type=F32)
    feat_ref[...] = _features(z, pb_ref[...], ps_ref[...])


def _feat_decode(h, w_small, p_bias, p_scale):
    rows = h.shape[0]
    return pl.pallas_call(
        _feat_decode_kernel,
        out_shape=jax.ShapeDtypeStruct((rows, LANES), F32),
        compiler_params=pltpu.CompilerParams(vmem_limit_bytes=VMEM_LIMIT),
        name="feat_decode",
    )(h, w_small, p_bias, p_scale)


def _gdn_chunk_kernel(q_ref, k_ref, v_ref, zg_ref, feat_ref, cs_ref, cst_ref, nw_ref,
                      o_ref, sfin_ref, s_ref, cb_ref, cdec_ref, cgl_ref, *, nseq):
    ch = pl.program_id(1)

    @pl.when(ch == 0)
    def _():
        s_ref[...] = jnp.zeros(s_ref.shape, F32)
        cb_ref[...] = jnp.zeros(cb_ref.shape, BF16)
        cdec_ref[...] = jnp.zeros(cdec_ref.shape, F32)
        cgl_ref[...] = jnp.zeros(cgl_ref.shape, F32)

    row = lax.broadcasted_iota(jnp.int32, (CHUNK, CHUNK), 0)
    col = lax.broadcasted_iota(jnp.int32, (CHUNK, CHUNK), 1)
    incl = row >= col
    strict = row > col
    row_c = lax.broadcasted_iota(jnp.int32, (HALF, CHUNK), 0)
    col_c = lax.broadcasted_iota(jnp.int32, (HALF, CHUNK), 1)
    lo_c = col_c < HALF
    eye_c = (jnp.bitwise_and(col_c, HALF - 1) == row_c).astype(F32)
    zero_c = jnp.zeros((HALF, CHUNK), BF16)

    def bdiag(x):
        return jnp.concatenate([jnp.where(lo_c, x, zero_c), jnp.where(lo_c, zero_c, x)], axis=0)

    nw = nw_ref[...]
    chains = [(j, h) for j in range(nseq) for h in range(GDN_HEADS)]
    nchain = len(chains)
    Q, QD, K, KB, KD, VB, KBE = range(7)

    def lanes(h):
        return slice(h * GDN_DK, (h + 1) * GDN_DK)

    kbf = [cb_ref[K, i] for i in range(nchain)]
    decay = [cdec_ref[i] for i in range(nchain)]
    a = [jnp.where(strict, _dot_nt(cb_ref[KB, i], y) * d, 0.0)
         for i, (y, d) in enumerate(zip(kbf, decay))]
    p = [jnp.where(lo_c, -x[:HALF], -x[HALF:]) for x in a]
    inv = [eye_c + x for x in p]
    r_p = [bdiag(x.astype(BF16)) for x in p]
    for _ in range(5):
        p = [_dot(x, r) for x, r in zip(p, r_p)]
        r_p = [bdiag(x.astype(BF16)) for x in p]
        inv = [x + _dot(x, r) for x, r in zip(inv, r_p)]
    invb = [x.astype(BF16) for x in inv]
    r_a21 = [jnp.concatenate([zero_c, jnp.where(lo_c, x[HALF:].astype(BF16), zero_c)], axis=0)
             for x in a]
    fold = [_dot(x, r) for x, r in zip(invb, r_a21)]
    r_i11 = [jnp.concatenate([jnp.where(lo_c, x, zero_c), zero_c], axis=0) for x in invb]
    fold = [_dot(x, r) for x, r in zip(fold, r_i11)]
    t_inv = [jnp.concatenate([jnp.where(lo_c, x, 0.0), jnp.where(lo_c, -y, x)], axis=0)
             for x, y in zip(inv, fold)]
    uw = [_dot(t, jnp.concatenate([cb_ref[VB, i], cb_ref[KBE, i]], axis=1))
          for i, t in enumerate(t_inv)]
    attn = [_dot_nt(cb_ref[Q, i], y) * d for i, (y, d) in enumerate(zip(kbf, decay))]
    s = [s_ref[i] for i in range(nchain)]
    ws_qs = [_dot(jnp.concatenate([x[:, GDN_DK:].astype(BF16), cb_ref[QD, i]], axis=0), z)
             for i, (x, z) in enumerate(zip(uw, s))]
    v_new = [(x[:, :GDN_DK] - y[:CHUNK]).astype(BF16) for x, y in zip(uw, ws_qs)]
    o = [y[CHUNK:] + _dot(x, z) for y, x, z in zip(ws_qs, attn, v_new)]
    for i, (j, h) in enumerate(chains):
        s_ref[i] = s[i] * cgl_ref[i, 0:1, :] + _dot_tn(cb_ref[KD, i], v_new[i])
        on = o[i] * lax.rsqrt(jnp.mean(o[i] * o[i], -1, keepdims=True) + EPS)
        o_ref[j, :, lanes(h)] = (on * nw * zg_ref[j, :, lanes(h)].astype(F32)).astype(o_ref.dtype)

    def unit(x, scale):
        return x * (lax.rsqrt(jnp.sum(x * x, -1, keepdims=True) + EPS) * scale)

    for i, (j, h) in enumerate(chains):
        q = unit(q_ref[j, :, lanes(h)].astype(F32), GDN_DK ** -0.5)
        k = unit(k_ref[j, :, lanes(h)].astype(F32), 1.0)
        beta = feat_ref[j, :, FEAT_BETA + h:FEAT_BETA + h + 1]
        gc = cs_ref[j, :, FEAT_G + h:FEAT_G + h + 1]
        gr = cst_ref[j, 0, FEAT_G + h:FEAT_G + h + 1, :]
        g_last = gc[CHUNK - 1:CHUNK, :]
        eg = jnp.exp(gc)
        kb = k * beta
        cdec_ref[i] = jnp.exp(jnp.where(incl, gc - gr, NEG_BIG))
        cgl_ref[i] = jnp.broadcast_to(jnp.exp(g_last), (SUBLANES, GDN_DK))
        cb_ref[Q, i] = q.astype(BF16)
        cb_ref[QD, i] = (q * eg).astype(BF16)
        cb_ref[K, i] = k.astype(BF16)
        cb_ref[KB, i] = kb.astype(BF16)
        cb_ref[KD, i] = (k * jnp.exp(g_last - gc)).astype(BF16)
        cb_ref[VB, i] = (v_ref[j, :, lanes(h)].astype(F32) * beta).astype(BF16)
        cb_ref[KBE, i] = (kb * eg).astype(BF16)

    @pl.when(ch == pl.num_programs(1) - 1)
    def _():
        for j in range(nseq):
            sfin_ref[j] = s_ref[j * GDN_HEADS:(j + 1) * GDN_HEADS]


GDN_SEQ_PER_STEP = 2


def _gdn_chunk(qkv_act, z_act, feat, cs, cst, norm_w):
    bsz, t_pad, _ = qkv_act.shape
    nch = t_pad // CHUNK
    nseq = GDN_SEQ_PER_STEP if bsz % GDN_SEQ_PER_STEP == 0 else 1
    nchain = nseq * GDN_HEADS

    def cur(c):
        return jnp.minimum(c, nch - 1)

    def prev(c):
        return jnp.maximum(c - 1, 0)

    def qkv_spec(part):
        return pl.BlockSpec((nseq, CHUNK, GDN_QK), lambda b, c: (b, cur(c), part))

    return pl.pallas_call(
        functools.partial(_gdn_chunk_kernel, nseq=nseq),
        grid=(bsz // nseq, nch + 1),
        in_specs=[qkv_spec(0), qkv_spec(1), qkv_spec(2),
                  pl.BlockSpec((nseq, CHUNK, D_GDN), lambda b, c: (b, prev(c), 0)),
                  pl.BlockSpec((nseq, CHUNK, LANES), lambda b, c: (b, cur(c), 0)),
                  pl.BlockSpec((nseq, CHUNK, LANES), lambda b, c: (b, cur(c), 0)),
                  pl.BlockSpec((nseq, 1, CHUNK, LANES), lambda b, c: (b, cur(c), 0, 0)),
                  pl.BlockSpec((1, GDN_DK), lambda b, c: (0, 0))],
        out_specs=[pl.BlockSpec((nseq, CHUNK, D_GDN), lambda b, c: (b, prev(c), 0)),
                   pl.BlockSpec((nseq, GDN_HEADS, GDN_DK, GDN_DK), lambda b, c: (b, 0, 0, 0))],
        out_shape=[jax.ShapeDtypeStruct((bsz, t_pad, D_GDN), ACT_DTYPE),
                   jax.ShapeDtypeStruct((bsz, GDN_HEADS, GDN_DK, GDN_DK), F32)],
        scratch_shapes=[pltpu.VMEM((nchain, GDN_DK, GDN_DK), F32),
                        pltpu.VMEM((7, nchain, CHUNK, GDN_DK), BF16),
                        pltpu.VMEM((nchain, CHUNK, CHUNK), F32),
                        pltpu.VMEM((nchain, SUBLANES, GDN_DK), F32)],
        compiler_params=_cparams("arbitrary", "arbitrary"),
        name="gdn_chunk",
    )(qkv_act, qkv_act, qkv_act, z_act, feat, cs, cst, norm_w.reshape(1, GDN_DK))


def _ssd_finish(y, zs, nw):
    yg = y * zs
    gw = D_SSM // SSM_GROUPS
    parts = []
    for g in range(SSM_GROUPS):
        blk = yg[:, g * gw:(g + 1) * gw]
        parts.append(blk * lax.rsqrt(jnp.mean(blk * blk, -1, keepdims=True) + EPS))
    return jnp.concatenate(parts, axis=1) * nw


def _ssd_chunk_kernel(x_ref, b_ref, c_ref, zs_ref, featt_ref, cs_ref, cst_ref, d_ref, nw_ref,
                      y_ref, sfin_ref, s_ref, *, nseq):
    ch = pl.program_id(1)

    @pl.when(ch == 0)
    def _():
        s_ref[...] = jnp.zeros(s_ref.shape, F32)

    row = lax.broadcasted_iota(jnp.int32, (CHUNK, CHUNK), 0)
    col = lax.broadcasted_iota(jnp.int32, (CHUNK, CHUNK), 1)
    upper = col >= row
    top = row < SSM_HEAD_DIM
    heads_per_group = SSM_HEADS // SSM_GROUPS
    npair = SSM_HEADS // 2
    ecs, edl = [], []
    for j in range(nseq):
        cst = cst_ref[j, 0]
        ecs.append(jnp.exp(cst))
        edl.append(jnp.exp(cst[:, CHUNK - 1:CHUNK] - cst))
    grp_slices = [slice(g * SSM_STATE, (g + 1) * SSM_STATE) for g in range(SSM_GROUPS)]
    cbt = [[_dot_nt(b_ref[j, :, gs], c_ref[j, :, gs]) for gs in grp_slices] for j in range(nseq)]
    chains = [(j, p) for j in range(nseq) for p in range(npair)]

    def grp_of(p):
        return grp_slices[(2 * p) // heads_per_group]

    def rows2(tile, base, p):
        return jnp.where(top, tile[base + 2 * p:base + 2 * p + 1, :],
                         tile[base + 2 * p + 1:base + 2 * p + 2, :])

    def decay_t(j, h):
        ac_row = cst_ref[j, 0, FEAT_LA + h:FEAT_LA + h + 1, :]
        ac_col = cs_ref[j, :, FEAT_LA + h:FEAT_LA + h + 1]
        return jnp.exp(jnp.where(upper, ac_row - ac_col, NEG_BIG))

    xt = [x_ref[j, :, p * LANES:(p + 1) * LANES].astype(F32).T for j, p in chains]
    xdt = [x * rows2(featt_ref[j, 0], FEAT_DT, p) for x, (j, p) in zip(xt, chains)]
    st = [jnp.concatenate([cbt[j][(2 * p) // heads_per_group] * decay_t(j, 2 * p),
                           cbt[j][(2 * p) // heads_per_group] * decay_t(j, 2 * p + 1)], axis=0)
          for j, p in chains]
    lhs = [jnp.concatenate([jnp.where(top, x, 0.0), jnp.where(top, 0.0, x)], axis=1) for x in xdt]
    y_diag = [_dot(l, r) for l, r in zip(lhs, st)]
    s = [s_ref[i * LANES:(i + 1) * LANES, :] for i in range(len(chains))]
    y_off = [_dot_nt(z, c_ref[j, :, grp_of(p)]) * rows2(ecs[j], FEAT_LA, p)
             for z, (j, p) in zip(s, chains)]
    ds = [_dot(x * rows2(edl[j], FEAT_LA, p), b_ref[j, :, grp_of(p)])
          for x, (j, p) in zip(xdt, chains)]
    for i, (j, p) in enumerate(chains):
        last = rows2(ecs[j], FEAT_LA, p)[:, CHUNK - 1:CHUNK]
        s_ref[i * LANES:(i + 1) * LANES, :] = s[i] * last + ds[i]
    yt = [a + b + d_ref[p * LANES:(p + 1) * LANES, :] * x
          for a, b, x, (j, p) in zip(y_diag, y_off, xt, chains)]
    for j in range(nseq):
        y = jnp.concatenate([yt[j * npair + p].T for p in range(npair)], axis=1)
        y_ref[j] = _ssd_finish(y, zs_ref[j].astype(F32), nw_ref[...]).astype(y_ref.dtype)

    @pl.when(ch == pl.num_programs(1) - 1)
    def _():
        rows = SSM_HEADS * SSM_HEAD_DIM
        for j in range(nseq):
            sfin_ref[j] = s_ref[j * rows:(j + 1) * rows, :]


SSD_SEQ_PER_STEP = 2


def _ssd_chunk(xbc_act, z_act, featt, cs, cst, d_rows, norm_w):
    bsz, t_pad, _ = xbc_act.shape
    nch = t_pad // CHUNK
    bc_w = SSM_GROUPS * SSM_STATE
    nseq = SSD_SEQ_PER_STEP if bsz % SSD_SEQ_PER_STEP == 0 else 1
    rows = SSM_HEADS * SSM_HEAD_DIM
    return pl.pallas_call(
        functools.partial(_ssd_chunk_kernel, nseq=nseq),
        grid=(bsz // nseq, nch),
        in_specs=[pl.BlockSpec((nseq, CHUNK, D_SSM), lambda b, c: (b, c, 0)),
                  pl.BlockSpec((nseq, CHUNK, bc_w), lambda b, c: (b, c, D_SSM // bc_w)),
                  pl.BlockSpec((nseq, CHUNK, bc_w), lambda b, c: (b, c, D_SSM // bc_w + 1)),
                  pl.BlockSpec((nseq, CHUNK, D_SSM), lambda b, c: (b, c, 1)),
                  pl.BlockSpec((nseq, 1, CHUNK, LANES), lambda b, c: (b, c, 0, 0)),
                  pl.BlockSpec((nseq, CHUNK, LANES), lambda b, c: (b, c, 0)),
                  pl.BlockSpec((nseq, 1, CHUNK, LANES), lambda b, c: (b, c, 0, 0)),
                  pl.BlockSpec((rows, LANES), lambda b, c: (0, 0)),
                  pl.BlockSpec((1, D_SSM), lambda b, c: (0, 0))],
        out_specs=[pl.BlockSpec((nseq, CHUNK, D_SSM), lambda b, c: (b, c, 0)),
                   pl.BlockSpec((nseq, rows, SSM_STATE), lambda b, c: (b, 0, 0))],
        out_shape=[jax.ShapeDtypeStruct((bsz, t_pad, D_SSM), ACT_DTYPE),
                   jax.ShapeDtypeStruct((bsz, rows, SSM_STATE), F32)],
        scratch_shapes=[pltpu.VMEM((nseq * rows, SSM_STATE), F32)],
        compiler_params=_cparams("arbitrary", "arbitrary"),
        name="ssd_chunk",
    )(xbc_act, xbc_act, xbc_act, z_act, featt, cs, cst, d_rows, norm_w.reshape(1, D_SSM))


def _out_proj_kernel(mg_ref, ms_ref, x_ref, w_ref, nw_ref, *out_refs, seq_rows):
    ho_ref = out_refs[-1]
    xo_ref = out_refs[0] if len(out_refs) == 2 else None
    acc = (jnp.dot(mg_ref[...].astype(BF16), w_ref[0:D_GDN, :], preferred_element_type=F32)
           + jnp.dot(ms_ref[...].astype(BF16), w_ref[D_GDN:, :], preferred_element_type=F32))
    xn = x_ref[...] + acc
    if seq_rows:
        tile = xn.shape[0]
        start = lax.rem(pl.program_id(0) * tile, seq_rows)
        r = start + lax.broadcasted_iota(jnp.int32, xn.shape, 0)
        pad = (r < FRONT_PAD) | ((r >= seq_rows) & (r < seq_rows + FRONT_PAD))
        xn = jnp.where(pad, 0.0, xn)
    if xo_ref is not None:
        xo_ref[...] = xn
    hn = xn * lax.rsqrt(jnp.mean(xn * xn, -1, keepdims=True) + EPS) * nw_ref[...]
    ho_ref[...] = hn.astype(ho_ref.dtype)


def _out_proj(mix_g, mix_s, x, w_out, next_norm_w, h_dtype, tile, seq_rows, keep_residual):
    rows = x.shape[0]
    row_spec = pl.BlockSpec((tile, D_MODEL), lambda i: (i, 0))
    out_specs = [row_spec, row_spec]
    out_shape = [jax.ShapeDtypeStruct((rows, D_MODEL), F32),
                 jax.ShapeDtypeStruct((rows, D_MODEL), h_dtype)]
    if not keep_residual:
        out_specs, out_shape = out_specs[1:], out_shape[1:]
    return pl.pallas_call(
        functools.partial(_out_proj_kernel, seq_rows=seq_rows),
        grid=(rows // tile,),
        in_specs=[row_spec, row_spec, row_spec,
                  pl.BlockSpec((D_GDN + D_SSM, D_MODEL), lambda i: (0, 0)),
                  pl.BlockSpec((1, D_MODEL), lambda i: (0, 0))],
        out_specs=out_specs,
        out_shape=out_shape,
        compiler_params=_cparams("arbitrary"),
        name="out_proj",
    )(mix_g, mix_s, x, w_out, next_norm_w.reshape(1, D_MODEL))


def _proj_conv_decode_kernel(h_ref, w_ref, buf_ref, cw_ref, cb_ref, out_ref, nbuf_ref):
    z = jnp.dot(h_ref[...], w_ref[...], preferred_element_type=F32)
    cw = cw_ref[...]
    acc = cb_ref[...] + cw[3:4] * z
    for tap in range(CONV_W - 1):
        acc = acc + cw[tap:tap + 1] * buf_ref[tap]
    out_ref[...] = _silu_of_twice(acc)
    nbuf_ref[0] = buf_ref[1]
    nbuf_ref[1] = buf_ref[2]
    nbuf_ref[2] = z


def _proj_silu_decode_kernel(h_ref, w_ref, out_ref):
    out_ref[...] = _silu_of_twice(jnp.dot(h_ref[...], w_ref[...], preferred_element_type=F32))


def _proj_conv_decode(h, w, buf, cw, cb):
    rows = h.shape[0]
    ncols = w.shape[1]
    return pl.pallas_call(
        _proj_conv_decode_kernel,
        grid=(ncols // COL_BLOCK,),
        in_specs=[pl.BlockSpec((rows, D_MODEL), lambda j: (0, 0)),
                  pl.BlockSpec((D_MODEL, COL_BLOCK), lambda j: (0, j)),
                  pl.BlockSpec((CONV_W - 1, rows, COL_BLOCK), lambda j: (0, 0, j)),
                  pl.BlockSpec((CONV_W, COL_BLOCK), lambda j: (0, j)),
                  pl.BlockSpec((1, COL_BLOCK), lambda j: (0, j))],
        out_specs=[pl.BlockSpec((rows, COL_BLOCK), lambda j: (0, j)),
                   pl.BlockSpec((CONV_W - 1, rows, COL_BLOCK), lambda j: (0, 0, j))],
        out_shape=[jax.ShapeDtypeStruct((rows, ncols), F32),
                   jax.ShapeDtypeStruct((CONV_W - 1, rows, ncols), F32)],
        compiler_params=_cparams("arbitrary"),
        name="proj_conv_decode",
    )(h, w, buf, cw, cb.reshape(1, ncols))


def _proj_silu_decode(h, w):
    rows = h.shape[0]
    ncols = w.shape[1]
    return pl.pallas_call(
        _proj_silu_decode_kernel,
        grid=(ncols // COL_BLOCK,),
        in_specs=[pl.BlockSpec((rows, D_MODEL), lambda j: (0, 0)),
                  pl.BlockSpec((D_MODEL, COL_BLOCK), lambda j: (0, j))],
        out_specs=pl.BlockSpec((rows, COL_BLOCK), lambda j: (0, j)),
        out_shape=jax.ShapeDtypeStruct((rows, ncols), F32),
        compiler_params=_cparams("arbitrary"),
        name="proj_silu_decode",
    )(h, w)


TOK_BLOCK = 8


def _gdn_step_kernel(qt_ref, kt_ref, v_ref, zg_ref, feat_ref, nw_ref, s_ref, *rest):
    o_ref, so_ref = rest[-2:]
    feat = feat_ref[...]
    vrows = v_ref[...]
    for h in range(GDN_HEADS):
        sl = slice(h * GDN_DK, (h + 1) * GDN_DK)
        qt = qt_ref[0, h]
        kt = kt_ref[0, h]
        qt = qt * (lax.rsqrt(jnp.sum(qt * qt, 0, keepdims=True) + EPS) * (GDN_DK ** -0.5))
        kt = kt * lax.rsqrt(jnp.sum(kt * kt, 0, keepdims=True) + EPS)
        for j in range(TOK_BLOCK):
            kc = kt[:, j:j + 1]
            qc = qt[:, j:j + 1]
            beta = feat[j:j + 1, FEAT_BETA + h:FEAT_BETA + h + 1]
            g = feat[j:j + 1, FEAT_G + h:FEAT_G + h + 1]
            s = s_ref[j, h] * jnp.exp(g)
            kv = jnp.sum(s * kc, 0, keepdims=True)
            d = (vrows[j:j + 1, sl] - kv) * beta
            s = s + kc * d
            so_ref[j, h] = s
            o_ref[j:j + 1, sl] = jnp.sum(s * qc, 0, keepdims=True)
    nw = nw_ref[...]
    for h in range(GDN_HEADS):
        sl = slice(h * GDN_DK, (h + 1) * GDN_DK)
        o = o_ref[:, sl]
        o_ref[:, sl] = o * lax.rsqrt(jnp.mean(o * o, -1, keepdims=True) + EPS) * nw * zg_ref[:, sl]


def _layer_state_call(kernel_fn, name, layer, states, acc, in_specs, out_specs, out_shapes, args):
    rows = states.shape[1]
    blk = (None, TOK_BLOCK) + states.shape[2:]
    zeros = (0,) * (states.ndim - 2)
    state_spec = pl.BlockSpec(blk, lambda i: (layer, i) + zeros)
    in_specs = in_specs + [state_spec]
    args = args + [states]
    aliases = {}
    if acc is not None:
        in_specs = in_specs + [pl.BlockSpec(memory_space=pl.ANY)]
        args = args + [acc]
        aliases = {len(args) - 1: len(out_shapes)}
    return pl.pallas_call(
        kernel_fn,
        grid=(rows // TOK_BLOCK,),
        in_specs=in_specs,
        out_specs=out_specs + [state_spec],
        out_shape=out_shapes + [jax.ShapeDtypeStruct(states.shape, F32)],
        input_output_aliases=aliases,
        compiler_params=_cparams("arbitrary"),
        name=name,
    )(*args)


def _gdn_step(layer, qt, kt, qkv_act, z_act, feat, norm_w, states, acc):
    rows = qkv_act.shape[0]
    return _layer_state_call(
        _gdn_step_kernel, "gdn_step", layer, states, acc,
        [pl.BlockSpec((1, GDN_HEADS, GDN_DK, TOK_BLOCK), lambda i: (i, 0, 0, 0)),
         pl.BlockSpec((1, GDN_HEADS, GDN_DK, TOK_BLOCK), lambda i: (i, 0, 0, 0)),
         pl.BlockSpec((TOK_BLOCK, D_GDN), lambda i: (i, 2)),
         pl.BlockSpec((TOK_BLOCK, D_GDN), lambda i: (i, 0)),
         pl.BlockSpec((TOK_BLOCK, LANES), lambda i: (i, 0)),
         pl.BlockSpec((1, GDN_DK), lambda i: (0, 0))],
        [pl.BlockSpec((TOK_BLOCK, D_GDN), lambda i: (i, 0))],
        [jax.ShapeDtypeStruct((rows, D_GDN), F32)],
        [qt, kt, qkv_act, z_act, feat, norm_w.reshape(1, GDN_DK)])


SSD_STEP_HEADS = 4


def _ssd_step_kernel(xt_ref, b_ref, c_ref, feat_ref, s_ref, *rest):
    yt_ref, so_ref = rest[-2:]
    feat = feat_ref[...]
    heads_per_group = SSM_HEADS // SSM_GROUPS
    ela = jnp.exp(feat)
    for h0 in range(0, SSM_HEADS, SSD_STEP_HEADS):
        hj = [(h, j) for h in range(h0, h0 + SSD_STEP_HEADS) for j in range(TOK_BLOCK)]

        def grp_lanes(h):
            grp = h // heads_per_group
            return slice(grp * SSM_STATE, (grp + 1) * SSM_STATE)

        xdt = [xt_ref[0, h, :, j:j + 1] * feat[j:j + 1, FEAT_DT + h:FEAT_DT + h + 1] for h, j in hj]
        xb = [jnp.broadcast_to(x, (SSM_HEAD_DIM, SSM_STATE)) for x in xdt]
        s = [s_ref[j, h] * ela[j:j + 1, FEAT_LA + h:FEAT_LA + h + 1] + x * b_ref[j:j + 1, grp_lanes(h)]
             for (h, j), x in zip(hj, xb)]
        for (h, j), x in zip(hj, s):
            so_ref[j, h] = x
        y = [jnp.sum(x * c_ref[j:j + 1, grp_lanes(h)], -1, keepdims=True) for (h, j), x in zip(hj, s)]
        for i in range(SSD_STEP_HEADS):
            yt_ref[0, h0 + i] = jnp.concatenate(y[i * TOK_BLOCK:(i + 1) * TOK_BLOCK], axis=1)


def _ssd_step(layer, xt, xbc_act, feat, states, acc):
    rows = xbc_act.shape[0]
    ngrp = rows // TOK_BLOCK
    bc_w = SSM_GROUPS * SSM_STATE
    return _layer_state_call(
        _ssd_step_kernel, "ssd_step", layer, states, acc,
        [pl.BlockSpec((1, SSM_HEADS, SSM_HEAD_DIM, TOK_BLOCK), lambda i: (i, 0, 0, 0)),
         pl.BlockSpec((TOK_BLOCK, bc_w), lambda i: (i, D_SSM // bc_w)),
         pl.BlockSpec((TOK_BLOCK, bc_w), lambda i: (i, D_SSM // bc_w + 1)),
         pl.BlockSpec((TOK_BLOCK, LANES), lambda i: (i, 0))],
        [pl.BlockSpec((1, SSM_HEADS, SSM_HEAD_DIM, TOK_BLOCK), lambda i: (i, 0, 0, 0))],
        [jax.ShapeDtypeStruct((ngrp, SSM_HEADS, SSM_HEAD_DIM, TOK_BLOCK), F32)],
        [xt, xbc_act, xbc_act, feat])


def _ssd_finish_decode_kernel(y_ref, x_ref, zs_ref, d_ref, nw_ref, out_ref):
    out_ref[...] = _ssd_finish(y_ref[...] + d_ref[...] * x_ref[...], zs_ref[...], nw_ref[...])


def _ssd_finish_decode(y, xbc_act, z_act, d_cols, norm_w):
    rows = y.shape[0]
    return pl.pallas_call(
        _ssd_finish_decode_kernel,
        grid=(1,),
        in_specs=[pl.BlockSpec((rows, D_SSM), lambda i: (0, 0)),
                  pl.BlockSpec((rows, D_SSM), lambda i: (0, 0)),
                  pl.BlockSpec((rows, D_SSM), lambda i: (0, 1)),
                  pl.BlockSpec((1, D_SSM), lambda i: (0, 0)),
                  pl.BlockSpec((1, D_SSM), lambda i: (0, 0))],
        out_specs=pl.BlockSpec((rows, D_SSM), lambda i: (0, 0)),
        out_shape=jax.ShapeDtypeStruct((rows, D_SSM), F32),
        compiler_params=_cparams("arbitrary"),
        name="ssd_finish_decode",
    )(y, xbc_act, z_act, d_cols, norm_w.reshape(1, D_SSM))


def _prepare_weights(w_in, gdn_A_log, gdn_dt_bias, ssm_A_log, ssm_dt_bias, ssm_D, w_out):
    depth = w_in.shape[0]
    w_qkv = w_in[..., :OFF_Z_G].astype(BF16)
    w_xbc = w_in[..., OFF_XBC:OFF_Z_S].astype(BF16)
    w_z = (0.5 * jnp.concatenate([w_in[..., OFF_Z_G:OFF_B], w_in[..., OFF_Z_S:OFF_DT]], axis=-1)
           ).astype(BF16)
    w_dt = w_in[..., OFF_DT:]
    w_small = jnp.concatenate(
        [w_in[..., OFF_B:OFF_XBC], w_dt, w_dt, jnp.zeros((depth, D_MODEL, LANES - FEAT_END), F32)],
        axis=-1).astype(BF16)
    zeros8 = jnp.zeros((depth, GDN_HEADS), F32)
    tail = jnp.zeros((depth, LANES - FEAT_END), F32)
    p_bias = jnp.concatenate([zeros8, gdn_dt_bias.astype(F32), ssm_dt_bias.astype(F32),
                              ssm_dt_bias.astype(F32), tail], axis=-1).reshape(depth, 1, LANES)
    p_scale = jnp.concatenate([zeros8, -jnp.exp(gdn_A_log.astype(F32)),
                               jnp.ones((depth, SSM_HEADS), F32), -jnp.exp(ssm_A_log.astype(F32)),
                               tail], axis=-1).reshape(depth, 1, LANES)
    d_cols = jnp.repeat(ssm_D.astype(F32), SSM_HEAD_DIM, axis=-1).reshape(depth, 1, D_SSM)
    return w_qkv, w_xbc, w_z, w_small, p_bias, p_scale, d_cols, w_out.astype(BF16)


def kernel(x_prompt, x_sample, state_gdn, state_gdn_conv, state_ssm, state_ssm_conv, meta_tokens,
           norm_w, w_in, gdn_conv_w, gdn_A_log, gdn_dt_bias, gdn_norm_w, ssm_conv_w, ssm_conv_b,
           ssm_A_log, ssm_dt_bias, ssm_D, ssm_norm_w, w_out, final_norm_w):
    bp, seq, _ = x_prompt.shape
    bd = x_sample.shape[0]
    t_pad = FRONT_PAD + N_META + seq
    xp, hp = _embed_prompt(x_prompt.astype(F32), meta_tokens.astype(F32), norm_w[0].astype(F32))
    xs = x_sample.astype(F32).reshape(bd, D_MODEL)
    hs = _rmsnorm(xs, norm_w[0], bd)
    zero_bias = jnp.zeros((GDN_CONV_DIM,), F32)
    ngrp = bd // TOK_BLOCK
    p_g, p_gc, p_s, p_sc = [], [], [], []
    d_g, d_s = None, None
    d_gc, d_sc = [], []
    prepared = _prepare_weights(w_in, gdn_A_log, gdn_dt_bias, ssm_A_log, ssm_dt_bias, ssm_D, w_out)
    for l in range(DEPTH):
        w_qkv, w_xbc, w_z, w_small, p_bias, p_scale, d_cols, w_o = [a[l] for a in prepared]
        last = l == DEPTH - 1
        nxt_w = final_norm_w if last else norm_w[l + 1]
        h_dtype = F32 if last else BF16
        gcw = 0.5 * gdn_conv_w[l].astype(F32)
        scw = 0.5 * ssm_conv_w[l].astype(F32)
        scb = 0.5 * ssm_conv_b[l].astype(F32)

        h3 = hp.reshape(bp, t_pad, D_MODEL)
        qkv_act, qkv_tail = _proj_conv(h3, w_qkv, gcw, None)
        xbc_act, xbc_tail = _proj_conv(h3, w_xbc, scw, scb)
        z_act = _proj_silu(h3, w_z)
        feat, featt, cs, cst = _feat_prompt(h3, w_small, p_bias, p_scale)
        d_rows = jnp.broadcast_to(d_cols.reshape(D_SSM, 1), (D_SSM, LANES))
        mix_g, sg = _gdn_chunk(qkv_act, z_act, feat, cs, cst, gdn_norm_w[l].astype(F32))
        mix_s, ss = _ssd_chunk(xbc_act, z_act, featt, cs, cst, d_rows, ssm_norm_w[l].astype(F32))
        outs = _out_proj(mix_g.reshape(bp * t_pad, D_GDN), mix_s.reshape(bp * t_pad, D_SSM), xp,
                         w_o, nxt_w.astype(F32), h_dtype, ROW_TILE, t_pad, not last)
        xp, hp = outs[0], outs[-1]
        p_g.append(sg)
        p_gc.append(qkv_tail[:, SUBLANES - (CONV_W - 1):, :])
        p_s.append(ss.reshape(bp, SSM_HEADS, SSM_HEAD_DIM, SSM_STATE))
        p_sc.append(xbc_tail[:, SUBLANES - (CONV_W - 1):, :])

        gbuf = jnp.transpose(state_gdn_conv[l].astype(F32), (1, 0, 2))
        sbuf = jnp.transpose(state_ssm_conv[l].astype(F32), (1, 0, 2))
        qkv_d, gbuf_n = _proj_conv_decode(hs, w_qkv, gbuf, gcw, zero_bias)
        xbc_d, sbuf_n = _proj_conv_decode(hs, w_xbc, sbuf, scw, scb)
        z_d = _proj_silu_decode(hs, w_z)
        feat_d = _feat_decode(hs, w_small, p_bias, p_scale)

        def head_major(a, heads, dim):
            return jnp.transpose(a.reshape(ngrp, TOK_BLOCK, heads, dim), (0, 2, 3, 1))

        qt = head_major(qkv_d[:, :GDN_QK], GDN_HEADS, GDN_DK)
        kt = head_major(qkv_d[:, GDN_QK:2 * GDN_QK], GDN_HEADS, GDN_DK)
        mix_gd, d_g = _gdn_step(l, qt, kt, qkv_d, z_d, feat_d, gdn_norm_w[l].astype(F32),
                                state_gdn.astype(F32), d_g)
        xt = head_major(xbc_d[:, :D_SSM], SSM_HEADS, SSM_HEAD_DIM)
        yt, d_s = _ssd_step(l, xt, xbc_d, feat_d, state_ssm.astype(F32), d_s)
        y_d = jnp.transpose(yt, (0, 3, 1, 2)).reshape(bd, D_SSM)
        mix_sd = _ssd_finish_decode(y_d, xbc_d, z_d, d_cols, ssm_norm_w[l].astype(F32))
        outs = _out_proj(mix_gd, mix_sd, xs, w_o, nxt_w.astype(F32), h_dtype, bd, 0, not last)
        xs, hs = outs[0], outs[-1]
        d_gc.append(jnp.transpose(gbuf_n, (1, 0, 2)))
        d_sc.append(jnp.transpose(sbuf_n, (1, 0, 2)))

    y_prompt = hp.reshape(bp, t_pad, D_MODEL)[:, FRONT_PAD + N_META:]
    y_sample = hs.reshape(bd, 1, D_MODEL)
    return (y_prompt, y_sample,
            jnp.stack(p_g), jnp.stack(p_gc), jnp.stack(p_s), jnp.stack(p_sc),
            d_g, jnp.stack(d_gc), d_s, jnp.stack(d_sc))
```
